```python
import jax
import jax.numpy as jnp
from jax import lax
import numpy as np

D_MODEL = 2048
BATCH = 4
SEQ = 2048
DEPTH = 4
DEC_BATCH = 32
DEC_SEQ = 1
PAST_LEN = 16384
PAGE_SIZE = 128

N_MIXERS = 3
LAYER_MIXER = tuple(i % N_MIXERS for i in range(DEPTH))
LAYER_SLOT = tuple(LAYER_MIXER[:i].count(LAYER_MIXER[i]) for i in range(DEPTH))
N_A = LAYER_MIXER.count(0)
N_B = LAYER_MIXER.count(1)
N_C = LAYER_MIXER.count(2)
EPS = 1e-6

D_RNN = 2560
LRU_BLOCKS = 16
LRU_BLOCK = D_RNN // LRU_BLOCKS
CONV_A = 4
LRU_C = 8.0

N_HEADS = 32
N_KV = 4
HEAD_DIM = 64
GROUP = N_HEADS // N_KV
WINDOW = 128
Q_BLOCK = 128
ROT_DIM = HEAD_DIM // 4
ROPE_THETA = 500000.0

HG_HEADS = 16
HG_DK = 128
HG_DV = D_MODEL // HG_HEADS
HG_CHUNK = 64

D_FF = 3 * D_MODEL
CONV_F = 3

kernel_name = "hybrid_rglru_swa_hgrn2_convffn_step"

F32 = jnp.float32


def rms_norm(x, g):
    xf = x.astype(F32)
    y = xf * lax.rsqrt(jnp.mean(xf * xf, axis=-1, keepdims=True) + EPS)
    return (y * g.astype(F32)).astype(x.dtype)


def causal_dwconv(x, buf, w, b):
    width, T = w.shape[0], x.shape[1]
    ext = jnp.concatenate([buf.astype(x.dtype), x], axis=1)
    y = b + ext[:, width - 1:width - 1 + T] * w[width - 1]
    for j in range(width - 1):
        y = y + ext[:, j:j + T] * w[j]
    return y, ext[:, T:]


def rglru(x, h0, w_a, b_a, w_i, b_i, lam):
    B, T, _ = x.shape
    xb = x.reshape(B, T, LRU_BLOCKS, LRU_BLOCK)
    r = jax.nn.sigmoid(jnp.einsum('btnc,ncd->btnd', xb, w_a).reshape(B, T, D_RNN).astype(F32) + b_a.astype(F32))
    i = jax.nn.sigmoid(jnp.einsum('btnc,ncd->btnd', xb, w_i).reshape(B, T, D_RNN).astype(F32) + b_i.astype(F32))
    log_a = -LRU_C * r * jax.nn.softplus(-lam.astype(F32))
    a = jnp.exp(log_a)
    u = jnp.sqrt(-jnp.expm1(2.0 * log_a)) * (i * x.astype(F32))
    u = u.at[:, 0].add(a[:, 0] * h0.astype(F32))

    def combine(left, right):
        a1, b1 = left
        a2, b2 = right
        return a1 * a2, a2 * b1 + b2

    _, h = lax.associative_scan(combine, (a, u), axis=1)
    return h.astype(x.dtype), h[:, -1].astype(x.dtype)


def mixer_lru(x, h0, conv_buf, w_in, conv_w, conv_b, w_a, b_a, w_i, b_i, lam, w_out):
    gate, xr = jnp.split(x @ w_in, 2, axis=-1)
    xr, new_buf = causal_dwconv(xr, conv_buf, conv_w, conv_b)
    h, h_last = rglru(xr, h0, w_a, b_a, w_i, b_i, lam)
    y = (jax.nn.gelu(gate) * h) @ w_out
    return y, h_last, new_buf


def rope_partial(x, pos):
    half = ROT_DIM // 2
    inv = ROPE_THETA ** (-jnp.arange(half, dtype=F32) * (2.0 / ROT_DIM))
    ang = pos[:, None] * inv[None, :]
    cos = jnp.cos(ang)[None, :, None, :]
    sin = jnp.sin(ang)[None, :, None, :]
    xr = x[..., :ROT_DIM].astype(F32)
    x1, x2 = xr[..., :half], xr[..., half:]
    rot = jnp.concatenate([x1 * cos - x2 * sin, x2 * cos + x1 * sin], axis=-1)
    return jnp.concatenate([rot.astype(x.dtype), x[..., ROT_DIM:]], axis=-1)


def mixer_swa(x, pos0, k_buf, v_buf, w_qkv, b_qkv, sinks, w_o, b_o):
    B, T, _ = x.shape
    qkv = x @ w_qkv + b_qkv
    q, k, v = jnp.split(qkv, [N_HEADS * HEAD_DIM, (N_HEADS + N_KV) * HEAD_DIM], axis=-1)
    pos = (pos0 + jnp.arange(T)).astype(F32)
    q = rope_partial(q.reshape(B, T, N_HEADS, HEAD_DIM), pos)
    k = rope_partial(k.reshape(B, T, N_KV, HEAD_DIM), pos)
    v = v.reshape(B, T, N_KV, HEAD_DIM)
    W = k_buf.shape[1]
    k_ext = jnp.concatenate([k_buf.astype(x.dtype), k], axis=1)
    v_ext = jnp.concatenate([v_buf.astype(x.dtype), v], axis=1)
    bq = Q_BLOCK if T % Q_BLOCK == 0 else T
    nb = T // bq
    nk = W + bq
    kidx = jnp.arange(nb)[:, None] * bq + jnp.arange(nk)[None, :]
    kb = k_ext[:, kidx]
    vb = v_ext[:, kidx]
    qb = q.reshape(B, nb, bq, N_KV, GROUP, HEAD_DIM)
    q_pos = pos0 + jnp.arange(T).reshape(nb, bq)
    k_pos = pos0 - W + kidx
    rel = q_pos[:, :, None] - k_pos[:, None, :]
    valid = (rel >= 0) & (rel < WINDOW) & (k_pos[:, None, :] >= 0)
    s = jnp.einsum('bnqkgd,bnskd->bnkgqs', qb, kb, preferred_element_type=F32) * (HEAD_DIM ** -0.5)
    s = jnp.where(valid[None, :, None, None], s, -jnp.inf)
    sink = sinks.astype(F32).reshape(N_KV, GROUP)[None, None, :, :, None, None]
    m = jnp.maximum(jnp.max(s, axis=-1, keepdims=True), sink)
    p = jnp.exp(s - m)
    p = p / (jnp.sum(p, axis=-1, keepdims=True) + jnp.exp(sink - m))
    o = jnp.einsum('bnkgqs,bnskd->bnqkgd', p.astype(x.dtype), vb).reshape(B, T, N_HEADS * HEAD_DIM)
    y = o @ w_o + b_o
    return y, k_ext[:, -W:], v_ext[:, -W:]


def hgrn_chunked(q, k, log_f, v, S0):
    B, T, H, _ = q.shape
    C = HG_CHUNK if T >= HG_CHUNK else T
    Tp = -(-T // C) * C
    pad = Tp - T
    if pad:
        padw = ((0, 0), (0, pad), (0, 0), (0, 0))
        q, k, log_f, v = (jnp.pad(a, padw) for a in (q, k, log_f, v))
    nc = Tp // C

    def to_chunks(a):
        return a.reshape(B, nc, C, H, a.shape[-1]).transpose(1, 0, 3, 2, 4)

    mask = jnp.tril(jnp.ones((C, C), dtype=bool))[:, :, None]

    def step(S, inp):
        qc, kc, lfc, vc = inp
        b = jnp.cumsum(lfc, axis=2)
        diff = b[:, :, :, None, :] - b[:, :, None, :, :]
        decay = jnp.exp(jnp.where(mask, diff, -jnp.inf))
        A = jnp.einsum('bhtd,bhsd,bhtsd->bhts', qc, kc, decay)
        o = jnp.einsum('bhts,bhse->bhte', A, vc) + jnp.einsum('bhtd,bhde->bhte', qc * jnp.exp(b), S)
        b_last = b[:, :, -1, :]
        S = S * jnp.exp(b_last)[..., None] + jnp.einsum('bhsd,bhse->bhde', kc * jnp.exp(b_last[:, :, None, :] - b), vc)
        return S, o

    S, o = lax.scan(step, S0, (to_chunks(q), to_chunks(k), to_chunks(log_f), to_chunks(v)))
    o = o.transpose(1, 0, 3, 2, 4).reshape(B, Tp, H, v.shape[-1])[:, :T]
    return o, S


def mixer_hgrn2(x, S0, lb, w_in, g_norm, w_o):
    B, T, _ = x.shape
    fk = HG_HEADS * HG_DK
    fv = HG_HEADS * HG_DV
    q, z, v, g = jnp.split(x @ w_in, [fk, 2 * fk, 2 * fk + fv], axis=-1)
    zf = z.astype(F32)
    log_f = jnp.logaddexp(jnp.log(lb), jnp.log1p(-lb) + jax.nn.log_sigmoid(zf))
    k = (1.0 - lb) * jax.nn.sigmoid(-zf)
    shp_k = (B, T, HG_HEADS, HG_DK)
    shp_v = (B, T, HG_HEADS, HG_DV)
    o, S = hgrn_chunked(jax.nn.silu(q.astype(F32)).reshape(shp_k), k.reshape(shp_k), log_f.reshape(shp_k),
                        v.astype(F32).reshape(shp_v), S0.astype(F32))
    o = rms_norm(o, g_norm) * jax.nn.silu(g.astype(F32)).reshape(shp_v)
    y = o.reshape(B, T, fv).astype(x.dtype) @ w_o
    return y, S.astype(x.dtype)


def conv_ffn(x, buf, w_in, conv_w, conv_b, w_out):
    gate, up = jnp.split(x @ w_in, 2, axis=-1)
    gate, new_buf = causal_dwconv(gate, buf, conv_w, conv_b)
    return (jax.nn.gelu(gate) * up) @ w_out, new_buf


def run_trunk(x, pos0, lru_h, lru_conv, swa_k, swa_v, hg_S, ffn_conv, weights):
    (norm_mix, norm_ffn, norm_final,
     lru_w_in, lru_conv_w, lru_conv_b, lru_w_a, lru_b_a, lru_w_i, lru_b_i, lru_lambda, lru_w_out,
     swa_w_qkv, swa_b_qkv, swa_sinks, swa_w_o, swa_b_o,
     hg_w_in, hg_lb_logits, hg_norm, hg_w_o,
     ffn_w_in, ffn_conv_w, ffn_conv_b, ffn_w_out) = weights
    sm = jax.nn.softmax(hg_lb_logits.astype(F32), axis=0)
    lb_all = jnp.cumsum(sm, axis=0) - sm[0]
    n_h, n_c, n_k, n_v, n_s, n_f = [], [], [], [], [], []
    for layer in range(DEPTH):
        kind, j = LAYER_MIXER[layer], LAYER_SLOT[layer]
        hn = rms_norm(x, norm_mix[layer])
        if kind == 0:
            y, h_new, c_new = mixer_lru(hn, lru_h[j], lru_conv[j], lru_w_in[j], lru_conv_w[j], lru_conv_b[j],
                                        lru_w_a[j], lru_b_a[j], lru_w_i[j], lru_b_i[j], lru_lambda[j], lru_w_out[j])
            n_h.append(h_new)
            n_c.append(c_new)
        elif kind == 1:
            y, k_new, v_new = mixer_swa(hn, pos0, swa_k[j], swa_v[j], swa_w_qkv[j], swa_b_qkv[j], swa_sinks[j],
                                        swa_w_o[j], swa_b_o[j])
            n_k.append(k_new)
            n_v.append(v_new)
        else:
            y, s_new = mixer_hgrn2(hn, hg_S[j], lb_all[layer], hg_w_in[j], hg_norm[j], hg_w_o[j])
            n_s.append(s_new)
        x = x + y
        y, f_new = conv_ffn(rms_norm(x, norm_ffn[layer]), ffn_conv[layer], ffn_w_in[layer], ffn_conv_w[layer],
                            ffn_conv_b[layer], ffn_w_out[layer])
        n_f.append(f_new)
        x = x + y
    return (rms_norm(x, norm_final), jnp.stack(n_h), jnp.stack(n_c), jnp.stack(n_k), jnp.stack(n_v),
            jnp.stack(n_s), jnp.stack(n_f))


def setup_inputs(seed: int = 0) -> dict:
    key = jax.random.key(seed)
    keys = iter(jax.random.split(key, 48))

    def nrm(shape, scale):
        return jax.random.normal(next(keys), shape, F32) * scale

    wb = min(WINDOW, PAST_LEN)
    u = jax.random.uniform(next(keys), (N_A, D_RNN), F32, minval=0.9, maxval=0.999)
    return {
        "x_prompt": nrm((BATCH, SEQ, D_MODEL), 1.0),
        "x_sample": nrm((DEC_BATCH, DEC_SEQ, D_MODEL), 1.0),
        "state_lru_h": nrm((N_A, DEC_BATCH, D_RNN), 1.0),
        "state_lru_conv": nrm((N_A, DEC_BATCH, CONV_A - 1, D_RNN), 1.0),
        "cache_swa_k": nrm((N_B, DEC_BATCH, wb, N_KV, HEAD_DIM), 1.0),
        "cache_swa_v": nrm((N_B, DEC_BATCH, wb, N_KV, HEAD_DIM), 1.0),
        "state_hgrn": nrm((N_C, DEC_BATCH, HG_HEADS, HG_DK, HG_DV), 1.0),
        "state_ffn_conv": nrm((DEPTH, DEC_BATCH, CONV_F - 1, D_FF), 1.0),
        "norm_mix": 1.0 + nrm((DEPTH, D_MODEL), 0.02),
        "norm_ffn": 1.0 + nrm((DEPTH, D_MODEL), 0.02),
        "norm_final": 1.0 + nrm((D_MODEL,), 0.02),
        "lru_w_in": nrm((N_A, D_MODEL, 2 * D_RNN), D_MODEL ** -0.5),
        "lru_conv_w": nrm((N_A, CONV_A, D_RNN), CONV_A ** -0.5),
        "lru_conv_b": nrm((N_A, D_RNN), 0.01),
        "lru_w_a": nrm((N_A, LRU_BLOCKS, LRU_BLOCK, LRU_BLOCK), LRU_BLOCK ** -0.5),
        "lru_b_a": nrm((N_A, D_RNN), 0.01),
        "lru_w_i": nrm((N_A, LRU_BLOCKS, LRU_BLOCK, LRU_BLOCK), LRU_BLOCK ** -0.5),
        "lru_b_i": nrm((N_A, D_RNN), 0.01),
        "lru_lambda": jnp.log(u) - jnp.log1p(-u),
        "lru_w_out": nrm((N_A, D_RNN, D_MODEL), D_RNN ** -0.5),
        "swa_w_qkv": nrm((N_B, D_MODEL, (N_HEADS + 2 * N_KV) * HEAD_DIM), D_MODEL ** -0.5),
        "swa_b_qkv": nrm((N_B, (N_HEADS + 2 * N_KV) * HEAD_DIM), 0.01),
        "swa_sinks": nrm((N_B, N_HEADS), 1.0),
        "swa_w_o": nrm((N_B, N_HEADS * HEAD_DIM, D_MODEL), (N_HEADS * HEAD_DIM) ** -0.5),
        "swa_b_o": nrm((N_B, D_MODEL), 0.01),
        "hg_w_in": nrm((N_C, D_MODEL, 2 * HG_HEADS * HG_DK + 2 * HG_HEADS * HG_DV), D_MODEL ** -0.5),
        "hg_lb_logits": nrm((DEPTH, HG_HEADS * HG_DK), 0.5),
        "hg_norm": 1.0 + nrm((N_C, HG_DV), 0.02),
        "hg_w_o": nrm((N_C, HG_HEADS * HG_DV, D_MODEL), (HG_HEADS * HG_DV) ** -0.5),
        "ffn_w_in": nrm((DEPTH, D_MODEL, 2 * D_FF), D_MODEL ** -0.5),
        "ffn_conv_w": nrm((DEPTH, CONV_F, D_FF), CONV_F ** -0.5),
        "ffn_conv_b": nrm((DEPTH, D_FF), 0.01),
        "ffn_w_out": nrm((DEPTH, D_FF, D_MODEL), D_FF ** -0.5),
    }


def reference(x_prompt, x_sample, state_lru_h, state_lru_conv, cache_swa_k, cache_swa_v, state_hgrn, state_ffn_conv,
              norm_mix, norm_ffn, norm_final,
              lru_w_in, lru_conv_w, lru_conv_b, lru_w_a, lru_b_a, lru_w_i, lru_b_i, lru_lambda, lru_w_out,
              swa_w_qkv, swa_b_qkv, swa_sinks, swa_w_o, swa_b_o,
              hg_w_in, hg_lb_logits, hg_norm, hg_w_o,
              ffn_w_in, ffn_conv_w, ffn_conv_b, ffn_w_out):
    weights = (norm_mix, norm_ffn, norm_final,
               lru_w_in, lru_conv_w, lru_conv_b, lru_w_a, lru_b_a, lru_w_i, lru_b_i, lru_lambda, lru_w_out,
               swa_w_qkv, swa_b_qkv, swa_sinks, swa_w_o, swa_b_o,
               hg_w_in, hg_lb_logits, hg_norm, hg_w_o,
               ffn_w_in, ffn_conv_w, ffn_conv_b, ffn_w_out)
    nb = x_prompt.shape[0]

    def empty(a):
        return jnp.zeros((a.shape[0], nb) + a.shape[2:], x_prompt.dtype)

    y_prompt, p_h, p_c, p_k, p_v, p_s, p_f = run_trunk(
        x_prompt, 0, empty(state_lru_h), empty(state_lru_conv), empty(cache_swa_k), empty(cache_swa_v),
        empty(state_hgrn), empty(state_ffn_conv), weights)
    y_sample, s_h, s_c, s_k, s_v, s_s, s_f = run_trunk(
        x_sample, PAST_LEN, state_lru_h, state_lru_conv, cache_swa_k, cache_swa_v, state_hgrn, state_ffn_conv, weights)
    return (y_prompt, y_sample, p_h, p_c, p_k, p_v, p_s, p_f, s_h, s_c, s_k, s_v, s_s, s_f)
```

```python
import functools

import numpy as np
import jax
import jax.numpy as jnp
from jax import lax
from jax.experimental import pallas as pl
from jax.experimental.pallas import tpu as pltpu

F32 = jnp.float32
BF16 = jnp.bfloat16

D_MODEL = 2048
DEPTH = 4
PAST_LEN = 16384
EPS = 1e-6
LAYER_MIXER = tuple(i % 3 for i in range(DEPTH))
LAYER_SLOT = tuple(LAYER_MIXER[:i].count(LAYER_MIXER[i]) for i in range(DEPTH))

D_RNN = 2560
LRU_BLOCKS = 16
LRU_BLOCK = D_RNN // LRU_BLOCKS
LRU_GROUP = 640
N_LRU_GROUPS = D_RNN // LRU_GROUP
CONV_A = 4
LRU_C = 8.0

N_HEADS = 32
N_KV = 4
HEAD_DIM = 64
GROUP = N_HEADS // N_KV
WINDOW = 128
ROT_DIM = HEAD_DIM // 4
ROPE_THETA = 500000.0
QK_COLS = (N_HEADS + N_KV) * HEAD_DIM
KV_COLS = N_KV * HEAD_DIM

HG_HEADS = 16
HG_DK = 128
HG_DV = 128
HG_CHUNK = 64
HG_LEVELS = 6

D_FF = 3 * D_MODEL
CONV_F = 3

LANES = 128
V7X_VMEM_LIMIT = 56 * 1024 * 1024


def _params(n_axes, vmem=V7X_VMEM_LIMIT):
    return pltpu.CompilerParams(dimension_semantics=("arbitrary",) * n_axes, vmem_limit_bytes=vmem)


def _sigmoid(x):
    return 1.0 / (1.0 + jnp.exp(-x))


def _rms_rows(x, g):
    var = jnp.mean(x * x, axis=-1, keepdims=True)
    return x * lax.rsqrt(var + EPS) * g


def _dot(a, b):
    return jnp.dot(a, b, preferred_element_type=F32)


def _dot_nt(a, b):
    return lax.dot_general(a, b, (((1,), (1,)), ((), ())), preferred_element_type=F32)


def _dot_tn(a, b):
    return lax.dot_general(a, b, (((0,), (0,)), ((), ())), preferred_element_type=F32)


def _largest_tile(n, cap):
    t = cap
    while n % t:
        t -= LANES
    return t


def _norm_matmul_kernel(x_ref, g_ref, w_ref, b_ref, o_ref, hn_ref):
    @pl.when(pl.program_id(1) == 0)
    def _():
        hn_ref[...] = _rms_rows(x_ref[...], g_ref[...]).astype(BF16)

    o_ref[...] = _dot(hn_ref[...], w_ref[...].astype(BF16)) + b_ref[...]


def norm_matmul(x, g, w_stack, li, b, *, tm, tn):
    m, k = x.shape
    n = w_stack.shape[2]
    return pl.pallas_call(
        _norm_matmul_kernel,
        grid=(m // tm, n // tn),
        in_specs=[
            pl.BlockSpec((tm, k), lambda i, j: (i, 0)),
            pl.BlockSpec((1, k), lambda i, j: (0, 0)),
            pl.BlockSpec((None, k, tn), lambda i, j: (li, 0, j)),
            pl.BlockSpec((1, tn), lambda i, j: (0, j)),
        ],
        out_specs=pl.BlockSpec((tm, tn), lambda i, j: (i, j)),
        out_shape=jax.ShapeDtypeStruct((m, n), F32),
        scratch_shapes=[pltpu.VMEM((tm, k), BF16)],
        compiler_params=_params(2),
        name="norm_matmul",
    )(x, g.reshape(1, k), w_stack, b.reshape(1, n))


def _matmul_residual_kernel(a_ref, w_ref, b_ref, x_ref, o_ref):
    o_ref[...] = x_ref[...] + (_dot(a_ref[...], w_ref[...].astype(BF16)) + b_ref[...])


def matmul_residual(a, w_stack, li, b, x, *, tm, tn):
    m, k = a.shape
    n = w_stack.shape[2]
    return pl.pallas_call(
        _matmul_residual_kernel,
        grid=(m // tm, n // tn),
        in_specs=[
            pl.BlockSpec((tm, k), lambda i, j: (i, 0)),
            pl.BlockSpec((None, k, tn), lambda i, j: (li, 0, j)),
            pl.BlockSpec((1, tn), lambda i, j: (0, j)),
            pl.BlockSpec((tm, tn), lambda i, j: (i, j)),
        ],
        out_specs=pl.BlockSpec((tm, tn), lambda i, j: (i, j)),
        out_shape=jax.ShapeDtypeStruct((m, n), F32),
        compiler_params=_params(2),
        name="matmul_residual",
    )(a, w_stack, b.reshape(1, n), x)


def _norm_kernel(x_ref, g_ref, o_ref):
    o_ref[...] = _rms_rows(x_ref[...], g_ref[...])


def final_norm(x, g, *, tm):
    m, k = x.shape
    return pl.pallas_call(
        _norm_kernel,
        grid=(m // tm,),
        in_specs=[pl.BlockSpec((tm, k), lambda i: (i, 0)), pl.BlockSpec((1, k), lambda i: (0, 0))],
        out_specs=pl.BlockSpec((tm, k), lambda i: (i, 0)),
        out_shape=jax.ShapeDtypeStruct((m, k), F32),
        compiler_params=_params(1),
        name="final_norm",
    )(x, g.reshape(1, k))


def _ffn_kernel(*refs, step_mode, tiles_per_seq, tail_rows):
    if step_mode:
        x_ref, g_ref, wg_ref, wu_ref, cw_ref, cb_ref, wo_ref, prev_ref, o_ref, gt_ref, hn_ref = refs
    else:
        x_ref, g_ref, wg_ref, wu_ref, cw_ref, cb_ref, wo_ref, o_ref, gt_ref, hn_ref, ext_ref, tail_ref = refs
    mi = pl.program_id(0)
    fi = pl.program_id(1)
    tm = x_ref.shape[0]

    @pl.when(fi == 0)
    def _():
        x = x_ref[...]
        hn_ref[...] = _rms_rows(x, g_ref[...]).astype(BF16)
        o_ref[...] = x

    hn = hn_ref[...]
    gate = _dot(hn, wg_ref[...].astype(BF16))
    up = _dot(hn, wu_ref[...].astype(BF16))
    cw = cw_ref[...]
    if step_mode:
        prev2 = prev_ref[0]
        prev1 = prev_ref[1]
    else:
        seq_start = (mi % tiles_per_seq) == 0
        ext_ref[0:8, :] = jnp.where(seq_start, 0.0, tail_ref[fi])
        ext_ref[8:, :] = gate
        prev1 = ext_ref[7:7 + tm, :]
        prev2 = ext_ref[6:6 + tm, :]
        tail_ref[fi] = gate[tm - 8:, :]
    conv = cb_ref[...] + gate * cw[2:3] + prev2 * cw[0:1] + prev1 * cw[1:2]
    act = (jax.nn.gelu(conv) * up).astype(BF16)
    o_ref[...] += _dot(act, wo_ref[...].astype(BF16))
    gt_ref[...] = gate[tm - tail_rows:, :]


def conv_ffn(x, g, w_in, conv_w, conv_b, w_out, li, prev, *, tm, tf, seq_len):
    m, k = x.shape
    nf = D_FF // tf
    step_mode = prev is not None
    tail_rows = tm if step_mode else 8
    n_tail = 1 if step_mode else m // seq_len
    tiles_per_seq = 1 if step_mode else seq_len // tm
    in_specs = [
        pl.BlockSpec((tm, k), lambda i, f: (i, 0)),
        pl.BlockSpec((1, k), lambda i, f: (0, 0)),
        pl.BlockSpec((None, k, tf), lambda i, f: (li, 0, f)),
        pl.BlockSpec((None, k, tf), lambda i, f: (li, 0, nf + f)),
        pl.BlockSpec((None, CONV_F, tf), lambda i, f: (li, 0, f)),
        pl.BlockSpec((None, 1, tf), lambda i, f: (li, 0, f)),
        pl.BlockSpec((None, tf, k), lambda i, f: (li, f, 0)),
    ]
    args = [x, g.reshape(1, k), w_in, w_in, conv_w, conv_b.reshape(DEPTH, 1, D_FF), w_out]
    scratch = [pltpu.VMEM((tm, k), BF16)]
    if step_mode:
        in_specs.append(pl.BlockSpec((2, tm, tf), lambda i, f: (0, i, f)))
        args.append(prev)
    else:
        scratch += [pltpu.VMEM((tm + 8, tf), F32), pltpu.VMEM((nf, 8, tf), F32)]
    y, tail = pl.pallas_call(
        functools.partial(_ffn_kernel, step_mode=step_mode, tiles_per_seq=tiles_per_seq, tail_rows=tail_rows),
        grid=(m // tm, nf),
        in_specs=in_specs,
        out_specs=[
            pl.BlockSpec((tm, k), lambda i, f: (i, 0)),
            pl.BlockSpec((None, tail_rows, tf), lambda i, f: (i // tiles_per_seq, 0, f)),
        ],
        out_shape=[jax.ShapeDtypeStruct((m, k), F32), jax.ShapeDtypeStruct((n_tail, tail_rows, D_FF), F32)],
        scratch_shapes=scratch,
        compiler_params=_params(2),
        name="conv_ffn",
    )(*args)
    return y, tail


def _lru_gates(xc, wg_ref, ba, bi, lam):
    lam_abs = jnp.abs(lam)
    softplus_neg = jnp.maximum(-lam, 0.0) + jnp.log(1.0 + jnp.exp(-lam_abs))
    a_parts, u_parts = [], []
    for gi in range(N_LRU_GROUPS):
        cols = slice(gi * LRU_GROUP, (gi + 1) * LRU_GROUP)
        xg = xc[:, cols]
        proj = _dot(xg.astype(BF16), wg_ref[gi])
        r = _sigmoid(proj[:, :LRU_GROUP] + ba[:, cols])
        i = _sigmoid(proj[:, LRU_GROUP:] + bi[:, cols])
        log_a = -LRU_C * r * softplus_neg[:, cols]
        a = jnp.exp(log_a)
        a_parts.append(a)
        u_parts.append(jnp.sqrt(1.0 - a * a) * (i * xg))
    return a_parts, u_parts


def _lru_seq_kernel(gate_ref, xr_ref, cw_ref, cb_ref, wg_ref, ba_ref, bi_ref, lam_ref,
                    y_ref, hlast_ref, ext_ref, a_ref, u_ref, h_ref, carry_ref):
    tt = xr_ref.shape[0]

    @pl.when(pl.program_id(1) == 0)
    def _():
        ext_ref[0:8, :] = jnp.zeros((8, D_RNN), F32)
        carry_ref[...] = jnp.zeros((1, D_RNN), F32)

    xr = xr_ref[...]
    ext_ref[8:, :] = xr
    cw = cw_ref[...]
    xc = cb_ref[...] + xr * cw[3:4]
    for j in range(CONV_A - 1):
        xc = xc + ext_ref[5 + j:5 + j + tt, :] * cw[j:j + 1]
    ext_ref[0:8, :] = xr[tt - 8:, :]

    a_parts, u_parts = _lru_gates(xc, wg_ref, ba_ref[...], bi_ref[...], lam_ref[...])
    for gi in range(N_LRU_GROUPS):
        cols = slice(gi * LRU_GROUP, (gi + 1) * LRU_GROUP)
        a_ref[:, cols] = a_parts[gi]
        u_ref[:, cols] = u_parts[gi]

    def rows8(r8, h):
        rows = pl.ds(pl.multiple_of(r8 * 8, 8), 8)
        a8 = a_ref[rows, :]
        u8 = u_ref[rows, :]
        out = []
        for k in range(8):
            h = a8[k:k + 1, :] * h + u8[k:k + 1, :]
            out.append(h)
        h_ref[rows, :] = jnp.concatenate(out, axis=0)
        return h

    h = lax.fori_loop(0, tt // 8, rows8, carry_ref[...])
    carry_ref[...] = h
    hlast_ref[...] = h
    y_ref[...] = (jax.nn.gelu(gate_ref[...]) * h_ref[...]).astype(BF16)


def lru_seq(gx, conv_w, conv_b, wg, b_a, b_i, lam, *, batch, seq_len, tt):
    m = gx.shape[0]
    nt = seq_len // tt
    vec = lambda: pl.BlockSpec((1, D_RNN), lambda b, t: (0, 0))
    y, h_last = pl.pallas_call(
        _lru_seq_kernel,
        grid=(batch, nt),
        in_specs=[
            pl.BlockSpec((tt, D_RNN), lambda b, t: (b * nt + t, 0)),
            pl.BlockSpec((tt, D_RNN), lambda b, t: (b * nt + t, 1)),
            pl.BlockSpec((CONV_A, D_RNN), lambda b, t: (0, 0)),
            vec(),
            pl.BlockSpec((N_LRU_GROUPS, LRU_GROUP, 2 * LRU_GROUP), lambda b, t: (0, 0, 0)),
            vec(), vec(), vec(),
        ],
        out_specs=[
            pl.BlockSpec((tt, D_RNN), lambda b, t: (b * nt + t, 0)),
            pl.BlockSpec((None, 1, D_RNN), lambda b, t: (b, 0, 0)),
        ],
        out_shape=[jax.ShapeDtypeStruct((m, D_RNN), BF16), jax.ShapeDtypeStruct((batch, 1, D_RNN), F32)],
        scratch_shapes=[pltpu.VMEM((tt + 8, D_RNN), F32), pltpu.VMEM((tt, D_RNN), F32),
                        pltpu.VMEM((tt, D_RNN), F32), pltpu.VMEM((tt, D_RNN), F32),
                        pltpu.VMEM((1, D_RNN), F32)],
        compiler_params=_params(2),
        name="lru_seq",
    )(gx, gx, conv_w, conv_b.reshape(1, D_RNN), wg, b_a.reshape(1, D_RNN), b_i.reshape(1, D_RNN),
      lam.reshape(1, D_RNN))
    return y, h_last.reshape(batch, D_RNN)


def _lru_step_kernel(gx_ref, h0_ref, cbuf_ref, cw_ref, cb_ref, wg_ref, ba_ref, bi_ref, lam_ref, y_ref, h_ref):
    gate = gx_ref[:, :D_RNN]
    xr = gx_ref[:, D_RNN:]
    cw = cw_ref[...]
    xc = cb_ref[...] + xr * cw[3:4]
    for j in range(CONV_A - 1):
        xc = xc + cbuf_ref[j] * cw[j:j + 1]
    a_parts, u_parts = _lru_gates(xc, wg_ref, ba_ref[...], bi_ref[...], lam_ref[...])
    a = jnp.concatenate(a_parts, axis=1)
    u = jnp.concatenate(u_parts, axis=1)
    h = u + a * h0_ref[...]
    h_ref[...] = h
    y_ref[...] = (jax.nn.gelu(gate) * h).astype(BF16)


def lru_step(gx, h0, cbuf, conv_w, conv_b, wg, b_a, b_i, lam):
    m = gx.shape[0]
    return pl.pallas_call(
        _lru_step_kernel,
        out_shape=[jax.ShapeDtypeStruct((m, D_RNN), BF16), jax.ShapeDtypeStruct((m, D_RNN), F32)],
        compiler_params=pltpu.CompilerParams(vmem_limit_bytes=V7X_VMEM_LIMIT),
        name="lru_step",
    )(gx, h0, cbuf, conv_w, conv_b.reshape(1, D_RNN), wg, b_a.reshape(1, D_RNN), b_i.reshape(1, D_RNN),
      lam.reshape(1, D_RNN))


def _lru_gate_weights(w_a, w_i):
    per = LRU_GROUP // LRU_BLOCK

    def blockdiag(w):
        w = w.reshape(N_LRU_GROUPS, per, LRU_BLOCK, LRU_BLOCK)
        eye = jnp.eye(per, dtype=w.dtype)
        return jnp.einsum('gpcd,pq->gpcqd', w, eye).reshape(N_LRU_GROUPS, LRU_GROUP, LRU_GROUP)

    return jnp.concatenate([blockdiag(w_a), blockdiag(w_i)], axis=2).astype(BF16)


def _rope_tables(pos):
    half = ROT_DIM // 2
    inv = ROPE_THETA ** (-jnp.arange(half, dtype=F32) * (2.0 / ROT_DIM))
    ang = pos[:, None] * inv[None, :]
    cos, sin = jnp.cos(ang), jnp.sin(ang)
    rows = pos.shape[0]
    ones = jnp.ones((rows, HEAD_DIM - ROT_DIM), F32)
    zeros_h = jnp.zeros((rows, half), F32)
    zeros_t = jnp.zeros((rows, HEAD_DIM - ROT_DIM), F32)
    c = jnp.concatenate([cos, cos, ones], axis=1)
    s_lo = jnp.concatenate([-sin, zeros_h, zeros_t], axis=1)
    s_hi = jnp.concatenate([zeros_h, sin, zeros_t], axis=1)
    rep = LANES // HEAD_DIM
    return tuple(jnp.tile(t, (1, rep)) for t in (c, s_lo, s_hi))


def _rope_kernel(qk_ref, c_ref, slo_ref, shi_ref, q_ref, k_ref):
    c, s_lo, s_hi = c_ref[...], slo_ref[...], shi_ref[...]
    half = ROT_DIM // 2
    n_q = N_HEADS * HEAD_DIM // LANES
    for ci in range(QK_COLS // LANES):
        x = qk_ref[:, ci * LANES:(ci + 1) * LANES]
        rot = x * c + pltpu.roll(x, LANES - half, 1) * s_lo + pltpu.roll(x, half, 1) * s_hi
        if ci < n_q:
            q_ref[:, ci * LANES:(ci + 1) * LANES] = rot.astype(BF16)
        else:
            k_ref[:, (ci - n_q) * LANES:(ci - n_q + 1) * LANES] = rot


def rope(qkv, tables, *, tm):
    m = qkv.shape[0]
    nt = tables[0].shape[0] // tm
    tab = lambda: pl.BlockSpec((tm, LANES), lambda i: (i % nt, 0))
    return pl.pallas_call(
        _rope_kernel,
        grid=(m // tm,),
        in_specs=[pl.BlockSpec((tm, QK_COLS), lambda i: (i, 0)), tab(), tab(), tab()],
        out_specs=[pl.BlockSpec((tm, N_HEADS * HEAD_DIM), lambda i: (i, 0)),
                   pl.BlockSpec((tm, KV_COLS), lambda i: (i, 0))],
        out_shape=[jax.ShapeDtypeStruct((m, N_HEADS * HEAD_DIM), BF16), jax.ShapeDtypeStruct((m, KV_COLS), F32)],
        compiler_params=_params(1),
        name="rope",
    )(qkv, *tables)


def _swa_seq_kernel(sink_ref, q_ref, kp_ref, kc_ref, vp_ref, vc_ref, o_ref):
    qi = pl.program_id(1)
    tq = q_ref.shape[0]
    kk = jnp.concatenate([kp_ref[...], kc_ref[...]], axis=0).astype(BF16)
    vv = jnp.concatenate([vp_ref[...], vc_ref[...]], axis=0).astype(BF16)
    row = lax.broadcasted_iota(jnp.int32, (tq, 2 * tq), 0)
    col = lax.broadcasted_iota(jnp.int32, (tq, 2 * tq), 1)
    rel = tq + row - col
    valid = (rel >= 0) & (rel < WINDOW) & ((qi > 0) | (col >= tq))
    for h in range(N_HEADS):
        kh = h // GROUP
        kcols = slice(kh * HEAD_DIM, (kh + 1) * HEAD_DIM)
        hcols = slice(h * HEAD_DIM, (h + 1) * HEAD_DIM)
        s = _dot_nt(q_ref[:, hcols], kk[:, kcols]) * (HEAD_DIM ** -0.5)
        s = jnp.where(valid, s, -jnp.inf)
        sink = sink_ref[h]
        mx = jnp.maximum(jnp.max(s, axis=-1, keepdims=True), sink)
        p = jnp.exp(s - mx)
        denom = jnp.sum(p, axis=-1, keepdims=True) + jnp.exp(sink - mx)
        o = _dot(p.astype(BF16), vv[:, kcols]) / denom
        o_ref[:, hcols] = o.astype(BF16)


def swa_seq(q_rot, k_rot, qkv, sinks, *, batch, seq_len):
    m = q_rot.shape[0]
    tq = WINDOW
    nq = seq_len // tq
    v_col = QK_COLS // KV_COLS
    prev = lambda col: pl.BlockSpec((tq, KV_COLS), lambda b, i: (b * nq + jnp.maximum(i - 1, 0), col))
    cur = lambda col: pl.BlockSpec((tq, KV_COLS), lambda b, i: (b * nq + i, col))
    return pl.pallas_call(
        _swa_seq_kernel,
        grid=(batch, nq),
        in_specs=[
            pl.BlockSpec(memory_space=pltpu.SMEM),
            pl.BlockSpec((tq, N_HEADS * HEAD_DIM), lambda b, i: (b * nq + i, 0)),
            prev(0), cur(0), prev(v_col), cur(v_col),
        ],
        out_specs=pl.BlockSpec((tq, N_HEADS * HEAD_DIM), lambda b, i: (b * nq + i, 0)),
        out_shape=jax.ShapeDtypeStruct((m, N_HEADS * HEAD_DIM), BF16),
        compiler_params=_params(2),
        name="swa_seq",
    )(sinks, q_rot, k_rot, k_rot, qkv, qkv)


def _swa_step_kernel(q_ref, kn_ref, vn_ref, ck_ref, cv_ref, sink_ref, o_ref):
    ck = ck_ref[...].astype(BF16)
    cv = cv_ref[...].astype(BF16)
    col = lax.broadcasted_iota(jnp.int32, (GROUP, WINDOW), 1)
    for kh in range(N_KV):
        kcols = slice(kh * HEAD_DIM, (kh + 1) * HEAD_DIM)
        heads = slice(kh * GROUP, (kh + 1) * GROUP)
        q = q_ref[heads, :]
        k_new = kn_ref[kh:kh + 1, :].astype(BF16).astype(F32)
        v_new = vn_ref[kh:kh + 1, :].astype(BF16).astype(F32)
        scale = HEAD_DIM ** -0.5
        s_c = _dot_nt(q, ck[:, kcols]) * scale
        s_c = jnp.where(col >= 1, s_c, -jnp.inf)
        s_n = jnp.sum(q.astype(F32) * k_new, axis=-1, keepdims=True) * scale
        sink = sink_ref[heads, :]
        mx = jnp.maximum(jnp.maximum(jnp.max(s_c, axis=-1, keepdims=True), s_n), sink)
        p_c = jnp.exp(s_c - mx)
        p_n = jnp.exp(s_n - mx)
        denom = jnp.sum(p_c, axis=-1, keepdims=True) + p_n + jnp.exp(sink - mx)
        o = _dot(p_c.astype(BF16), cv[:, kcols]) + p_n.astype(BF16).astype(F32) * v_new
        o_ref[heads, :] = (o / denom).astype(BF16)


def swa_step(q3, k_new, v_new, cache_k, cache_v, sinks):
    b = q3.shape[0]
    return pl.pallas_call(
        _swa_step_kernel,
        grid=(b,),
        in_specs=[
            pl.BlockSpec((None, N_HEADS, HEAD_DIM), lambda i: (i, 0, 0)),
            pl.BlockSpec((None, N_KV, HEAD_DIM), lambda i: (i, 0, 0)),
            pl.BlockSpec((None, N_KV, HEAD_DIM), lambda i: (i, 0, 0)),
            pl.BlockSpec((None, WINDOW, KV_COLS), lambda i: (i, 0, 0)),
            pl.BlockSpec((None, WINDOW, KV_COLS), lambda i: (i, 0, 0)),
            pl.BlockSpec((N_HEADS, 1), lambda i: (0, 0)),
        ],
        out_specs=pl.BlockSpec((None, N_HEADS, HEAD_DIM), lambda i: (i, 0, 0)),
        out_shape=jax.ShapeDtypeStruct((b, N_HEADS, HEAD_DIM), BF16),
        compiler_params=_params(1),
        name="swa_step",
    )(q3, k_new, v_new, cache_k, cache_v, sinks.reshape(N_HEADS, 1))


def _hgrn_lower_bound(logits, layer):
    mx = jnp.max(logits, axis=0, keepdims=True)
    e = jnp.exp(logits - mx)
    sm = e / jnp.sum(e, axis=0, keepdims=True)
    lb = jnp.zeros_like(sm[0:1])
    for i in range(1, layer + 1):
        lb = lb + sm[i:i + 1]
    return lb


def _hgrn_consts():
    c = HG_CHUNK
    t = np.arange(c)[:, None]
    s = np.arange(c)[None, :]
    sel = [(s <= t)]
    msk = [(s == t)]
    for lvl in range(1, HG_LEVELS + 1):
        w = 1 << (lvl - 1)
        ref_row = (t // (2 * w)) * (2 * w) + w - 1
        sel.append(s <= ref_row)
        msk.append((t // (2 * w)) == (s // (2 * w)))
    sel = np.concatenate(sel, axis=0).astype(np.float32)
    msk = np.stack(msk, axis=0).astype(np.float32)
    return jnp.asarray(sel, BF16), jnp.asarray(msk, F32)


def _split3(x):
    hi = x.astype(BF16)
    r1 = x - hi.astype(F32)
    mid = r1.astype(BF16)
    lo = (r1 - mid.astype(F32)).astype(BF16)
    return hi, mid, lo


def _hgrn_seq_kernel(q_ref, z_ref, v_ref, g_ref, lbl_ref, gn_ref, sel_ref, msk_ref, y_ref, s_out_ref, st_ref,
                     *, layer, heads_per_step):
    ti = pl.program_id(2)
    tt = q_ref.shape[0]
    c = HG_CHUNK

    @pl.when(ti == 0)
    def _():
        st_ref[...] = jnp.zeros(st_ref.shape, F32)

    lb_all = _hgrn_lower_bound(lbl_ref[...], layer)
    gn = gn_ref[...]
    row = lax.broadcasted_iota(jnp.int32, (c, HG_DK), 0)

    for hh in range(heads_per_step):
        cols = slice(hh * HG_DK, (hh + 1) * HG_DK)
        lb = lb_all[:, cols]

        def chunk(ci, carry, cols=cols, lb=lb, hh=hh):
            rows = pl.ds(pl.multiple_of(ci * c, c), c)
            q = q_ref[rows, cols]
            z = z_ref[rows, cols]
            v = v_ref[rows, cols].astype(BF16)
            sq = q * _sigmoid(q)
            sz = _sigmoid(z)
            log_f = jnp.log(lb + (1.0 - lb) * sz)
            k = (1.0 - lb) * (1.0 - sz)
            sums = _dot(sel_ref[...], jnp.concatenate(_split3(log_f), axis=1))
            sums = sums[:, :HG_DK] + sums[:, HG_DK:2 * HG_DK] + sums[:, 2 * HG_DK:]
            b = sums[0:c]
            att = _dot_nt(sq.astype(BF16), k.astype(BF16)) * msk_ref[0]
            for lvl in range(1, HG_LEVELS + 1):
                ref = sums[lvl * c:(lvl + 1) * c]
                e = jnp.exp(-jnp.abs(b - ref))
                upper = ((row >> (lvl - 1)) & 1) == 1
                qm = jnp.where(upper, sq * e, 0.0).astype(BF16)
                km = jnp.where(upper, 0.0, k * e).astype(BF16)
                att = att + _dot_nt(qm, km) * msk_ref[lvl]
            st = st_ref[hh]
            o = _dot(att.astype(BF16), v) + _dot_nt((sq * jnp.exp(b)).astype(BF16), st.astype(BF16))
            b_last = b[c - 1:c, :]
            k_dec = (k * jnp.exp(b_last - b)).astype(BF16)
            st_ref[hh] = st * jnp.exp(b_last) + _dot_tn(v, k_dec)
            y = _rms_rows(o, gn)
            g = g_ref[rows, cols]
            y_ref[rows, cols] = (y * (g * _sigmoid(g))).astype(BF16)
            return carry

        lax.fori_loop(0, tt // c, chunk, 0)

    @pl.when(ti == pl.num_programs(2) - 1)
    def _():
        for hh in range(heads_per_step):
            s_out_ref[hh] = st_ref[hh].T


def hgrn_seq(qzvg, lb_logits, g_norm, *, layer, batch, seq_len, tt, heads_per_step):
    m = qzvg.shape[0]
    nt = seq_len // tt
    hw = heads_per_step * HG_DK
    nh = HG_HEADS // heads_per_step
    sel, msk = _hgrn_consts()
    part = lambda p: pl.BlockSpec((tt, hw), lambda b, h, t: (b * nt + t, p * nh + h))
    y, s_out = pl.pallas_call(
        functools.partial(_hgrn_seq_kernel, layer=layer, heads_per_step=heads_per_step),
        grid=(batch, nh, nt),
        in_specs=[
            part(0), part(1), part(2), part(3),
            pl.BlockSpec((DEPTH, hw), lambda b, h, t: (0, h)),
            pl.BlockSpec((1, HG_DV), lambda b, h, t: (0, 0)),
            pl.BlockSpec(sel.shape, lambda b, h, t: (0, 0)),
            pl.BlockSpec(msk.shape, lambda b, h, t: (0, 0, 0)),
        ],
        out_specs=[
            pl.BlockSpec((tt, hw), lambda b, h, t: (b * nt + t, h)),
            pl.BlockSpec((None, heads_per_step, HG_DK, HG_DV), lambda b, h, t: (b, h, 0, 0)),
        ],
        out_shape=[jax.ShapeDtypeStruct((m, HG_HEADS * HG_DV), BF16),
                   jax.ShapeDtypeStruct((batch, HG_HEADS, HG_DK, HG_DV), F32)],
        scratch_shapes=[pltpu.VMEM((heads_per_step, HG_DV, HG_DK), F32)],
        compiler_params=_params(3),
        name="hgrn_seq",
    )(qzvg, qzvg, qzvg, qzvg, lb_logits, g_norm.reshape(1, HG_DV), sel, msk)
    return y, s_out


def _hgrn_step_kernel(x_ref, s_ref, lbl_ref, gn_ref, y_ref, s_out_ref, *, layer):
    nh = HG_HEADS
    q = x_ref[0:nh, :]
    z = x_ref[nh:2 * nh, :]
    v = x_ref[2 * nh:3 * nh, :]
    g = x_ref[3 * nh:4 * nh, :]
    lb = _hgrn_lower_bound(lbl_ref[...], layer)[0]
    sq = q * _sigmoid(q)
    sz = _sigmoid(z)
    f = lb + (1.0 - lb) * sz
    k = (1.0 - lb) * (1.0 - sz)
    pad = jnp.zeros((LANES - 3 * nh, HG_DK), F32)
    cols = jnp.concatenate([f, k, sq, pad], axis=0).T
    outs = []
    for h in range(nh):
        f_col = cols[:, h:h + 1]
        k_col = cols[:, nh + h:nh + h + 1]
        q_col = cols[:, 2 * nh + h:2 * nh + h + 1]
        s_new = s_ref[h] * f_col + k_col * v[h:h + 1, :]
        s_out_ref[h] = s_new
        outs.append(jnp.sum(s_new * q_col, axis=0, keepdims=True))
    o = jnp.concatenate(outs, axis=0)
    y_ref[...] = (_rms_rows(o, gn_ref[...]) * (g * _sigmoid(g))).astype(BF16)


def hgrn_step(x4, state, lb_logits, g_norm, *, layer):
    b = x4.shape[0]
    return pl.pallas_call(
        functools.partial(_hgrn_step_kernel, layer=layer),
        grid=(b,),
        in_specs=[
            pl.BlockSpec((None, 4 * HG_HEADS, HG_DK), lambda i: (i, 0, 0)),
            pl.BlockSpec((None, HG_HEADS, HG_DK, HG_DV), lambda i: (i, 0, 0, 0)),
            pl.BlockSpec((DEPTH, HG_HEADS, HG_DK), lambda i: (0, 0, 0)),
            pl.BlockSpec((1, HG_DV), lambda i: (0, 0)),
        ],
        out_specs=[
            pl.BlockSpec((None, HG_HEADS, HG_DV), lambda i: (i, 0, 0)),
            pl.BlockSpec((None, HG_HEADS, HG_DK, HG_DV), lambda i: (i, 0, 0, 0)),
        ],
        out_shape=[jax.ShapeDtypeStruct((b, HG_HEADS, HG_DV), BF16),
                   jax.ShapeDtypeStruct((b, HG_HEADS, HG_DK, HG_DV), F32)],
        compiler_params=_params(1),
        name="hgrn_step",
    )(x4, state, lb_logits.reshape(DEPTH, HG_HEADS, HG_DK), g_norm.reshape(1, HG_DV))


def _trunk(x3, states, w, *, step_mode):
    batch, seq_len, _ = x3.shape
    m = batch * seq_len
    x = x3.reshape(m, D_MODEL)
    tm = m if step_mode else 1024
    tn_cap = 1280 if step_mode else 512
    tf = 512 if step_mode else 256
    zeros = lambda n: jnp.zeros((n,), F32)
    lru_wg = [_lru_gate_weights(w["lru_w_a"][j], w["lru_w_i"][j]) for j in range(w["lru_w_a"].shape[0])]
    if step_mode:
        pos = jnp.full((m,), PAST_LEN, F32)
    else:
        pos = jnp.arange(seq_len, dtype=F32)
    rope_tables = _rope_tables(pos)

    new = {"lru_h": [], "lru_conv": [], "swa_k": [], "swa_v": [], "hgrn": [], "ffn_conv": []}
    for layer in range(DEPTH):
        kind, j = LAYER_MIXER[layer], LAYER_SLOT[layer]
        g_mix = w["norm_mix"][layer]
        if kind == 0:
            n = 2 * D_RNN
            gx = norm_matmul(x, g_mix, w["lru_w_in"], j, zeros(n), tm=tm, tn=_largest_tile(n, tn_cap))
            if step_mode:
                cbuf = states["lru_conv"][j]
                y, h_new = lru_step(gx, states["lru_h"][j], jnp.swapaxes(cbuf, 0, 1), w["lru_conv_w"][j],
                                    w["lru_conv_b"][j], lru_wg[j], w["lru_b_a"][j], w["lru_b_i"][j],
                                    w["lru_lambda"][j])
                c_new = jnp.concatenate([cbuf[:, 1:], gx[:, None, D_RNN:]], axis=1)
            else:
                y, h_new = lru_seq(gx, w["lru_conv_w"][j], w["lru_conv_b"][j], lru_wg[j], w["lru_b_a"][j],
                                   w["lru_b_i"][j], w["lru_lambda"][j], batch=batch, seq_len=seq_len, tt=256)
                c_new = gx.reshape(batch, seq_len, n)[:, seq_len - (CONV_A - 1):, D_RNN:]
            new["lru_h"].append(h_new)
            new["lru_conv"].append(c_new)
            x = matmul_residual(y, w["lru_w_out"], j, zeros(D_MODEL), x, tm=tm, tn=_largest_tile(D_MODEL, tn_cap))
        elif kind == 1:
            n = QK_COLS + KV_COLS
            qkv = norm_matmul(x, g_mix, w["swa_w_qkv"], j, w["swa_b_qkv"][j], tm=tm, tn=_largest_tile(n, tn_cap))
            q_rot, k_rot = rope(qkv, rope_tables, tm=tm)
            v = qkv[:, QK_COLS:]
            if step_mode:
                ck = states["swa_k"][j]
                cv = states["swa_v"][j]
                k_new = k_rot.reshape(m, N_KV, HEAD_DIM)
                v_new = v.reshape(m, N_KV, HEAD_DIM)
                o = swa_step(q_rot.reshape(m, N_HEADS, HEAD_DIM), k_new, v_new,
                             ck.reshape(m, WINDOW, KV_COLS), cv.reshape(m, WINDOW, KV_COLS), w["swa_sinks"][j])
                o = o.reshape(m, N_HEADS * HEAD_DIM)
                k_cache = jnp.concatenate([ck[:, 1:], k_new[:, None]], axis=1)
                v_cache = jnp.concatenate([cv[:, 1:], v_new[:, None]], axis=1)
            else:
                o = swa_seq(q_rot, k_rot, qkv, w["swa_sinks"][j], batch=batch, seq_len=seq_len)
                k_cache = k_rot.reshape(batch, seq_len, N_KV, HEAD_DIM)[:, seq_len - WINDOW:]
                v_cache = v.reshape(batch, seq_len, N_KV, HEAD_DIM)[:, seq_len - WINDOW:]
            new["swa_k"].append(k_cache)
            new["swa_v"].append(v_cache)
            x = matmul_residual(o, w["swa_w_o"], j, w["swa_b_o"][j], x, tm=tm, tn=_largest_tile(D_MODEL, tn_cap))
        else:
            n = 2 * HG_HEADS * HG_DK + 2 * HG_HEADS * HG_DV
            qzvg = norm_matmul(x, g_mix, w["hg_w_in"], j, zeros(n), tm=tm, tn=_largest_tile(n, tn_cap))
            if step_mode:
                y, s_new = hgrn_step(qzvg.reshape(m, 4 * HG_HEADS, HG_DK), states["hgrn"][j], w["hg_lb_logits"],
                                     w["hg_norm"][j], layer=layer)
                y = y.reshape(m, HG_HEADS * HG_DV)
            else:
                y, s_new = hgrn_seq(qzvg, w["hg_lb_logits"], w["hg_norm"][j], layer=layer, batch=batch,
                                    seq_len=seq_len, tt=512, heads_per_step=2)
            new["hgrn"].append(s_new)
            x = matmul_residual(y, w["hg_w_o"], j, zeros(D_MODEL), x, tm=tm, tn=_largest_tile(D_MODEL, tn_cap))

        if step_mode:
            fbuf = states["ffn_conv"][layer]
            x, gate = conv_ffn(x, w["norm_ffn"][layer], w["ffn_w_in"], w["ffn_conv_w"], w["ffn_conv_b"],
                               w["ffn_w_out"], layer, jnp.swapaxes(fbuf, 0, 1), tm=tm, tf=tf, seq_len=seq_len)
            f_new = jnp.concatenate([fbuf[:, 1:], gate.reshape(m, 1, D_FF)], axis=1)
        else:
            x, tail = conv_ffn(x, w["norm_ffn"][layer], w["ffn_w_in"], w["ffn_conv_w"], w["ffn_conv_b"],
                               w["ffn_w_out"], layer, None, tm=tm, tf=tf, seq_len=seq_len)
            f_new = tail[:, 8 - (CONV_F - 1):]
        new["ffn_conv"].append(f_new)

    y = final_norm(x, w["norm_final"], tm=tm).reshape(batch, seq_len, D_MODEL)
    return (y, jnp.stack(new["lru_h"]), jnp.stack(new["lru_conv"]), jnp.stack(new["swa_k"]),
            jnp.stack(new["swa_v"]), jnp.stack(new["hgrn"]), jnp.stack(new["ffn_conv"]))


def kernel(x_prompt, x_sample, state_lru_h, state_lru_conv, cache_swa_k, cache_swa_v, state_hgrn, state_ffn_conv,
           norm_mix, norm_ffn, norm_final,
           lru_w_in, lru_conv_w, lru_conv_b, lru_w_a, lru_b_a, lru_w_i, lru_b_i, lru_lambda, lru_w_out,
           swa_w_qkv, swa_b_qkv, swa_sinks, swa_w_o, swa_b_o,
           hg_w_in, hg_lb_logits, hg_norm, hg_w_o,
           ffn_w_in, ffn_conv_w, ffn_conv_b, ffn_w_out):
    w = dict(norm_mix=norm_mix, norm_ffn=norm_ffn, norm_final=norm_final,
             lru_w_in=lru_w_in, lru_conv_w=lru_conv_w, lru_conv_b=lru_conv_b, lru_w_a=lru_w_a, lru_b_a=lru_b_a,
             lru_w_i=lru_w_i, lru_b_i=lru_b_i, lru_lambda=lru_lambda, lru_w_out=lru_w_out,
             swa_w_qkv=swa_w_qkv, swa_b_qkv=swa_b_qkv, swa_sinks=swa_sinks, swa_w_o=swa_w_o, swa_b_o=swa_b_o,
             hg_w_in=hg_w_in, hg_lb_logits=hg_lb_logits, hg_norm=hg_norm, hg_w_o=hg_w_o,
             ffn_w_in=ffn_w_in, ffn_conv_w=ffn_conv_w, ffn_conv_b=ffn_conv_b, ffn_w_out=ffn_w_out)
    states = dict(lru_h=state_lru_h, lru_conv=state_lru_conv, swa_k=cache_swa_k, swa_v=cache_swa_v,
                  hgrn=state_hgrn, ffn_conv=state_ffn_conv)
    prompt = _trunk(x_prompt, None, w, step_mode=False)
    sample = _trunk(x_sample, states, w, step_mode=True)
    return (prompt[0], sample[0]) + prompt[1:] + sample[1:]
```

```python
import functools

import numpy as np
import jax
import jax.numpy as jnp
from jax import lax
from jax.experimental import pallas as pl
from jax.experimental.pallas import tpu as pltpu

F32 = jnp.float32
BF16 = jnp.bfloat16

D_MODEL = 2048
DEPTH = 4
PAST_LEN = 16384
EPS = 1e-6
LAYER_MIXER = tuple(i % 3 for i in range(DEPTH))
LAYER_SLOT = tuple(LAYER_MIXER[:i].count(LAYER_MIXER[i]) for i in range(DEPTH))

D_RNN = 2560
LRU_BLOCKS = 16
LRU_BLOCK = D_RNN // LRU_BLOCKS
LRU_GROUP = 640
N_LRU_GROUPS = D_RNN // LRU_GROUP
CONV_A = 4
LRU_C = 8.0

N_HEADS = 32
N_KV = 4
HEAD_DIM = 64
GROUP = N_HEADS // N_KV
WINDOW = 128
ROT_DIM = HEAD_DIM // 4
ROPE_THETA = 500000.0
QK_COLS = (N_HEADS + N_KV) * HEAD_DIM
KV_COLS = N_KV * HEAD_DIM

HG_HEADS = 16
HG_DK = 128
HG_DV = 128
HG_CHUNK = 64
HG_LEVELS = 6

D_FF = 3 * D_MODEL
CONV_F = 3

LANES = 128
SUBLANES = 8
V7X_VMEM_LIMIT = 56 * 1024 * 1024
SINGLE_BUFFER_BYTES = 14 * 1024 * 1024
W_TILE_BYTES = 8 * 1024 * 1024


def _params(n_axes, vmem=V7X_VMEM_LIMIT):
    return pltpu.CompilerParams(dimension_semantics=("arbitrary",) * n_axes, vmem_limit_bytes=vmem)


def _sigmoid(x):
    return 1.0 / (1.0 + jnp.exp(-x))


def _rms_rows(x, g):
    var = jnp.mean(x * x, axis=-1, keepdims=True)
    return x * lax.rsqrt(var + EPS) * g


def _dot(a, b):
    return jnp.dot(a, b, preferred_element_type=F32)


def _dot_nt(a, b):
    return lax.dot_general(a, b, (((1,), (1,)), ((), ())), preferred_element_type=F32)


def _dot_tn(a, b):
    return lax.dot_general(a, b, (((0,), (0,)), ((), ())), preferred_element_type=F32)


def _largest_tile(n, cap):
    t = cap
    while n % t:
        t -= LANES
    return t


def _row_tile_spec(shape, index_map, dtype):
    if shape[0] * shape[1] * jnp.dtype(dtype).itemsize >= SINGLE_BUFFER_BYTES:
        return pl.BlockSpec(shape, index_map, pipeline_mode=pl.Buffered(1))
    return pl.BlockSpec(shape, index_map)


def _norm_matmul_kernel(x_ref, g_ref, w_ref, b_ref, o_ref, hn_ref):
    @pl.when(pl.program_id(1) == 0)
    def _():
        hn_ref[...] = _rms_rows(x_ref[...], g_ref[...]).astype(BF16)

    o_ref[...] = _dot(hn_ref[...], w_ref[...].astype(BF16)) + b_ref[...]


def norm_matmul(x, g, w_stack, li, b, *, tm, tn):
    m, k = x.shape
    n = w_stack.shape[2]
    return pl.pallas_call(
        _norm_matmul_kernel,
        grid=(m // tm, n // tn),
        in_specs=[
            _row_tile_spec((tm, k), lambda i, j: (i, 0), F32),
            pl.BlockSpec((1, k), lambda i, j: (0, 0)),
            pl.BlockSpec((None, k, tn), lambda i, j: (li, 0, j)),
            pl.BlockSpec((1, tn), lambda i, j: (0, j)),
        ],
        out_specs=pl.BlockSpec((tm, tn), lambda i, j: (i, j)),
        out_shape=jax.ShapeDtypeStruct((m, n), F32),
        scratch_shapes=[pltpu.VMEM((tm, k), BF16)],
        compiler_params=_params(2),
        name="norm_matmul",
    )(x, g.reshape(1, k), w_stack, b.reshape(1, n))


def _matmul_residual_kernel(a_ref, w_ref, b_ref, x_ref, o_ref):
    o_ref[...] = x_ref[...] + (_dot(a_ref[...], w_ref[...].astype(BF16)) + b_ref[...])


def matmul_residual(a, w_stack, li, b, x, *, tm, tn, k_blocks=1, k_index=0):
    m = a.shape[0]
    k = a.shape[1] // k_blocks
    n = w_stack.shape[2]
    return pl.pallas_call(
        _matmul_residual_kernel,
        grid=(m // tm, n // tn),
        in_specs=[
            _row_tile_spec((tm, k), lambda i, j: (i, k_index), a.dtype),
            pl.BlockSpec((None, k, tn), lambda i, j: (li, k_index, j)),
            pl.BlockSpec((1, tn), lambda i, j: (0, j)),
            pl.BlockSpec((tm, tn), lambda i, j: (i, j)),
        ],
        out_specs=pl.BlockSpec((tm, tn), lambda i, j: (i, j)),
        out_shape=jax.ShapeDtypeStruct((m, n), F32),
        compiler_params=_params(2),
        name="matmul_residual",
    )(a, w_stack, b.reshape(1, n), x)


def _norm_kernel(x_ref, g_ref, o_ref):
    o_ref[...] = _rms_rows(x_ref[...], g_ref[...])


def final_norm(x, g, *, tm):
    m, k = x.shape
    return pl.pallas_call(
        _norm_kernel,
        grid=(m // tm,),
        in_specs=[pl.BlockSpec((tm, k), lambda i: (i, 0)), pl.BlockSpec((1, k), lambda i: (0, 0))],
        out_specs=pl.BlockSpec((tm, k), lambda i: (i, 0)),
        out_shape=jax.ShapeDtypeStruct((m, k), F32),
        compiler_params=_params(1),
        name="final_norm",
    )(x, g.reshape(1, k))


def _ffn_in_kernel(*refs, step_mode, tail_rows):
    if step_mode:
        x_ref, g_ref, wg_ref, wu_ref, cw_ref, cb_ref, prev_ref, act_ref, gt_ref, hn_ref = refs
    else:
        x_ref, g_ref, wg_ref, wu_ref, cw_ref, cb_ref, act_ref, gt_ref, hn_ref, ext_ref = refs
    tm = x_ref.shape[0]

    @pl.when(pl.program_id(1) == 0)
    def _():
        hn_ref[...] = _rms_rows(x_ref[...], g_ref[...]).astype(BF16)
        if not step_mode:
            ext_ref[0:SUBLANES, :] = jnp.zeros((SUBLANES, ext_ref.shape[1]), F32)

    hn = hn_ref[...]
    gate = _dot(hn, wg_ref[...].astype(BF16))
    up = _dot(hn, wu_ref[...].astype(BF16))
    cw = cw_ref[...]
    if step_mode:
        prev2 = prev_ref[0]
        prev1 = prev_ref[1]
    else:
        ext_ref[SUBLANES:, :] = gate
        prev1 = ext_ref[SUBLANES - 1:SUBLANES - 1 + tm, :]
        prev2 = ext_ref[SUBLANES - 2:SUBLANES - 2 + tm, :]
    conv = cb_ref[...] + gate * cw[2:3] + prev2 * cw[0:1] + prev1 * cw[1:2]
    act_ref[...] = (jax.nn.gelu(conv) * up).astype(BF16)
    gt_ref[...] = gate[tm - tail_rows:, :]


def ffn_in(x, g, w_in, conv_w, conv_b, li, prev, *, tm, tf):
    m, k = x.shape
    nf = D_FF // tf
    step_mode = prev is not None
    tail_rows = tm if step_mode else SUBLANES
    in_specs = [
        _row_tile_spec((tm, k), lambda i, f: (i, 0), F32),
        pl.BlockSpec((1, k), lambda i, f: (0, 0)),
        pl.BlockSpec((None, k, tf), lambda i, f: (li, 0, f)),
        pl.BlockSpec((None, k, tf), lambda i, f: (li, 0, nf + f)),
        pl.BlockSpec((None, CONV_F, tf), lambda i, f: (li, 0, f)),
        pl.BlockSpec((None, 1, tf), lambda i, f: (li, 0, f)),
    ]
    args = [x, g.reshape(1, k), w_in, w_in, conv_w, conv_b.reshape(DEPTH, 1, D_FF)]
    scratch = [pltpu.VMEM((tm, k), BF16)]
    if step_mode:
        in_specs.append(pl.BlockSpec((2, tm, tf), lambda i, f: (0, i, f)))
        args.append(prev)
    else:
        scratch.append(pltpu.VMEM((tm + SUBLANES, tf), F32))
    act, tail = pl.pallas_call(
        functools.partial(_ffn_in_kernel, step_mode=step_mode, tail_rows=tail_rows),
        grid=(m // tm, nf),
        in_specs=in_specs,
        out_specs=[
            pl.BlockSpec((tm, tf), lambda i, f: (i, f)),
            pl.BlockSpec((None, tail_rows, tf), lambda i, f: (i, 0, f)),
        ],
        out_shape=[jax.ShapeDtypeStruct((m, D_FF), BF16),
                   jax.ShapeDtypeStruct((m // tm, tail_rows, D_FF), F32)],
        scratch_shapes=scratch,
        compiler_params=_params(2),
        name="ffn_in",
    )(*args)
    return act, tail


def _lru_gates(xc, wg_ref, ba, bi, lam):
    lam_abs = jnp.abs(lam)
    softplus_neg = jnp.maximum(-lam, 0.0) + jnp.log(1.0 + jnp.exp(-lam_abs))
    a_parts, u_parts = [], []
    for gi in range(N_LRU_GROUPS):
        cols = slice(gi * LRU_GROUP, (gi + 1) * LRU_GROUP)
        xg = xc[:, cols]
        proj = _dot(xg.astype(BF16), wg_ref[gi])
        r = _sigmoid(proj[:, :LRU_GROUP] + ba[:, cols])
        i = _sigmoid(proj[:, LRU_GROUP:] + bi[:, cols])
        log_a = -LRU_C * r * softplus_neg[:, cols]
        a = jnp.exp(log_a)
        a_parts.append(a)
        u_parts.append(jnp.sqrt(1.0 - a * a) * (i * xg))
    return a_parts, u_parts


def _lru_seq_kernel(gate_ref, xr_ref, cw_ref, cb_ref, wg_ref, ba_ref, bi_ref, lam_ref,
                    y_ref, hlast_ref, ext_ref, a_ref, u_ref, carry_ref):
    tt = xr_ref.shape[0]
    n_grp = tt // SUBLANES

    @pl.when(pl.program_id(1) == 0)
    def _():
        ext_ref[0:SUBLANES, :] = jnp.zeros((SUBLANES, D_RNN), F32)
        carry_ref[...] = jnp.zeros((1, D_RNN), F32)

    xr = xr_ref[...]
    ext_ref[SUBLANES:, :] = xr
    cw = cw_ref[...]
    xc = cb_ref[...] + xr * cw[CONV_A - 1:CONV_A]
    for j in range(CONV_A - 1):
        off = SUBLANES - (CONV_A - 1) + j
        xc = xc + ext_ref[off:off + tt, :] * cw[j:j + 1]
    ext_ref[0:SUBLANES, :] = xr[tt - SUBLANES:, :]

    a_parts, u_parts = _lru_gates(xc, wg_ref, ba_ref[...], bi_ref[...], lam_ref[...])
    for gi in range(N_LRU_GROUPS):
        cols = slice(gi * LRU_GROUP, (gi + 1) * LRU_GROUP)
        a_ref[:, :, cols] = a_parts[gi].reshape(n_grp, SUBLANES, LRU_GROUP)
        u_ref[:, :, cols] = u_parts[gi].reshape(n_grp, SUBLANES, LRU_GROUP)

    sub = lax.broadcasted_iota(jnp.int32, (n_grp, SUBLANES, LANES), 1)
    for ci in range(D_RNN // LANES):
        cols = slice(ci * LANES, (ci + 1) * LANES)
        a3 = a_ref[:, :, cols]
        u3 = u_ref[:, :, cols]
        d = 1
        while d < SUBLANES:
            keep = sub >= d
            a_sh = jnp.where(keep, pltpu.roll(a3, d, 1), 1.0)
            u_sh = jnp.where(keep, pltpu.roll(u3, d, 1), 0.0)
            u3 = a3 * u_sh + u3
            a3 = a3 * a_sh
            d *= 2
        h_prev = jnp.broadcast_to(carry_ref[:, cols], (SUBLANES, LANES))
        for g in range(0, n_grp, 2):
            h0 = a3[g] * h_prev + u3[g]
            h_prev = jnp.broadcast_to(h0[SUBLANES - 1:, :], (SUBLANES, LANES))
            h1 = a3[g + 1] * h_prev + u3[g + 1]
            h_prev = jnp.broadcast_to(h1[SUBLANES - 1:, :], (SUBLANES, LANES))
            rows = slice(g * SUBLANES, (g + 2) * SUBLANES)
            h = jnp.concatenate([h0, h1], axis=0)
            y_ref[rows, cols] = (jax.nn.gelu(gate_ref[rows, cols]) * h).astype(BF16)
        carry_ref[:, cols] = h_prev[0:1, :]
        hlast_ref[:, cols] = h_prev[0:1, :]


def lru_seq(gx, conv_w, conv_b, wg, b_a, b_i, lam, *, batch, seq_len, tt):
    m = gx.shape[0]
    nt = seq_len // tt
    vec = lambda: pl.BlockSpec((1, D_RNN), lambda b, t: (0, 0))
    y, h_last = pl.pallas_call(
        _lru_seq_kernel,
        grid=(batch, nt),
        in_specs=[
            pl.BlockSpec((tt, D_RNN), lambda b, t: (b * nt + t, 0)),
            pl.BlockSpec((tt, D_RNN), lambda b, t: (b * nt + t, 1)),
            pl.BlockSpec((CONV_A, D_RNN), lambda b, t: (0, 0)),
            vec(),
            pl.BlockSpec((N_LRU_GROUPS, LRU_GROUP, 2 * LRU_GROUP), lambda b, t: (0, 0, 0)),
            vec(), vec(), vec(),
        ],
        out_specs=[
            pl.BlockSpec((tt, D_RNN), lambda b, t: (b * nt + t, 0)),
            pl.BlockSpec((None, 1, D_RNN), lambda b, t: (b, 0, 0)),
        ],
        out_shape=[jax.ShapeDtypeStruct((m, D_RNN), BF16), jax.ShapeDtypeStruct((batch, 1, D_RNN), F32)],
        scratch_shapes=[pltpu.VMEM((tt + SUBLANES, D_RNN), F32), pltpu.VMEM((tt // SUBLANES, SUBLANES, D_RNN), F32),
                        pltpu.VMEM((tt // SUBLANES, SUBLANES, D_RNN), F32), pltpu.VMEM((1, D_RNN), F32)],
        compiler_params=_params(2),
        name="lru_seq",
    )(gx, gx, conv_w, conv_b.reshape(1, D_RNN), wg, b_a.reshape(1, D_RNN), b_i.reshape(1, D_RNN),
      lam.reshape(1, D_RNN))
    return y, h_last.reshape(batch, D_RNN)


def _lru_step_kernel(gx_ref, h0_ref, cbuf_ref, cw_ref, cb_ref, wg_ref, ba_ref, bi_ref, lam_ref, y_ref, h_ref):
    gate = gx_ref[:, :D_RNN]
    xr = gx_ref[:, D_RNN:]
    cw = cw_ref[...]
    xc = cb_ref[...] + xr * cw[CONV_A - 1:CONV_A]
    for j in range(CONV_A - 1):
        xc = xc + cbuf_ref[j] * cw[j:j + 1]
    a_parts, u_parts = _lru_gates(xc, wg_ref, ba_ref[...], bi_ref[...], lam_ref[...])
    a = jnp.concatenate(a_parts, axis=1)
    u = jnp.concatenate(u_parts, axis=1)
    h = u + a * h0_ref[...]
    h_ref[...] = h
    y_ref[...] = (jax.nn.gelu(gate) * h).astype(BF16)


def lru_step(gx, h0, cbuf, conv_w, conv_b, wg, b_a, b_i, lam):
    m = gx.shape[0]
    return pl.pallas_call(
        _lru_step_kernel,
        out_shape=[jax.ShapeDtypeStruct((m, D_RNN), BF16), jax.ShapeDtypeStruct((m, D_RNN), F32)],
        compiler_params=pltpu.CompilerParams(vmem_limit_bytes=V7X_VMEM_LIMIT),
        name="lru_step",
    )(gx, h0, cbuf, conv_w, conv_b.reshape(1, D_RNN), wg, b_a.reshape(1, D_RNN), b_i.reshape(1, D_RNN),
      lam.reshape(1, D_RNN))


def _lru_gate_weights(w_a, w_i):
    per = LRU_GROUP // LRU_BLOCK

    def blockdiag(w):
        w = w.reshape(N_LRU_GROUPS, per, LRU_BLOCK, LRU_BLOCK)
        eye = jnp.eye(per, dtype=w.dtype)
        return jnp.einsum('gpcd,pq->gpcqd', w, eye).reshape(N_LRU_GROUPS, LRU_GROUP, LRU_GROUP)

    return jnp.concatenate([blockdiag(w_a), blockdiag(w_i)], axis=2).astype(BF16)


def _rope_tables(pos):
    half = ROT_DIM // 2
    inv = ROPE_THETA ** (-jnp.arange(half, dtype=F32) * (2.0 / ROT_DIM))
    ang = pos[:, None] * inv[None, :]
    cos, sin = jnp.cos(ang), jnp.sin(ang)
    rows = pos.shape[0]
    ones = jnp.ones((rows, HEAD_DIM - ROT_DIM), F32)
    zeros_h = jnp.zeros((rows, half), F32)
    zeros_t = jnp.zeros((rows, HEAD_DIM - ROT_DIM), F32)
    c = jnp.concatenate([cos, cos, ones], axis=1)
    s_lo = jnp.concatenate([-sin, zeros_h, zeros_t], axis=1)
    s_hi = jnp.concatenate([zeros_h, sin, zeros_t], axis=1)
    rep = LANES // HEAD_DIM
    return tuple(jnp.tile(t, (1, rep)) for t in (c, s_lo, s_hi))


def _rope_kernel(qk_ref, c_ref, slo_ref, shi_ref, q_ref, k_ref):
    c, s_lo, s_hi = c_ref[...], slo_ref[...], shi_ref[...]
    half = ROT_DIM // 2
    n_q = N_HEADS * HEAD_DIM // LANES
    for ci in range(QK_COLS // LANES):
        x = qk_ref[:, ci * LANES:(ci + 1) * LANES]
        rot = x * c + pltpu.roll(x, LANES - half, 1) * s_lo + pltpu.roll(x, half, 1) * s_hi
        if ci < n_q:
            q_ref[:, ci * LANES:(ci + 1) * LANES] = rot.astype(BF16)
        else:
            k_ref[:, (ci - n_q) * LANES:(ci - n_q + 1) * LANES] = rot


def rope(qkv, tables, *, tm):
    m = qkv.shape[0]
    nt = tables[0].shape[0] // tm
    tab = lambda: pl.BlockSpec((tm, LANES), lambda i: (i % nt, 0))
    return pl.pallas_call(
        _rope_kernel,
        grid=(m // tm,),
        in_specs=[pl.BlockSpec((tm, QK_COLS), lambda i: (i, 0)), tab(), tab(), tab()],
        out_specs=[pl.BlockSpec((tm, N_HEADS * HEAD_DIM), lambda i: (i, 0)),
                   pl.BlockSpec((tm, KV_COLS), lambda i: (i, 0))],
        out_shape=[jax.ShapeDtypeStruct((m, N_HEADS * HEAD_DIM), BF16), jax.ShapeDtypeStruct((m, KV_COLS), F32)],
        compiler_params=_params(1),
        name="rope",
    )(qkv, *tables)


def _swa_seq_kernel(sink_ref, q_ref, kp_ref, kc_ref, vp_ref, vc_ref, o_ref):
    qi = pl.program_id(1)
    tq = q_ref.shape[0]
    kk = jnp.concatenate([kp_ref[...], kc_ref[...]], axis=0).astype(BF16)
    vv = jnp.concatenate([vp_ref[...], vc_ref[...]], axis=0).astype(BF16)
    row = lax.broadcasted_iota(jnp.int32, (tq, 2 * tq), 0)
    col = lax.broadcasted_iota(jnp.int32, (tq, 2 * tq), 1)
    rel = tq + row - col
    valid = (rel >= 0) & (rel < WINDOW) & ((qi > 0) | (col >= tq))
    for h in range(N_HEADS):
        kh = h // GROUP
        kcols = slice(kh * HEAD_DIM, (kh + 1) * HEAD_DIM)
        hcols = slice(h * HEAD_DIM, (h + 1) * HEAD_DIM)
        s = _dot_nt(q_ref[:, hcols], kk[:, kcols]) * (HEAD_DIM ** -0.5)
        s = jnp.where(valid, s, -jnp.inf)
        sink = sink_ref[h]
        mx = jnp.maximum(jnp.max(s, axis=-1, keepdims=True), sink)
        p = jnp.exp(s - mx)
        denom = jnp.sum(p, axis=-1, keepdims=True) + jnp.exp(sink - mx)
        o = _dot(p.astype(BF16), vv[:, kcols]) / denom
        o_ref[:, hcols] = o.astype(BF16)


def swa_seq(q_rot, k_rot, qkv, sinks, *, batch, seq_len):
    m = q_rot.shape[0]
    tq = WINDOW
    nq = seq_len // tq
    v_col = QK_COLS // KV_COLS
    prev = lambda col: pl.BlockSpec((tq, KV_COLS), lambda b, i: (b * nq + jnp.maximum(i - 1, 0), col))
    cur = lambda col: pl.BlockSpec((tq, KV_COLS), lambda b, i: (b * nq + i, col))
    return pl.pallas_call(
        _swa_seq_kernel,
        grid=(batch, nq),
        in_specs=[
            pl.BlockSpec(memory_space=pltpu.SMEM),
            pl.BlockSpec((tq, N_HEADS * HEAD_DIM), lambda b, i: (b * nq + i, 0)),
            prev(0), cur(0), prev(v_col), cur(v_col),
        ],
        out_specs=pl.BlockSpec((tq, N_HEADS * HEAD_DIM), lambda b, i: (b * nq + i, 0)),
        out_shape=jax.ShapeDtypeStruct((m, N_HEADS * HEAD_DIM), BF16),
        compiler_params=_params(2),
        name="swa_seq",
    )(sinks, q_rot, k_rot, k_rot, qkv, qkv)


def _swa_step_kernel(q_ref, kn_ref, vn_ref, ck_ref, cv_ref, sink_ref, o_ref):
    ck = ck_ref[...].astype(BF16)
    cv = cv_ref[...].astype(BF16)
    col = lax.broadcasted_iota(jnp.int32, (GROUP, WINDOW), 1)
    for kh in range(N_KV):
        kcols = slice(kh * HEAD_DIM, (kh + 1) * HEAD_DIM)
        heads = slice(kh * GROUP, (kh + 1) * GROUP)
        q = q_ref[heads, :]
        k_new = kn_ref[kh:kh + 1, :].astype(BF16).astype(F32)
        v_new = vn_ref[kh:kh + 1, :].astype(BF16).astype(F32)
        scale = HEAD_DIM ** -0.5
        s_c = _dot_nt(q, ck[:, kcols]) * scale
        s_c = jnp.where(col >= 1, s_c, -jnp.inf)
        s_n = jnp.sum(q.astype(F32) * k_new, axis=-1, keepdims=True) * scale
        sink = sink_ref[heads, :]
        mx = jnp.maximum(jnp.maximum(jnp.max(s_c, axis=-1, keepdims=True), s_n), sink)
        p_c = jnp.exp(s_c - mx)
        p_n = jnp.exp(s_n - mx)
        denom = jnp.sum(p_c, axis=-1, keepdims=True) + p_n + jnp.exp(sink - mx)
        o = _dot(p_c.astype(BF16), cv[:, kcols]) + p_n.astype(BF16).astype(F32) * v_new
        o_ref[heads, :] = (o / denom).astype(BF16)


def swa_step(q3, k_new, v_new, cache_k, cache_v, sinks):
    b = q3.shape[0]
    return pl.pallas_call(
        _swa_step_kernel,
        grid=(b,),
        in_specs=[
            pl.BlockSpec((None, N_HEADS, HEAD_DIM), lambda i: (i, 0, 0)),
            pl.BlockSpec((None, N_KV, HEAD_DIM), lambda i: (i, 0, 0)),
            pl.BlockSpec((None, N_KV, HEAD_DIM), lambda i: (i, 0, 0)),
            pl.BlockSpec((None, WINDOW, KV_COLS), lambda i: (i, 0, 0)),
            pl.BlockSpec((None, WINDOW, KV_COLS), lambda i: (i, 0, 0)),
            pl.BlockSpec((N_HEADS, 1), lambda i: (0, 0)),
        ],
        out_specs=pl.BlockSpec((None, N_HEADS, HEAD_DIM), lambda i: (i, 0, 0)),
        out_shape=jax.ShapeDtypeStruct((b, N_HEADS, HEAD_DIM), BF16),
        compiler_params=_params(1),
        name="swa_step",
    )(q3, k_new, v_new, cache_k, cache_v, sinks.reshape(N_HEADS, 1))


def _hgrn_lower_bound(logits, layer):
    mx = jnp.max(logits, axis=0, keepdims=True)
    e = jnp.exp(logits - mx)
    sm = e / jnp.sum(e, axis=0, keepdims=True)
    lb = jnp.zeros_like(sm[0:1])
    for i in range(1, layer + 1):
        lb = lb + sm[i:i + 1]
    return lb


def _hgrn_consts():
    c = HG_CHUNK
    t = np.arange(c)[:, None]
    s = np.arange(c)[None, :]
    tri = (s <= t).astype(np.float32)
    sel = [tri]
    msk = [(s == t)]
    for lvl in range(1, HG_LEVELS + 1):
        w = 1 << (lvl - 1)
        ref_row = (t // (2 * w)) * (2 * w) + w - 1
        upper_t = (t % (2 * w)) >= w
        lower_s = (s % (2 * w)) < w
        sign = np.where(upper_t, 1.0, -1.0)
        sel.append(sign * (tri - (s <= ref_row)))
        msk.append(((t // (2 * w)) == (s // (2 * w))) & upper_t & lower_s)
    sel.append(1.0 - tri)
    sel = np.concatenate(sel, axis=0).astype(np.float32)
    sel2 = np.concatenate([sel, sel], axis=1)
    msk = np.stack(msk, axis=0).astype(np.float32)
    return jnp.asarray(sel2, BF16), jnp.asarray(msk, F32)


def _hgrn_seq_kernel(q_ref, z_ref, v_ref, g_ref, lbl_ref, gn_ref, sel_ref, msk_ref, y_ref, s_out_ref, st_ref,
                     *, layer, heads_per_step):
    ti = pl.program_id(2)
    tt = q_ref.shape[0]
    c = HG_CHUNK

    @pl.when(ti == 0)
    def _():
        st_ref[...] = jnp.zeros(st_ref.shape, F32)

    lb_all = _hgrn_lower_bound(lbl_ref[...], layer)
    gn = gn_ref[...]
    row = lax.broadcasted_iota(jnp.int32, (c, HG_DK), 0)
    upper = [None] + [((row >> (lvl - 1)) & 1) == 1 for lvl in range(1, HG_LEVELS + 1)]
    states = [st_ref[hh] for hh in range(heads_per_step)]

    for ci in range(tt // c):
        rows = slice(ci * c, (ci + 1) * c)
        for hh in range(heads_per_step):
            cols = slice(hh * HG_DK, (hh + 1) * HG_DK)
            lb = lb_all[:, cols]
            q = q_ref[rows, cols]
            v = v_ref[rows, cols].astype(BF16)
            sq = q * _sigmoid(q)
            sz = _sigmoid(z_ref[rows, cols])
            log_f = jnp.log(lb + (1.0 - lb) * sz)
            k = (1.0 - lb) * (1.0 - sz)
            hi = log_f.astype(BF16)
            lo = (log_f - hi.astype(F32)).astype(BF16)
            ex = _dot(sel_ref[...], jnp.concatenate([hi, lo], axis=0))
            e_b = jnp.exp(ex[0:c])
            att = _dot_nt(sq.astype(BF16), k.astype(BF16)) * msk_ref[0]
            for lvl in range(1, HG_LEVELS + 1):
                x = (jnp.where(upper[lvl], sq, k) * jnp.exp(ex[lvl * c:(lvl + 1) * c])).astype(BF16)
                att = att + _dot_nt(x, x) * msk_ref[lvl]
            st = states[hh]
            o = _dot(att.astype(BF16), v) + _dot_nt((sq * e_b).astype(BF16), st.astype(BF16))
            k_dec = (k * jnp.exp(ex[(HG_LEVELS + 1) * c:])).astype(BF16)
            states[hh] = st * e_b[c - 1:c, :] + _dot_tn(v, k_dec)
            g = g_ref[rows, cols]
            y_ref[rows, cols] = (_rms_rows(o, gn) * (g * _sigmoid(g))).astype(BF16)

    for hh in range(heads_per_step):
        st_ref[hh] = states[hh]

    @pl.when(ti == pl.num_programs(2) - 1)
    def _():
        for hh in range(heads_per_step):
            s_out_ref[hh] = states[hh].T


def hgrn_seq(qzvg, lb_logits, g_norm, *, layer, batch, seq_len, tt, heads_per_step):
    m = qzvg.shape[0]
    nt = seq_len // tt
    hw = heads_per_step * HG_DK
    nh = HG_HEADS // heads_per_step
    sel, msk = _hgrn_consts()
    part = lambda p: pl.BlockSpec((tt, hw), lambda b, h, t: (b * nt + t, p * nh + h))
    y, s_out = pl.pallas_call(
        functools.partial(_hgrn_seq_kernel, layer=layer, heads_per_step=heads_per_step),
        grid=(batch, nh, nt),
        in_specs=[
            part(0), part(1), part(2), part(3),
            pl.BlockSpec((DEPTH, hw), lambda b, h, t: (0, h)),
            pl.BlockSpec((1, HG_DV), lambda b, h, t: (0, 0)),
            pl.BlockSpec(sel.shape, lambda b, h, t: (0, 0)),
            pl.BlockSpec(msk.shape, lambda b, h, t: (0, 0, 0)),
        ],
        out_specs=[
            pl.BlockSpec((tt, hw), lambda b, h, t: (b * nt + t, h)),
            pl.BlockSpec((None, heads_per_step, HG_DK, HG_DV), lambda b, h, t: (b, h, 0, 0)),
        ],
        out_shape=[jax.ShapeDtypeStruct((m, HG_HEADS * HG_DV), BF16),
                   jax.ShapeDtypeStruct((batch, HG_HEADS, HG_DK, HG_DV), F32)],
        scratch_shapes=[pltpu.VMEM((heads_per_step, HG_DV, HG_DK), F32)],
        compiler_params=_params(3),
        name="hgrn_seq",
    )(qzvg, qzvg, qzvg, qzvg, lb_logits, g_norm.reshape(1, HG_DV), sel, msk)
    return y, s_out


def _hgrn_step_kernel(x_ref, s_ref, lbl_ref, gn_ref, y_ref, s_out_ref, *, layer):
    nh = HG_HEADS
    q = x_ref[0:nh, :]
    z = x_ref[nh:2 * nh, :]
    v = x_ref[2 * nh:3 * nh, :]
    g = x_ref[3 * nh:4 * nh, :]
    lb = _hgrn_lower_bound(lbl_ref[...], layer)[0]
    sq = q * _sigmoid(q)
    sz = _sigmoid(z)
    f = lb + (1.0 - lb) * sz
    k = (1.0 - lb) * (1.0 - sz)
    pad = jnp.zeros((LANES - 3 * nh, HG_DK), F32)
    cols = jnp.concatenate([f, k, sq, pad], axis=0).T
    outs = []
    for h in range(nh):
        f_col = cols[:, h:h + 1]
        k_col = cols[:, nh + h:nh + h + 1]
        q_col = cols[:, 2 * nh + h:2 * nh + h + 1]
        s_new = s_ref[h] * f_col + k_col * v[h:h + 1, :]
        s_out_ref[h] = s_new
        outs.append(jnp.sum(s_new * q_col, axis=0, keepdims=True))
    o = jnp.concatenate(outs, axis=0)
    y_ref[...] = (_rms_rows(o, gn_ref[...]) * (g * _sigmoid(g))).astype(BF16)


def hgrn_step(x4, state, lb_logits, g_norm, *, layer):
    b = x4.shape[0]
    return pl.pallas_call(
        functools.partial(_hgrn_step_kernel, layer=layer),
        grid=(b,),
        in_specs=[
            pl.BlockSpec((None, 4 * HG_HEADS, HG_DK), lambda i: (i, 0, 0)),
            pl.BlockSpec((None, HG_HEADS, HG_DK, HG_DV), lambda i: (i, 0, 0, 0)),
            pl.BlockSpec((DEPTH, HG_HEADS, HG_DK), lambda i: (0, 0, 0)),
            pl.BlockSpec((1, HG_DV), lambda i: (0, 0)),
        ],
        out_specs=[
            pl.BlockSpec((None, HG_HEADS, HG_DV), lambda i: (i, 0, 0)),
            pl.BlockSpec((None, HG_HEADS, HG_DK, HG_DV), lambda i: (i, 0, 0, 0)),
        ],
        out_shape=[jax.ShapeDtypeStruct((b, HG_HEADS, HG_DV), BF16),
                   jax.ShapeDtypeStruct((b, HG_HEADS, HG_DK, HG_DV), F32)],
        compiler_params=_params(1),
        name="hgrn_step",
    )(x4, state, lb_logits.reshape(DEPTH, HG_HEADS, HG_DK), g_norm.reshape(1, HG_DV))


def _trunk(x3, states, w, *, step_mode):
    batch, seq_len, _ = x3.shape
    m = batch * seq_len
    x = x3.reshape(m, D_MODEL)
    tm = m if step_mode else seq_len
    tn_in = 1280 if step_mode else 512
    tn_out = 1024 if step_mode else 256
    tf = 512 if step_mode else 256
    zeros = lambda n: jnp.zeros((n,), F32)
    lru_wg = [_lru_gate_weights(w["lru_w_a"][j], w["lru_w_i"][j]) for j in range(w["lru_w_a"].shape[0])]
    if step_mode:
        pos = jnp.full((m,), PAST_LEN, F32)
    else:
        pos = jnp.arange(seq_len, dtype=F32)
    rope_tables = _rope_tables(pos)

    def out_proj(a, w_stack, li, b, res):
        k_blocks = 1
        while tm * (a.shape[1] // k_blocks) * a.dtype.itemsize >= SINGLE_BUFFER_BYTES:
            k_blocks *= 2
        k = a.shape[1] // k_blocks
        cap = max(LANES, min(tn_out, W_TILE_BYTES // (4 * k) // LANES * LANES))
        for ki in range(k_blocks):
            bias = b if ki == k_blocks - 1 else jnp.zeros_like(b)
            res = matmul_residual(a, w_stack, li, bias, res, tm=tm, tn=_largest_tile(D_MODEL, cap),
                                  k_blocks=k_blocks, k_index=ki)
        return res

    new = {"lru_h": [], "lru_conv": [], "swa_k": [], "swa_v": [], "hgrn": [], "ffn_conv": []}
    for layer in range(DEPTH):
        kind, j = LAYER_MIXER[layer], LAYER_SLOT[layer]
        g_mix = w["norm_mix"][layer]
        if kind == 0:
            n = 2 * D_RNN
            gx = norm_matmul(x, g_mix, w["lru_w_in"], j, zeros(n), tm=tm, tn=_largest_tile(n, tn_in))
            if step_mode:
                cbuf = states["lru_conv"][j]
                y, h_new = lru_step(gx, states["lru_h"][j], jnp.swapaxes(cbuf, 0, 1), w["lru_conv_w"][j],
                                    w["lru_conv_b"][j], lru_wg[j], w["lru_b_a"][j], w["lru_b_i"][j],
                                    w["lru_lambda"][j])
                c_new = jnp.concatenate([cbuf[:, 1:], gx[:, None, D_RNN:]], axis=1)
            else:
                y, h_new = lru_seq(gx, w["lru_conv_w"][j], w["lru_conv_b"][j], lru_wg[j], w["lru_b_a"][j],
                                   w["lru_b_i"][j], w["lru_lambda"][j], batch=batch, seq_len=seq_len, tt=256)
                c_new = gx.reshape(batch, seq_len, n)[:, seq_len - (CONV_A - 1):, D_RNN:]
            new["lru_h"].append(h_new)
            new["lru_conv"].append(c_new)
            x = out_proj(y, w["lru_w_out"], j, zeros(D_MODEL), x)
        elif kind == 1:
            n = QK_COLS + KV_COLS
            qkv = norm_matmul(x, g_mix, w["swa_w_qkv"], j, w["swa_b_qkv"][j], tm=tm, tn=_largest_tile(n, tn_in))
            q_rot, k_rot = rope(qkv, rope_tables, tm=min(tm, 1024))
            v = qkv[:, QK_COLS:]
            if step_mode:
                ck = states["swa_k"][j]
                cv = states["swa_v"][j]
                k_new = k_rot.reshape(m, N_KV, HEAD_DIM)
                v_new = v.reshape(m, N_KV, HEAD_DIM)
                o = swa_step(q_rot.reshape(m, N_HEADS, HEAD_DIM), k_new, v_new,
                             ck.reshape(m, WINDOW, KV_COLS), cv.reshape(m, WINDOW, KV_COLS), w["swa_sinks"][j])
                o = o.reshape(m, N_HEADS * HEAD_DIM)
                k_cache = jnp.concatenate([ck[:, 1:], k_new[:, None]], axis=1)
                v_cache = jnp.concatenate([cv[:, 1:], v_new[:, None]], axis=1)
            else:
                o = swa_seq(q_rot, k_rot, qkv, w["swa_sinks"][j], batch=batch, seq_len=seq_len)
                k_cache = k_rot.reshape(batch, seq_len, N_KV, HEAD_DIM)[:, seq_len - WINDOW:]
                v_cache = v.reshape(batch, seq_len, N_KV, HEAD_DIM)[:, seq_len - WINDOW:]
            new["swa_k"].append(k_cache)
            new["swa_v"].append(v_cache)
            x = out_proj(o, w["swa_w_o"], j, w["swa_b_o"][j], x)
        else:
            n = 2 * HG_HEADS * HG_DK + 2 * HG_HEADS * HG_DV
            qzvg = norm_matmul(x, g_mix, w["hg_w_in"], j, zeros(n), tm=tm, tn=_largest_tile(n, tn_in))
            if step_mode:
                y, s_new = hgrn_step(qzvg.reshape(m, 4 * HG_HEADS, HG_DK), states["hgrn"][j], w["hg_lb_logits"],
                                     w["hg_norm"][j], layer=layer)
                y = y.reshape(m, HG_HEADS * HG_DV)
            else:
                y, s_new = hgrn_seq(qzvg, w["hg_lb_logits"], w["hg_norm"][j], layer=layer, batch=batch,
                                    seq_len=seq_len, tt=256, heads_per_step=4)
            new["hgrn"].append(s_new)
            x = out_proj(y, w["hg_w_o"], j, zeros(D_MODEL), x)

        if step_mode:
            fbuf = states["ffn_conv"][layer]
            act, gate = ffn_in(x, w["norm_ffn"][layer], w["ffn_w_in"], w["ffn_conv_w"], w["ffn_conv_b"], layer,
                               jnp.swapaxes(fbuf, 0, 1), tm=tm, tf=tf)
            f_new = jnp.concatenate([fbuf[:, 1:], gate.reshape(m, 1, D_FF)], axis=1)
        else:
            act, tail = ffn_in(x, w["norm_ffn"][layer], w["ffn_w_in"], w["ffn_conv_w"], w["ffn_conv_b"], layer,
                               None, tm=tm, tf=tf)
            f_new = tail[:, SUBLANES - (CONV_F - 1):]
        new["ffn_conv"].append(f_new)
        x = out_proj(act, w["ffn_w_out"], layer, zeros(D_MODEL), x)

    y = final_norm(x, w["norm_final"], tm=min(tm, 1024)).reshape(batch, seq_len, D_MODEL)
    return (y, jnp.stack(new["lru_h"]), jnp.stack(new["lru_conv"]), jnp.stack(new["swa_k"]),
            jnp.stack(new["swa_v"]), jnp.stack(new["hgrn"]), jnp.stack(new["ffn_conv"]))


def kernel(x_prompt, x_sample, state_lru_h, state_lru_conv, cache_swa_k, cache_swa_v, state_hgrn, state_ffn_conv,
           norm_mix, norm_ffn, norm_final,
           lru_w_in, lru_conv_w, lru_conv_b, lru_w_a, lru_b_a, lru_w_i, lru_b_i, lru_lambda, lru_w_out,
           swa_w_qkv, swa_b_qkv, swa_sinks, swa_w_o, swa_b_o,
           hg_w_in, hg_lb_logits, hg_norm, hg_w_o,
           ffn_w_in, ffn_conv_w, ffn_conv_b, ffn_w_out):
    w = dict(norm_mix=norm_mix, norm_ffn=norm_ffn, norm_final=norm_final,
             lru_w_in=lru_w_in, lru_conv_w=lru_conv_w, lru_conv_b=lru_conv_b, lru_w_a=lru_w_a, lru_b_a=lru_b_a,
             lru_w_i=lru_w_i, lru_b_i=lru_b_i, lru_lambda=lru_lambda, lru_w_out=lru_w_out,
             swa_w_qkv=swa_w_qkv, swa_b_qkv=swa_b_qkv, swa_sinks=swa_sinks, swa_w_o=swa_w_o, swa_b_o=swa_b_o,
             hg_w_in=hg_w_in, hg_lb_logits=hg_lb_logits, hg_norm=hg_norm, hg_w_o=hg_w_o,
             ffn_w_in=ffn_w_in, ffn_conv_w=ffn_conv_w, ffn_conv_b=ffn_conv_b, ffn_w_out=ffn_w_out)
    states = dict(lru_h=state_lru_h, lru_conv=state_lru_conv, swa_k=cache_swa_k, swa_v=cache_swa_v,
                  hgrn=state_hgrn, ffn_conv=state_ffn_conv)
    prompt = _trunk(x_prompt, None, w, step_mode=False)
    sample = _trunk(x_sample, states, w, step_mode=True)
    return (prompt[0], sample[0]) + prompt[1:] + sample[1:]
```

```python
import functools

import numpy as np
import jax
import jax.numpy as jnp
from jax import lax
from jax.experimental import pallas as pl
from jax.experimental.pallas import tpu as pltpu

F32 = jnp.float32
BF16 = jnp.bfloat16

D_MODEL = 2048
DEPTH = 4
PAST_LEN = 16384
EPS = 1e-6
LAYER_MIXER = tuple(i % 3 for i in range(DEPTH))
LAYER_SLOT = tuple(LAYER_MIXER[:i].count(LAYER_MIXER[i]) for i in range(DEPTH))

D_RNN = 2560
LRU_BLOCKS = 16
LRU_BLOCK = D_RNN // LRU_BLOCKS
LRU_GROUP = 640
N_LRU_GROUPS = D_RNN // LRU_GROUP
CONV_A = 4
LRU_C = 8.0

N_HEADS = 32
N_KV = 4
HEAD_DIM = 64
GROUP = N_HEADS // N_KV
WINDOW = 128
ROT_DIM = HEAD_DIM // 4
ROPE_THETA = 500000.0
QK_COLS = (N_HEADS + N_KV) * HEAD_DIM
KV_COLS = N_KV * HEAD_DIM

HG_HEADS = 16
HG_DK = 128
HG_DV = 128
HG_CHUNK = 64
HG_LEVELS = 6

D_FF = 3 * D_MODEL
CONV_F = 3

LANES = 128
SUBLANES = 8
V7X_VMEM_LIMIT = 56 * 1024 * 1024
SINGLE_BUFFER_BYTES = 14 * 1024 * 1024
W_TILE_BYTES = 8 * 1024 * 1024
DOT_ROWS = 512


def _params(n_axes, vmem=V7X_VMEM_LIMIT):
    return pltpu.CompilerParams(dimension_semantics=("arbitrary",) * n_axes, vmem_limit_bytes=vmem)


def _sigmoid(x):
    return 1.0 / (1.0 + jnp.exp(-x))


def _rms_rows(x, g):
    var = jnp.mean(x * x, axis=-1, keepdims=True)
    return x * lax.rsqrt(var + EPS) * g


def _dot(a, b):
    return jnp.dot(a, b, preferred_element_type=F32)


def _dot_nt(a, b):
    return lax.dot_general(a, b, (((1,), (1,)), ((), ())), preferred_element_type=F32)


def _dot_tn(a, b):
    return lax.dot_general(a, b, (((0,), (0,)), ((), ())), preferred_element_type=F32)


def _largest_tile(n, cap):
    t = cap
    while n % t:
        t -= LANES
    return t


def _row_tile_spec(shape, index_map, dtype):
    if shape[0] * shape[1] * jnp.dtype(dtype).itemsize >= SINGLE_BUFFER_BYTES:
        return pl.BlockSpec(shape, index_map, pipeline_mode=pl.Buffered(1))
    return pl.BlockSpec(shape, index_map)


def _row_blocks(n_rows):
    step = min(n_rows, DOT_ROWS)
    return [slice(r, r + step) for r in range(0, n_rows, step)]


def _norm_matmul_kernel(x_ref, g_ref, w_ref, b_ref, o_ref, hn_ref):
    @pl.when(pl.program_id(1) == 0)
    def _():
        for rows in _row_blocks(x_ref.shape[0]):
            hn_ref[rows, :] = _rms_rows(x_ref[rows, :], g_ref[...]).astype(BF16)

    w = w_ref[...].astype(BF16)
    for rows in _row_blocks(o_ref.shape[0]):
        o_ref[rows, :] = _dot(hn_ref[rows, :], w) + b_ref[...]


def norm_matmul(x, g, w_stack, li, b, *, tm, tn):
    m, k = x.shape
    n = w_stack.shape[2]
    return pl.pallas_call(
        _norm_matmul_kernel,
        grid=(m // tm, n // tn),
        in_specs=[
            _row_tile_spec((tm, k), lambda i, j: (i, 0), F32),
            pl.BlockSpec((1, k), lambda i, j: (0, 0)),
            pl.BlockSpec((None, k, tn), lambda i, j: (li, 0, j)),
            pl.BlockSpec((1, tn), lambda i, j: (0, j)),
        ],
        out_specs=pl.BlockSpec((tm, tn), lambda i, j: (i, j)),
        out_shape=jax.ShapeDtypeStruct((m, n), F32),
        scratch_shapes=[pltpu.VMEM((tm, k), BF16)],
        compiler_params=_params(2),
        name="norm_matmul",
    )(x, g.reshape(1, k), w_stack, b.reshape(1, n))


def _matmul_residual_kernel(a_ref, w_ref, b_ref, x_ref, o_ref):
    w = w_ref[...].astype(BF16)
    for rows in _row_blocks(o_ref.shape[0]):
        o_ref[rows, :] = x_ref[rows, :] + (_dot(a_ref[rows, :], w) + b_ref[...])


def matmul_residual(a, w_stack, li, b, x, *, tm, tn, k_blocks=1, k_index=0):
    m = a.shape[0]
    k = a.shape[1] // k_blocks
    n = w_stack.shape[2]
    return pl.pallas_call(
        _matmul_residual_kernel,
        grid=(m // tm, n // tn),
        in_specs=[
            _row_tile_spec((tm, k), lambda i, j: (i, k_index), a.dtype),
            pl.BlockSpec((None, k, tn), lambda i, j: (li, k_index, j)),
            pl.BlockSpec((1, tn), lambda i, j: (0, j)),
            pl.BlockSpec((tm, tn), lambda i, j: (i, j)),
        ],
        out_specs=pl.BlockSpec((tm, tn), lambda i, j: (i, j)),
        out_shape=jax.ShapeDtypeStruct((m, n), F32),
        compiler_params=_params(2),
        name="matmul_residual",
    )(a, w_stack, b.reshape(1, n), x)


def _norm_kernel(x_ref, g_ref, o_ref):
    o_ref[...] = _rms_rows(x_ref[...], g_ref[...])


def final_norm(x, g, *, tm):
    m, k = x.shape
    return pl.pallas_call(
        _norm_kernel,
        grid=(m // tm,),
        in_specs=[pl.BlockSpec((tm, k), lambda i: (i, 0)), pl.BlockSpec((1, k), lambda i: (0, 0))],
        out_specs=pl.BlockSpec((tm, k), lambda i: (i, 0)),
        out_shape=jax.ShapeDtypeStruct((m, k), F32),
        compiler_params=_params(1),
        name="final_norm",
    )(x, g.reshape(1, k))


def _ffn_in_kernel(*refs, step_mode, tail_rows):
    if step_mode:
        x_ref, g_ref, wg_ref, wu_ref, cw_ref, cb_ref, prev_ref, act_ref, gt_ref, hn_ref = refs
    else:
        x_ref, g_ref, wg_ref, wu_ref, cw_ref, cb_ref, act_ref, gt_ref, hn_ref, ext_ref = refs
    tm = x_ref.shape[0]

    @pl.when(pl.program_id(1) == 0)
    def _():
        for rows in _row_blocks(tm):
            hn_ref[rows, :] = _rms_rows(x_ref[rows, :], g_ref[...]).astype(BF16)
        if not step_mode:
            ext_ref[0:SUBLANES, :] = jnp.zeros((SUBLANES, ext_ref.shape[1]), F32)

    wg = wg_ref[...].astype(BF16)
    wu = wu_ref[...].astype(BF16)
    cw = cw_ref[...]
    for rows in _row_blocks(tm):
        r0, n = rows.start, rows.stop - rows.start
        hn = hn_ref[rows, :]
        gate = _dot(hn, wg)
        up = _dot(hn, wu)
        if step_mode:
            prev2 = prev_ref[0]
            prev1 = prev_ref[1]
        else:
            ext_ref[SUBLANES + r0:SUBLANES + r0 + n, :] = gate
            prev1 = ext_ref[SUBLANES - 1 + r0:SUBLANES - 1 + r0 + n, :]
            prev2 = ext_ref[SUBLANES - 2 + r0:SUBLANES - 2 + r0 + n, :]
        conv = cb_ref[...] + gate * cw[2:3] + prev2 * cw[0:1] + prev1 * cw[1:2]
        act_ref[rows, :] = (jax.nn.gelu(conv) * up).astype(BF16)
        if rows.stop == tm:
            gt_ref[...] = gate[n - tail_rows:, :]


def ffn_in(x, g, w_in, conv_w, conv_b, li, prev, *, tm, tf):
    m, k = x.shape
    nf = D_FF // tf
    step_mode = prev is not None
    tail_rows = tm if step_mode else SUBLANES
    in_specs = [
        _row_tile_spec((tm, k), lambda i, f: (i, 0), F32),
        pl.BlockSpec((1, k), lambda i, f: (0, 0)),
        pl.BlockSpec((None, k, tf), lambda i, f: (li, 0, f)),
        pl.BlockSpec((None, k, tf), lambda i, f: (li, 0, nf + f)),
        pl.BlockSpec((None, CONV_F, tf), lambda i, f: (li, 0, f)),
        pl.BlockSpec((None, 1, tf), lambda i, f: (li, 0, f)),
    ]
    args = [x, g.reshape(1, k), w_in, w_in, conv_w, conv_b.reshape(DEPTH, 1, D_FF)]
    scratch = [pltpu.VMEM((tm, k), BF16)]
    if step_mode:
        in_specs.append(pl.BlockSpec((2, tm, tf), lambda i, f: (0, i, f)))
        args.append(prev)
    else:
        scratch.append(pltpu.VMEM((tm + SUBLANES, tf), F32))
    act, tail = pl.pallas_call(
        functools.partial(_ffn_in_kernel, step_mode=step_mode, tail_rows=tail_rows),
        grid=(m // tm, nf),
        in_specs=in_specs,
        out_specs=[
            pl.BlockSpec((tm, tf), lambda i, f: (i, f)),
            pl.BlockSpec((None, tail_rows, tf), lambda i, f: (i, 0, f)),
        ],
        out_shape=[jax.ShapeDtypeStruct((m, D_FF), BF16),
                   jax.ShapeDtypeStruct((m // tm, tail_rows, D_FF), F32)],
        scratch_shapes=scratch,
        compiler_params=_params(2),
        name="ffn_in",
    )(*args)
    return act, tail


def _lru_gates(xc, wg_ref, ba, bi, lam):
    lam_abs = jnp.abs(lam)
    softplus_neg = jnp.maximum(-lam, 0.0) + jnp.log(1.0 + jnp.exp(-lam_abs))
    a_parts, u_parts = [], []
    for gi in range(N_LRU_GROUPS):
        cols = slice(gi * LRU_GROUP, (gi + 1) * LRU_GROUP)
        xg = xc[:, cols]
        proj = _dot(xg.astype(BF16), wg_ref[gi])
        r = _sigmoid(proj[:, :LRU_GROUP] + ba[:, cols])
        i = _sigmoid(proj[:, LRU_GROUP:] + bi[:, cols])
        log_a = -LRU_C * r * softplus_neg[:, cols]
        a = jnp.exp(log_a)
        a_parts.append(a)
        u_parts.append(jnp.sqrt(1.0 - a * a) * (i * xg))
    return a_parts, u_parts


def _lru_seq_kernel(gate_ref, xr_ref, cw_ref, cb_ref, wg_ref, ba_ref, bi_ref, lam_ref,
                    y_ref, hlast_ref, ext_ref, a_ref, u_ref, carry_ref):
    tt = xr_ref.shape[0]
    n_grp = tt // SUBLANES

    @pl.when(pl.program_id(1) == 0)
    def _():
        ext_ref[0:SUBLANES, :] = jnp.zeros((SUBLANES, D_RNN), F32)
        carry_ref[...] = jnp.zeros((1, D_RNN), F32)

    xr = xr_ref[...]
    ext_ref[SUBLANES:, :] = xr
    cw = cw_ref[...]
    xc = cb_ref[...] + xr * cw[CONV_A - 1:CONV_A]
    for j in range(CONV_A - 1):
        off = SUBLANES - (CONV_A - 1) + j
        xc = xc + ext_ref[off:off + tt, :] * cw[j:j + 1]
    ext_ref[0:SUBLANES, :] = xr[tt - SUBLANES:, :]

    a_parts, u_parts = _lru_gates(xc, wg_ref, ba_ref[...], bi_ref[...], lam_ref[...])
    for gi in range(N_LRU_GROUPS):
        cols = slice(gi * LRU_GROUP, (gi + 1) * LRU_GROUP)
        a_ref[:, :, cols] = a_parts[gi].reshape(n_grp, SUBLANES, LRU_GROUP)
        u_ref[:, :, cols] = u_parts[gi].reshape(n_grp, SUBLANES, LRU_GROUP)

    sub = lax.broadcasted_iota(jnp.int32, (n_grp, SUBLANES, LANES), 1)
    for ci in range(D_RNN // LANES):
        cols = slice(ci * LANES, (ci + 1) * LANES)
        a3 = a_ref[:, :, cols]
        u3 = u_ref[:, :, cols]
        d = 1
        while d < SUBLANES:
            keep = sub >= d
            a_sh = jnp.where(keep, pltpu.roll(a3, d, 1), 1.0)
            u_sh = jnp.where(keep, pltpu.roll(u3, d, 1), 0.0)
            u3 = a3 * u_sh + u3
            a3 = a3 * a_sh
            d *= 2
        h_prev = jnp.broadcast_to(carry_ref[:, cols], (SUBLANES, LANES))
        for g in range(0, n_grp, 2):
            h0 = a3[g] * h_prev + u3[g]
            h_prev = jnp.broadcast_to(h0[SUBLANES - 1:, :], (SUBLANES, LANES))
            h1 = a3[g + 1] * h_prev + u3[g + 1]
            h_prev = jnp.broadcast_to(h1[SUBLANES - 1:, :], (SUBLANES, LANES))
            rows = slice(g * SUBLANES, (g + 2) * SUBLANES)
            h = jnp.concatenate([h0, h1], axis=0)
            y_ref[rows, cols] = (jax.nn.gelu(gate_ref[rows, cols]) * h).astype(BF16)
        carry_ref[:, cols] = h_prev[0:1, :]
        hlast_ref[:, cols] = h_prev[0:1, :]


def lru_seq(gx, conv_w, conv_b, wg, b_a, b_i, lam, *, batch, seq_len, tt):
    m = gx.shape[0]
    nt = seq_len // tt
    vec = lambda: pl.BlockSpec((1, D_RNN), lambda b, t: (0, 0))
    y, h_last = pl.pallas_call(
        _lru_seq_kernel,
        grid=(batch, nt),
        in_specs=[
            pl.BlockSpec((tt, D_RNN), lambda b, t: (b * nt + t, 0)),
            pl.BlockSpec((tt, D_RNN), lambda b, t: (b * nt + t, 1)),
            pl.BlockSpec((CONV_A, D_RNN), lambda b, t: (0, 0)),
            vec(),
            pl.BlockSpec((N_LRU_GROUPS, LRU_GROUP, 2 * LRU_GROUP), lambda b, t: (0, 0, 0)),
            vec(), vec(), vec(),
        ],
        out_specs=[
            pl.BlockSpec((tt, D_RNN), lambda b, t: (b * nt + t, 0)),
            pl.BlockSpec((None, 1, D_RNN), lambda b, t: (b, 0, 0)),
        ],
        out_shape=[jax.ShapeDtypeStruct((m, D_RNN), BF16), jax.ShapeDtypeStruct((batch, 1, D_RNN), F32)],
        scratch_shapes=[pltpu.VMEM((tt + SUBLANES, D_RNN), F32), pltpu.VMEM((tt // SUBLANES, SUBLANES, D_RNN), F32),
                        pltpu.VMEM((tt // SUBLANES, SUBLANES, D_RNN), F32), pltpu.VMEM((1, D_RNN), F32)],
        compiler_params=_params(2),
        name="lru_seq",
    )(gx, gx, conv_w, conv_b.reshape(1, D_RNN), wg, b_a.reshape(1, D_RNN), b_i.reshape(1, D_RNN),
      lam.reshape(1, D_RNN))
    return y, h_last.reshape(batch, D_RNN)


def _lru_step_kernel(gx_ref, h0_ref, cbuf_ref, cw_ref, cb_ref, wg_ref, ba_ref, bi_ref, lam_ref, y_ref, h_ref):
    gate = gx_ref[:, :D_RNN]
    xr = gx_ref[:, D_RNN:]
    cw = cw_ref[...]
    xc = cb_ref[...] + xr * cw[CONV_A - 1:CONV_A]
    for j in range(CONV_A - 1):
        xc = xc + cbuf_ref[j] * cw[j:j + 1]
    a_parts, u_parts = _lru_gates(xc, wg_ref, ba_ref[...], bi_ref[...], lam_ref[...])
    a = jnp.concatenate(a_parts, axis=1)
    u = jnp.concatenate(u_parts, axis=1)
    h = u + a * h0_ref[...]
    h_ref[...] = h
    y_ref[...] = (jax.nn.gelu(gate) * h).astype(BF16)


def lru_step(gx, h0, cbuf, conv_w, conv_b, wg, b_a, b_i, lam):
    m = gx.shape[0]
    return pl.pallas_call(
        _lru_step_kernel,
        out_shape=[jax.ShapeDtypeStruct((m, D_RNN), BF16), jax.ShapeDtypeStruct((m, D_RNN), F32)],
        compiler_params=pltpu.CompilerParams(vmem_limit_bytes=V7X_VMEM_LIMIT),
        name="lru_step",
    )(gx, h0, cbuf, conv_w, conv_b.reshape(1, D_RNN), wg, b_a.reshape(1, D_RNN), b_i.reshape(1, D_RNN),
      lam.reshape(1, D_RNN))


def _lru_gate_weights(w_a, w_i):
    per = LRU_GROUP // LRU_BLOCK

    def blockdiag(w):
        w = w.reshape(N_LRU_GROUPS, per, LRU_BLOCK, LRU_BLOCK)
        eye = jnp.eye(per, dtype=w.dtype)
        return jnp.einsum('gpcd,pq->gpcqd', w, eye).reshape(N_LRU_GROUPS, LRU_GROUP, LRU_GROUP)

    return jnp.concatenate([blockdiag(w_a), blockdiag(w_i)], axis=2).astype(BF16)


def _rope_tables(pos):
    half = ROT_DIM // 2
    inv = ROPE_THETA ** (-jnp.arange(half, dtype=F32) * (2.0 / ROT_DIM))
    ang = pos[:, None] * inv[None, :]
    cos, sin = jnp.cos(ang), jnp.sin(ang)
    rows = pos.shape[0]
    ones = jnp.ones((rows, HEAD_DIM - ROT_DIM), F32)
    zeros_h = jnp.zeros((rows, half), F32)
    zeros_t = jnp.zeros((rows, HEAD_DIM - ROT_DIM), F32)
    c = jnp.concatenate([cos, cos, ones], axis=1)
    s_lo = jnp.concatenate([-sin, zeros_h, zeros_t], axis=1)
    s_hi = jnp.concatenate([zeros_h, sin, zeros_t], axis=1)
    rep = LANES // HEAD_DIM
    return tuple(jnp.tile(t, (1, rep)) for t in (c, s_lo, s_hi))


def _rope_kernel(qk_ref, c_ref, slo_ref, shi_ref, q_ref, k_ref):
    c, s_lo, s_hi = c_ref[...], slo_ref[...], shi_ref[...]
    half = ROT_DIM // 2
    n_q = N_HEADS * HEAD_DIM // LANES
    for ci in range(QK_COLS // LANES):
        x = qk_ref[:, ci * LANES:(ci + 1) * LANES]
        rot = x * c + pltpu.roll(x, LANES - half, 1) * s_lo + pltpu.roll(x, half, 1) * s_hi
        if ci < n_q:
            q_ref[:, ci * LANES:(ci + 1) * LANES] = rot.astype(BF16)
        else:
            k_ref[:, (ci - n_q) * LANES:(ci - n_q + 1) * LANES] = rot


def rope(qkv, tables, *, tm):
    m = qkv.shape[0]
    nt = tables[0].shape[0] // tm
    tab = lambda: pl.BlockSpec((tm, LANES), lambda i: (i % nt, 0))
    return pl.pallas_call(
        _rope_kernel,
        grid=(m // tm,),
        in_specs=[pl.BlockSpec((tm, QK_COLS), lambda i: (i, 0)), tab(), tab(), tab()],
        out_specs=[pl.BlockSpec((tm, N_HEADS * HEAD_DIM), lambda i: (i, 0)),
                   pl.BlockSpec((tm, KV_COLS), lambda i: (i, 0))],
        out_shape=[jax.ShapeDtypeStruct((m, N_HEADS * HEAD_DIM), BF16), jax.ShapeDtypeStruct((m, KV_COLS), F32)],
        compiler_params=_params(1),
        name="rope",
    )(qkv, *tables)


def _swa_seq_kernel(sink_ref, q_ref, kp_ref, kc_ref, vp_ref, vc_ref, o_ref):
    qi = pl.program_id(1)
    tq = q_ref.shape[0]
    nk = 2 * tq
    kk = jnp.concatenate([kp_ref[...], kc_ref[...]], axis=0) * (HEAD_DIM ** -0.5)
    vv = jnp.concatenate([vp_ref[...], vc_ref[...]], axis=0).astype(BF16)
    key = lax.broadcasted_iota(jnp.int32, (nk, tq), 0)
    qry = lax.broadcasted_iota(jnp.int32, (nk, tq), 1)
    rel = tq + qry - key
    valid = (rel >= 0) & (rel < WINDOW) & ((qi > 0) | (key >= tq))
    low_k = lax.broadcasted_iota(jnp.int32, (nk, LANES), 1) < HEAD_DIM
    low_o = lax.broadcasted_iota(jnp.int32, (tq, LANES), 1) < HEAD_DIM
    for kh in range(N_KV):
        blk = slice((kh // 2) * LANES, (kh // 2 + 1) * LANES)
        in_low = kh % 2 == 0
        k_blk = kk[:, blk]
        k_swap = pltpu.roll(k_blk, HEAD_DIM, 1)
        k_lo = jnp.where(low_k, k_blk if in_low else k_swap, 0.0).astype(BF16)
        k_hi = jnp.where(low_k, 0.0, k_swap if in_low else k_blk).astype(BF16)
        v_blk = vv[:, blk]
        for hp in range(GROUP // 2):
            h0 = kh * GROUP + 2 * hp
            hcols = slice(h0 * HEAD_DIM, (h0 + 2) * HEAD_DIM)
            q_pair = q_ref[:, hcols]
            outs = []
            for which, k_pad in enumerate((k_lo, k_hi)):
                s = jnp.where(valid, _dot_nt(k_pad, q_pair), -jnp.inf)
                sink = sink_ref[h0 + which]
                mx = jnp.maximum(jnp.max(s, axis=0, keepdims=True), sink)
                e = jnp.exp(s - mx)
                denom = jnp.sum(e, axis=0, keepdims=True) + jnp.exp(sink - mx)
                p = (e * (1.0 / denom)).astype(BF16)
                outs.append(_dot_tn(p, v_blk))
            if in_low:
                o_pair = jnp.where(low_o, outs[0], pltpu.roll(outs[1], HEAD_DIM, 1))
            else:
                o_pair = jnp.where(low_o, pltpu.roll(outs[0], HEAD_DIM, 1), outs[1])
            o_ref[:, hcols] = o_pair.astype(BF16)


def swa_seq(q_rot, k_rot, qkv, sinks, *, batch, seq_len):
    m = q_rot.shape[0]
    tq = WINDOW
    nq = seq_len // tq
    v_col = QK_COLS // KV_COLS
    prev = lambda col: pl.BlockSpec((tq, KV_COLS), lambda b, i: (b * nq + jnp.maximum(i - 1, 0), col))
    cur = lambda col: pl.BlockSpec((tq, KV_COLS), lambda b, i: (b * nq + i, col))
    return pl.pallas_call(
        _swa_seq_kernel,
        grid=(batch, nq),
        in_specs=[
            pl.BlockSpec(memory_space=pltpu.SMEM),
            pl.BlockSpec((tq, N_HEADS * HEAD_DIM), lambda b, i: (b * nq + i, 0)),
            prev(0), cur(0), prev(v_col), cur(v_col),
        ],
        out_specs=pl.BlockSpec((tq, N_HEADS * HEAD_DIM), lambda b, i: (b * nq + i, 0)),
        out_shape=jax.ShapeDtypeStruct((m, N_HEADS * HEAD_DIM), BF16),
        compiler_params=_params(2),
        name="swa_seq",
    )(sinks, q_rot, k_rot, k_rot, qkv, qkv)


def _swa_step_kernel(q_ref, kn_ref, vn_ref, ck_ref, cv_ref, sink_ref, o_ref):
    ck = ck_ref[...].astype(BF16)
    cv = cv_ref[...].astype(BF16)
    col = lax.broadcasted_iota(jnp.int32, (GROUP, WINDOW), 1)
    for kh in range(N_KV):
        kcols = slice(kh * HEAD_DIM, (kh + 1) * HEAD_DIM)
        heads = slice(kh * GROUP, (kh + 1) * GROUP)
        q = q_ref[heads, :]
        k_new = kn_ref[kh:kh + 1, :].astype(BF16).astype(F32)
        v_new = vn_ref[kh:kh + 1, :].astype(BF16).astype(F32)
        scale = HEAD_DIM ** -0.5
        s_c = _dot_nt(q, ck[:, kcols]) * scale
        s_c = jnp.where(col >= 1, s_c, -jnp.inf)
        s_n = jnp.sum(q.astype(F32) * k_new, axis=-1, keepdims=True) * scale
        sink = sink_ref[heads, :]
        mx = jnp.maximum(jnp.maximum(jnp.max(s_c, axis=-1, keepdims=True), s_n), sink)
        p_c = jnp.exp(s_c - mx)
        p_n = jnp.exp(s_n - mx)
        denom = jnp.sum(p_c, axis=-1, keepdims=True) + p_n + jnp.exp(sink - mx)
        o = _dot(p_c.astype(BF16), cv[:, kcols]) + p_n.astype(BF16).astype(F32) * v_new
        o_ref[heads, :] = (o / denom).astype(BF16)


def swa_step(q3, k_new, v_new, cache_k, cache_v, sinks):
    b = q3.shape[0]
    return pl.pallas_call(
        _swa_step_kernel,
        grid=(b,),
        in_specs=[
            pl.BlockSpec((None, N_HEADS, HEAD_DIM), lambda i: (i, 0, 0)),
            pl.BlockSpec((None, N_KV, HEAD_DIM), lambda i: (i, 0, 0)),
            pl.BlockSpec((None, N_KV, HEAD_DIM), lambda i: (i, 0, 0)),
            pl.BlockSpec((None, WINDOW, KV_COLS), lambda i: (i, 0, 0)),
            pl.BlockSpec((None, WINDOW, KV_COLS), lambda i: (i, 0, 0)),
            pl.BlockSpec((N_HEADS, 1), lambda i: (0, 0)),
        ],
        out_specs=pl.BlockSpec((None, N_HEADS, HEAD_DIM), lambda i: (i, 0, 0)),
        out_shape=jax.ShapeDtypeStruct((b, N_HEADS, HEAD_DIM), BF16),
        compiler_params=_params(1),
        name="swa_step",
    )(q3, k_new, v_new, cache_k, cache_v, sinks.reshape(N_HEADS, 1))


def _hgrn_lower_bound(logits, layer):
    mx = jnp.max(logits, axis=0, keepdims=True)
    e = jnp.exp(logits - mx)
    sm = e / jnp.sum(e, axis=0, keepdims=True)
    lb = jnp.zeros_like(sm[0:1])
    for i in range(1, layer + 1):
        lb = lb + sm[i:i + 1]
    return lb


def _hgrn_consts():
    c = HG_CHUNK
    t = np.arange(c)[:, None]
    s = np.arange(c)[None, :]
    tri = (s <= t).astype(np.float32)
    sel = [tri]
    msk = [(s == t)]
    for lvl in range(1, HG_LEVELS + 1):
        w = 1 << (lvl - 1)
        ref_row = (t // (2 * w)) * (2 * w) + w - 1
        upper_t = (t % (2 * w)) >= w
        lower_s = (s % (2 * w)) < w
        sign = np.where(upper_t, 1.0, -1.0)
        sel.append(sign * (tri - (s <= ref_row)))
        msk.append(((t // (2 * w)) == (s // (2 * w))) & upper_t & lower_s)
    sel.append(1.0 - tri)
    sel = np.concatenate(sel, axis=0).astype(np.float32)
    sel2 = np.concatenate([sel, sel], axis=1)
    msk = np.stack(msk, axis=0).astype(np.float32)
    return jnp.asarray(sel2, BF16), jnp.asarray(msk, F32)


def _hgrn_seq_kernel(q_ref, z_ref, v_ref, g_ref, lbl_ref, gn_ref, sel_ref, msk_ref, y_ref, s_out_ref, st_ref,
                     *, layer, heads_per_step):
    ti = pl.program_id(2)
    tt = q_ref.shape[0]
    c = HG_CHUNK

    @pl.when(ti == 0)
    def _():
        st_ref[...] = jnp.zeros(st_ref.shape, F32)

    lb_all = _hgrn_lower_bound(lbl_ref[...], layer)
    gn = gn_ref[...]
    row = lax.broadcasted_iota(jnp.int32, (c, heads_per_step * HG_DK), 0)
    upper = [None] + [((row >> (lvl - 1)) & 1) == 1 for lvl in range(1, HG_LEVELS + 1)]
    states = [st_ref[hh] for hh in range(heads_per_step)]

    head_cols = [slice(hh * HG_DK, (hh + 1) * HG_DK) for hh in range(heads_per_step)]

    def decays(ci):
        rows = slice(ci * c, (ci + 1) * c)
        q = q_ref[rows, :]
        sq = q * _sigmoid(q)
        sz = _sigmoid(z_ref[rows, :])
        log_f = jnp.log(lb_all + (1.0 - lb_all) * sz)
        k = (1.0 - lb_all) * (1.0 - sz)
        hi = log_f.astype(BF16)
        lo = (log_f - hi.astype(F32)).astype(BF16)
        ex = _dot(sel_ref[...], jnp.concatenate([hi, lo], axis=0))
        e_b = jnp.exp(ex[0:c])
        xs = [(jnp.where(upper[lvl], sq, k) * jnp.exp(ex[lvl * c:(lvl + 1) * c])).astype(BF16)
              for lvl in range(1, HG_LEVELS + 1)]
        return dict(rows=rows, v=v_ref[rows, :].astype(BF16), sq=sq.astype(BF16), k=k.astype(BF16), xs=xs,
                    q_dec=(sq * e_b).astype(BF16), k_dec=(k * jnp.exp(ex[(HG_LEVELS + 1) * c:])).astype(BF16),
                    e_last=e_b[c - 1:c, :])

    def scores(d):
        atts = []
        for cols in head_cols:
            att = _dot_nt(d["sq"][:, cols], d["k"][:, cols]) * msk_ref[0]
            for lvl in range(1, HG_LEVELS + 1):
                x = d["xs"][lvl - 1][:, cols]
                att = att + _dot_nt(x, x) * msk_ref[lvl]
            atts.append(att.astype(BF16))
        return atts

    def outputs(d, atts):
        outs = []
        for hh, cols in enumerate(head_cols):
            st = states[hh]
            o = _dot(atts[hh], d["v"][:, cols]) + _dot_nt(d["q_dec"][:, cols], st.astype(BF16))
            states[hh] = st * d["e_last"][:, cols] + _dot_tn(d["v"][:, cols], d["k_dec"][:, cols])
            outs.append(_rms_rows(o, gn))
        g = g_ref[d["rows"], :]
        y_ref[d["rows"], :] = (jnp.concatenate(outs, axis=1) * (g * _sigmoid(g))).astype(BF16)

    n_chunks = tt // c
    stage_a, stage_b = {}, {}
    for step in range(n_chunks + 2):
        if step < n_chunks:
            stage_a[step] = decays(step)
        if 0 <= step - 1 < n_chunks:
            stage_b[step - 1] = scores(stage_a[step - 1])
        if 0 <= step - 2 < n_chunks:
            outputs(stage_a.pop(step - 2), stage_b.pop(step - 2))

    for hh in range(heads_per_step):
        st_ref[hh] = states[hh]

    @pl.when(ti == pl.num_programs(2) - 1)
    def _():
        for hh in range(heads_per_step):
            s_out_ref[hh] = states[hh].T


def hgrn_seq(qzvg, lb_logits, g_norm, *, layer, batch, seq_len, tt, heads_per_step):
    m = qzvg.shape[0]
    nt = seq_len // tt
    hw = heads_per_step * HG_DK
    nh = HG_HEADS // heads_per_step
    sel, msk = _hgrn_consts()
    part = lambda p: pl.BlockSpec((tt, hw), lambda b, h, t: (b * nt + t, p * nh + h))
    y, s_out = pl.pallas_call(
        functools.partial(_hgrn_seq_kernel, layer=layer, heads_per_step=heads_per_step),
        grid=(batch, nh, nt),
        in_specs=[
            part(0), part(1), part(2), part(3),
            pl.BlockSpec((DEPTH, hw), lambda b, h, t: (0, h)),
            pl.BlockSpec((1, HG_DV), lambda b, h, t: (0, 0)),
            pl.BlockSpec(sel.shape, lambda b, h, t: (0, 0)),
            pl.BlockSpec(msk.shape, lambda b, h, t: (0, 0, 0)),
        ],
        out_specs=[
            pl.BlockSpec((tt, hw), lambda b, h, t: (b * nt + t, h)),
            pl.BlockSpec((None, heads_per_step, HG_DK, HG_DV), lambda b, h, t: (b, h, 0, 0)),
        ],
        out_shape=[jax.ShapeDtypeStruct((m, HG_HEADS * HG_DV), BF16),
                   jax.ShapeDtypeStruct((batch, HG_HEADS, HG_DK, HG_DV), F32)],
        scratch_shapes=[pltpu.VMEM((heads_per_step, HG_DV, HG_DK), F32)],
        compiler_params=_params(3),
        name="hgrn_seq",
    )(qzvg, qzvg, qzvg, qzvg, lb_logits, g_norm.reshape(1, HG_DV), sel, msk)
    return y, s_out


def _hgrn_step_kernel(x_ref, s_ref, lbl_ref, gn_ref, y_ref, s_out_ref, *, layer):
    nh = HG_HEADS
    q = x_ref[0:nh, :]
    z = x_ref[nh:2 * nh, :]
    v = x_ref[2 * nh:3 * nh, :]
    g = x_ref[3 * nh:4 * nh, :]
    lb = _hgrn_lower_bound(lbl_ref[...], layer)[0]
    sq = q * _sigmoid(q)
    sz = _sigmoid(z)
    f = lb + (1.0 - lb) * sz
    k = (1.0 - lb) * (1.0 - sz)
    pad = jnp.zeros((LANES - 3 * nh, HG_DK), F32)
    cols = jnp.concatenate([f, k, sq, pad], axis=0).T
    outs = []
    for h in range(nh):
        f_col = cols[:, h:h + 1]
        k_col = cols[:, nh + h:nh + h + 1]
        q_col = cols[:, 2 * nh + h:2 * nh + h + 1]
        s_new = s_ref[h] * f_col + k_col * v[h:h + 1, :]
        s_out_ref[h] = s_new
        outs.append(jnp.sum(s_new * q_col, axis=0, keepdims=True))
    o = jnp.concatenate(outs, axis=0)
    y_ref[...] = (_rms_rows(o, gn_ref[...]) * (g * _sigmoid(g))).astype(BF16)


def hgrn_step(x4, state, lb_logits, g_norm, *, layer):
    b = x4.shape[0]
    return pl.pallas_call(
        functools.partial(_hgrn_step_kernel, layer=layer),
        grid=(b,),
        in_specs=[
            pl.BlockSpec((None, 4 * HG_HEADS, HG_DK), lambda i: (i, 0, 0)),
            pl.BlockSpec((None, HG_HEADS, HG_DK, HG_DV), lambda i: (i, 0, 0, 0)),
            pl.BlockSpec((DEPTH, HG_HEADS, HG_DK), lambda i: (0, 0, 0)),
            pl.BlockSpec((1, HG_DV), lambda i: (0, 0)),
        ],
        out_specs=[
            pl.BlockSpec((None, HG_HEADS, HG_DV), lambda i: (i, 0, 0)),
            pl.BlockSpec((None, HG_HEADS, HG_DK, HG_DV), lambda i: (i, 0, 0, 0)),
        ],
        out_shape=[jax.ShapeDtypeStruct((b, HG_HEADS, HG_DV), BF16),
                   jax.ShapeDtypeStruct((b, HG_HEADS, HG_DK, HG_DV), F32)],
        compiler_params=_params(1),
        name="hgrn_step",
    )(x4, state, lb_logits.reshape(DEPTH, HG_HEADS, HG_DK), g_norm.reshape(1, HG_DV))


def _trunk(x3, states, w, *, step_mode):
    batch, seq_len, _ = x3.shape
    m = batch * seq_len
    x = x3.reshape(m, D_MODEL)
    tm = m if step_mode else seq_len
    tn_in = 1280 if step_mode else 512
    tn_out = 1024 if step_mode else 256
    tf = 512 if step_mode else 256
    zeros = lambda n: jnp.zeros((n,), F32)
    lru_wg = [_lru_gate_weights(w["lru_w_a"][j], w["lru_w_i"][j]) for j in range(w["lru_w_a"].shape[0])]
    if step_mode:
        pos = jnp.full((m,), PAST_LEN, F32)
    else:
        pos = jnp.arange(seq_len, dtype=F32)
    rope_tables = _rope_tables(pos)

    def out_proj(a, w_stack, li, b, res):
        k_blocks = 1
        while tm * (a.shape[1] // k_blocks) * a.dtype.itemsize >= SINGLE_BUFFER_BYTES:
            k_blocks *= 2
        k = a.shape[1] // k_blocks
        cap = max(LANES, min(tn_out, W_TILE_BYTES // (4 * k) // LANES * LANES))
        for ki in range(k_blocks):
            bias = b if ki == k_blocks - 1 else jnp.zeros_like(b)
            res = matmul_residual(a, w_stack, li, bias, res, tm=tm, tn=_largest_tile(D_MODEL, cap),
                                  k_blocks=k_blocks, k_index=ki)
        return res

    new = {"lru_h": [], "lru_conv": [], "swa_k": [], "swa_v": [], "hgrn": [], "ffn_conv": []}
    for layer in range(DEPTH):
        kind, j = LAYER_MIXER[layer], LAYER_SLOT[layer]
        g_mix = w["norm_mix"][layer]
        if kind == 0:
            n = 2 * D_RNN
            gx = norm_matmul(x, g_mix, w["lru_w_in"], j, zeros(n), tm=tm, tn=_largest_tile(n, tn_in))
            if step_mode:
                cbuf = states["lru_conv"][j]
                y, h_new = lru_step(gx, states["lru_h"][j], jnp.swapaxes(cbuf, 0, 1), w["lru_conv_w"][j],
                                    w["lru_conv_b"][j], lru_wg[j], w["lru_b_a"][j], w["lru_b_i"][j],
                                    w["lru_lambda"][j])
                c_new = jnp.concatenate([cbuf[:, 1:], gx[:, None, D_RNN:]], axis=1)
            else:
                y, h_new = lru_seq(gx, w["lru_conv_w"][j], w["lru_conv_b"][j], lru_wg[j], w["lru_b_a"][j],
                                   w["lru_b_i"][j], w["lru_lambda"][j], batch=batch, seq_len=seq_len, tt=256)
                c_new = gx.reshape(batch, seq_len, n)[:, seq_len - (CONV_A - 1):, D_RNN:]
            new["lru_h"].append(h_new)
            new["lru_conv"].append(c_new)
            x = out_proj(y, w["lru_w_out"], j, zeros(D_MODEL), x)
        elif kind == 1:
            n = QK_COLS + KV_COLS
            qkv = norm_matmul(x, g_mix, w["swa_w_qkv"], j, w["swa_b_qkv"][j], tm=tm, tn=_largest_tile(n, tn_in))
            q_rot, k_rot = rope(qkv, rope_tables, tm=min(tm, 1024))
            v = qkv[:, QK_COLS:]
            if step_mode:
                ck = states["swa_k"][j]
                cv = states["swa_v"][j]
                k_new = k_rot.reshape(m, N_KV, HEAD_DIM)
                v_new = v.reshape(m, N_KV, HEAD_DIM)
                o = swa_step(q_rot.reshape(m, N_HEADS, HEAD_DIM), k_new, v_new,
                             ck.reshape(m, WINDOW, KV_COLS), cv.reshape(m, WINDOW, KV_COLS), w["swa_sinks"][j])
                o = o.reshape(m, N_HEADS * HEAD_DIM)
                k_cache = jnp.concatenate([ck[:, 1:], k_new[:, None]], axis=1)
                v_cache = jnp.concatenate([cv[:, 1:], v_new[:, None]], axis=1)
            else:
                o = swa_seq(q_rot, k_rot, qkv, w["swa_sinks"][j], batch=batch, seq_len=seq_len)
                k_cache = k_rot.reshape(batch, seq_len, N_KV, HEAD_DIM)[:, seq_len - WINDOW:]
                v_cache = v.reshape(batch, seq_len, N_KV, HEAD_DIM)[:, seq_len - WINDOW:]
            new["swa_k"].append(k_cache)
            new["swa_v"].append(v_cache)
            x = out_proj(o, w["swa_w_o"], j, w["swa_b_o"][j], x)
        else:
            n = 2 * HG_HEADS * HG_DK + 2 * HG_HEADS * HG_DV
            qzvg = norm_matmul(x, g_mix, w["hg_w_in"], j, zeros(n), tm=tm, tn=_largest_tile(n, tn_in))
            if step_mode:
                y, s_new = hgrn_step(qzvg.reshape(m, 4 * HG_HEADS, HG_DK), states["hgrn"][j], w["hg_lb_logits"],
                                     w["hg_norm"][j], layer=layer)
                y = y.reshape(m, HG_HEADS * HG_DV)
            else:
                y, s_new = hgrn_seq(qzvg, w["hg_lb_logits"], w["hg_norm"][j], layer=layer, batch=batch,
                                    seq_len=seq_len, tt=512, heads_per_step=4)
            new["hgrn"].append(s_new)
            x = out_proj(y, w["hg_w_o"], j, zeros(D_MODEL), x)

        if step_mode:
            fbuf = states["ffn_conv"][layer]
            act, gate = ffn_in(x, w["norm_ffn"][layer], w["ffn_w_in"], w["ffn_conv_w"], w["ffn_conv_b"], layer,
                               jnp.swapaxes(fbuf, 0, 1), tm=tm, tf=tf)
            f_new = jnp.concatenate([fbuf[:, 1:], gate.reshape(m, 1, D_FF)], axis=1)
        else:
            act, tail = ffn_in(x, w["norm_ffn"][layer], w["ffn_w_in"], w["ffn_conv_w"], w["ffn_conv_b"], layer,
                               None, tm=tm, tf=tf)
            f_new = tail[:, SUBLANES - (CONV_F - 1):]
        new["ffn_conv"].append(f_new)
        x = out_proj(act, w["ffn_w_out"], layer, zeros(D_MODEL), x)

    y = final_norm(x, w["norm_final"], tm=min(tm, 1024)).reshape(batch, seq_len, D_MODEL)
    return (y, jnp.stack(new["lru_h"]), jnp.stack(new["lru_conv"]), jnp.stack(new["swa_k"]),
            jnp.stack(new["swa_v"]), jnp.stack(new["hgrn"]), jnp.stack(new["ffn_conv"]))


def kernel(x_prompt, x_sample, state_lru_h, state_lru_conv, cache_swa_k, cache_swa_v, state_hgrn, state_ffn_conv,
           norm_mix, norm_ffn, norm_final,
           lru_w_in, lru_conv_w, lru_conv_b, lru_w_a, lru_b_a, lru_w_i, lru_b_i, lru_lambda, lru_w_out,
           swa_w_qkv, swa_b_qkv, swa_sinks, swa_w_o, swa_b_o,
           hg_w_in, hg_lb_logits, hg_norm, hg_w_o,
           ffn_w_in, ffn_conv_w, ffn_conv_b, ffn_w_out):
    w = dict(norm_mix=norm_mix, norm_ffn=norm_ffn, norm_final=norm_final,
             lru_w_in=lru_w_in, lru_conv_w=lru_conv_w, lru_conv_b=lru_conv_b, lru_w_a=lru_w_a, lru_b_a=lru_b_a,
             lru_w_i=lru_w_i, lru_b_i=lru_b_i, lru_lambda=lru_lambda, lru_w_out=lru_w_out,
             swa_w_qkv=swa_w_qkv, swa_b_qkv=swa_b_qkv, swa_sinks=swa_sinks, swa_w_o=swa_w_o, swa_b_o=swa_b_o,
             hg_w_in=hg_w_in, hg_lb_logits=hg_lb_logits, hg_norm=hg_norm, hg_w_o=hg_w_o,
             ffn_w_in=ffn_w_in, ffn_conv_w=ffn_conv_w, ffn_conv_b=ffn_conv_b, ffn_w_out=ffn_w_out)
    states = dict(lru_h=state_lru_h, lru_conv=state_lru_conv, swa_k=cache_swa_k, swa_v=cache_swa_v,
                  hgrn=state_hgrn, ffn_conv=state_ffn_conv)
    prompt = _trunk(x_prompt, None, w, step_mode=False)
    sample = _trunk(x_sample, states, w, step_mode=True)
    return (prompt[0], sample[0]) + prompt[1:] + sample[1:]
```

```python
import functools

import numpy as np
import jax
import jax.numpy as jnp
from jax import lax
from jax.experimental import pallas as pl
from jax.experimental.pallas import tpu as pltpu

F32 = jnp.float32
BF16 = jnp.bfloat16

D_MODEL = 2048
DEPTH = 4
PAST_LEN = 16384
EPS = 1e-6
LAYER_MIXER = tuple(i % 3 for i in range(DEPTH))
LAYER_SLOT = tuple(LAYER_MIXER[:i].count(LAYER_MIXER[i]) for i in range(DEPTH))

D_RNN = 2560
LRU_BLOCKS = 16
LRU_BLOCK = D_RNN // LRU_BLOCKS
LRU_GROUP = 640
N_LRU_GROUPS = D_RNN // LRU_GROUP
CONV_A = 4
LRU_C = 8.0

N_HEADS = 32
N_KV = 4
HEAD_DIM = 64
GROUP = N_HEADS // N_KV
WINDOW = 128
ROT_DIM = HEAD_DIM // 4
ROPE_THETA = 500000.0
QK_COLS = (N_HEADS + N_KV) * HEAD_DIM
KV_COLS = N_KV * HEAD_DIM

HG_HEADS = 16
HG_DK = 128
HG_DV = 128
HG_CHUNK = 64
HG_LEVELS = 6

D_FF = 3 * D_MODEL
CONV_F = 3

LANES = 128
SUBLANES = 8
V7X_VMEM_LIMIT = 56 * 1024 * 1024
SINGLE_BUFFER_BYTES = 14 * 1024 * 1024
W_TILE_BYTES = 8 * 1024 * 1024
DOT_ROWS = 512


def _params(n_axes, vmem=V7X_VMEM_LIMIT):
    return pltpu.CompilerParams(dimension_semantics=("arbitrary",) * n_axes, vmem_limit_bytes=vmem)


def _sigmoid(x):
    return 1.0 / (1.0 + jnp.exp(-x))


def _rms_rows(x, g):
    var = jnp.mean(x * x, axis=-1, keepdims=True)
    return x * lax.rsqrt(var + EPS) * g


def _dot(a, b):
    return jnp.dot(a, b, preferred_element_type=F32)


def _dot_nt(a, b):
    return lax.dot_general(a, b, (((1,), (1,)), ((), ())), preferred_element_type=F32)


def _dot_tn(a, b):
    return lax.dot_general(a, b, (((0,), (0,)), ((), ())), preferred_element_type=F32)


def _largest_tile(n, cap):
    t = cap
    while n % t:
        t -= LANES
    return t


def _row_tile_spec(shape, index_map, dtype):
    if shape[0] * shape[1] * jnp.dtype(dtype).itemsize >= SINGLE_BUFFER_BYTES:
        return pl.BlockSpec(shape, index_map, pipeline_mode=pl.Buffered(1))
    return pl.BlockSpec(shape, index_map)


def _row_blocks(n_rows):
    step = min(n_rows, DOT_ROWS)
    return [slice(r, r + step) for r in range(0, n_rows, step)]


def _norm_matmul_kernel(x_ref, xs_ref, g_ref, w_ref, b_ref, o_ref, os_ref, hn_ref):
    tm, ns = x_ref.shape[0], xs_ref.shape[0]

    @pl.when(pl.program_id(1) == 0)
    def _():
        for rows in _row_blocks(tm):
            hn_ref[rows, :] = _rms_rows(x_ref[rows, :], g_ref[...]).astype(BF16)
        hn_ref[tm:, :] = _rms_rows(xs_ref[...], g_ref[...]).astype(BF16)

    w = w_ref[...].astype(BF16)
    blocks = _row_blocks(tm)
    for rows in blocks[:-1]:
        o_ref[rows, :] = _dot(hn_ref[rows, :], w) + b_ref[...]
    last = blocks[-1]
    res = _dot(hn_ref[last.start:, :], w) + b_ref[...]
    o_ref[last, :] = res[:tm - last.start]
    os_ref[...] = res[tm - last.start:]


def norm_matmul(x, xs, g, w_stack, li, b, *, tm, tn):
    m, k = x.shape
    ns = xs.shape[1]
    n = w_stack.shape[2]
    return pl.pallas_call(
        _norm_matmul_kernel,
        grid=(m // tm, n // tn),
        in_specs=[
            _row_tile_spec((tm, k), lambda i, j: (i, 0), F32),
            pl.BlockSpec((None, ns, k), lambda i, j: (0, 0, 0)),
            pl.BlockSpec((1, k), lambda i, j: (0, 0)),
            pl.BlockSpec((None, k, tn), lambda i, j: (li, 0, j)),
            pl.BlockSpec((1, tn), lambda i, j: (0, j)),
        ],
        out_specs=[pl.BlockSpec((tm, tn), lambda i, j: (i, j)),
                   pl.BlockSpec((None, ns, tn), lambda i, j: (i, 0, j))],
        out_shape=[jax.ShapeDtypeStruct((m, n), F32), jax.ShapeDtypeStruct((m // tm, ns, n), F32)],
        scratch_shapes=[pltpu.VMEM((tm + ns, k), BF16)],
        compiler_params=_params(2),
        name="norm_matmul",
    )(x, xs, g.reshape(1, k), w_stack, b.reshape(1, n))


def _matmul_residual_kernel(a_ref, as_ref, w_ref, b_ref, x_ref, xs_ref, o_ref, os_ref):
    tm = a_ref.shape[0]
    w = w_ref[...].astype(BF16)
    blocks = _row_blocks(tm)
    for rows in blocks[:-1]:
        o_ref[rows, :] = x_ref[rows, :] + (_dot(a_ref[rows, :], w) + b_ref[...])
    last = blocks[-1]
    lhs = jnp.concatenate([a_ref[last, :], as_ref[...]], axis=0)
    res = _dot(lhs, w) + b_ref[...]
    o_ref[last, :] = x_ref[last, :] + res[:tm - last.start]
    os_ref[...] = xs_ref[...] + res[tm - last.start:]


def matmul_residual(a, a_s, w_stack, li, b, x, xs, *, tm, tn, k_blocks=1, k_index=0):
    m = a.shape[0]
    k = a.shape[1] // k_blocks
    ns = xs.shape[1]
    n = w_stack.shape[2]
    return pl.pallas_call(
        _matmul_residual_kernel,
        grid=(m // tm, n // tn),
        in_specs=[
            _row_tile_spec((tm, k), lambda i, j: (i, k_index), a.dtype),
            pl.BlockSpec((None, ns, k), lambda i, j: (0, 0, k_index)),
            pl.BlockSpec((None, k, tn), lambda i, j: (li, k_index, j)),
            pl.BlockSpec((1, tn), lambda i, j: (0, j)),
            pl.BlockSpec((tm, tn), lambda i, j: (i, j)),
            pl.BlockSpec((None, ns, tn), lambda i, j: (0, 0, j)),
        ],
        out_specs=[pl.BlockSpec((tm, tn), lambda i, j: (i, j)),
                   pl.BlockSpec((None, ns, tn), lambda i, j: (i, 0, j))],
        out_shape=[jax.ShapeDtypeStruct((m, n), F32), jax.ShapeDtypeStruct((m // tm, ns, n), F32)],
        compiler_params=_params(2),
        name="matmul_residual",
    )(a, a_s, w_stack, b.reshape(1, n), x, xs)


def _norm_kernel(x_ref, g_ref, o_ref):
    o_ref[...] = _rms_rows(x_ref[...], g_ref[...])


def final_norm(x, g, *, tm):
    m, k = x.shape
    return pl.pallas_call(
        _norm_kernel,
        grid=(m // tm,),
        in_specs=[pl.BlockSpec((tm, k), lambda i: (i, 0)), pl.BlockSpec((1, k), lambda i: (0, 0))],
        out_specs=pl.BlockSpec((tm, k), lambda i: (i, 0)),
        out_shape=jax.ShapeDtypeStruct((m, k), F32),
        compiler_params=_params(1),
        name="final_norm",
    )(x, g.reshape(1, k))


def _ffn_in_kernel(x_ref, xs_ref, g_ref, wg_ref, wu_ref, cw_ref, cb_ref, prev_ref,
                   act_ref, acts_ref, gt_ref, gts_ref, hn_ref, ext_ref):
    tm = x_ref.shape[0]

    @pl.when(pl.program_id(1) == 0)
    def _():
        for rows in _row_blocks(tm):
            hn_ref[rows, :] = _rms_rows(x_ref[rows, :], g_ref[...]).astype(BF16)
        hn_ref[tm:, :] = _rms_rows(xs_ref[...], g_ref[...]).astype(BF16)
        ext_ref[0:SUBLANES, :] = jnp.zeros((SUBLANES, ext_ref.shape[1]), F32)

    wg = wg_ref[...].astype(BF16)
    wu = wu_ref[...].astype(BF16)
    cw = cw_ref[...]

    def activation(gate, up, prev1, prev2):
        conv = cb_ref[...] + gate * cw[2:3] + prev2 * cw[0:1] + prev1 * cw[1:2]
        return (jax.nn.gelu(conv) * up).astype(BF16)

    blocks = _row_blocks(tm)
    for rows in blocks:
        r0, n = rows.start, rows.stop - rows.start
        is_last = rows.stop == tm
        hn = hn_ref[r0:, :] if is_last else hn_ref[rows, :]
        gate = _dot(hn, wg)
        up = _dot(hn, wu)
        ext_ref[SUBLANES + r0:SUBLANES + r0 + n, :] = gate[:n]
        prev1 = ext_ref[SUBLANES - 1 + r0:SUBLANES - 1 + r0 + n, :]
        prev2 = ext_ref[SUBLANES - 2 + r0:SUBLANES - 2 + r0 + n, :]
        act_ref[rows, :] = activation(gate[:n], up[:n], prev1, prev2)
        if is_last:
            gt_ref[...] = gate[n - SUBLANES:n, :]
            gts_ref[...] = gate[n:, :]
            acts_ref[...] = activation(gate[n:], up[n:], prev_ref[1], prev_ref[0])


def ffn_in(x, xs, g, w_in, conv_w, conv_b, li, prev, *, tm, tf):
    m, k = x.shape
    ns = xs.shape[1]
    nf = D_FF // tf
    n_i = m // tm
    act, act_s, tail, gate_s = pl.pallas_call(
        _ffn_in_kernel,
        grid=(n_i, nf),
        in_specs=[
            _row_tile_spec((tm, k), lambda i, f: (i, 0), F32),
            pl.BlockSpec((None, ns, k), lambda i, f: (0, 0, 0)),
            pl.BlockSpec((1, k), lambda i, f: (0, 0)),
            pl.BlockSpec((None, k, tf), lambda i, f: (li, 0, f)),
            pl.BlockSpec((None, k, tf), lambda i, f: (li, 0, nf + f)),
            pl.BlockSpec((None, CONV_F, tf), lambda i, f: (li, 0, f)),
            pl.BlockSpec((None, 1, tf), lambda i, f: (li, 0, f)),
            pl.BlockSpec((2, ns, tf), lambda i, f: (0, 0, f)),
        ],
        out_specs=[
            pl.BlockSpec((tm, tf), lambda i, f: (i, f)),
            pl.BlockSpec((None, ns, tf), lambda i, f: (i, 0, f)),
            pl.BlockSpec((None, SUBLANES, tf), lambda i, f: (i, 0, f)),
            pl.BlockSpec((None, ns, tf), lambda i, f: (i, 0, f)),
        ],
        out_shape=[jax.ShapeDtypeStruct((m, D_FF), BF16), jax.ShapeDtypeStruct((n_i, ns, D_FF), BF16),
                   jax.ShapeDtypeStruct((n_i, SUBLANES, D_FF), F32), jax.ShapeDtypeStruct((n_i, ns, D_FF), F32)],
        scratch_shapes=[pltpu.VMEM((tm + ns, k), BF16), pltpu.VMEM((tm + SUBLANES, tf), F32)],
        compiler_params=_params(2),
        name="ffn_in",
    )(x, xs, g.reshape(1, k), w_in, w_in, conv_w, conv_b.reshape(DEPTH, 1, D_FF), prev)
    return act, act_s, tail, gate_s


def _lru_gates(xc, wg_ref, ba, bi, lam):
    lam_abs = jnp.abs(lam)
    softplus_neg = jnp.maximum(-lam, 0.0) + jnp.log(1.0 + jnp.exp(-lam_abs))
    a_parts, u_parts = [], []
    for gi in range(N_LRU_GROUPS):
        cols = slice(gi * LRU_GROUP, (gi + 1) * LRU_GROUP)
        xg = xc[:, cols]
        proj = _dot(xg.astype(BF16), wg_ref[gi])
        r = _sigmoid(proj[:, :LRU_GROUP] + ba[:, cols])
        i = _sigmoid(proj[:, LRU_GROUP:] + bi[:, cols])
        log_a = -LRU_C * r * softplus_neg[:, cols]
        a = jnp.exp(log_a)
        a_parts.append(a)
        u_parts.append(jnp.sqrt(1.0 - a * a) * (i * xg))
    return a_parts, u_parts


def _lru_seq_kernel(gate_ref, xr_ref, cw_ref, cb_ref, wg_ref, ba_ref, bi_ref, lam_ref,
                    y_ref, hlast_ref, ext_ref, a_ref, u_ref, carry_ref):
    tt = xr_ref.shape[0]
    n_grp = tt // SUBLANES

    @pl.when(pl.program_id(1) == 0)
    def _():
        ext_ref[0:SUBLANES, :] = jnp.zeros((SUBLANES, D_RNN), F32)
        carry_ref[...] = jnp.zeros((1, D_RNN), F32)

    xr = xr_ref[...]
    ext_ref[SUBLANES:, :] = xr
    cw = cw_ref[...]
    xc = cb_ref[...] + xr * cw[CONV_A - 1:CONV_A]
    for j in range(CONV_A - 1):
        off = SUBLANES - (CONV_A - 1) + j
        xc = xc + ext_ref[off:off + tt, :] * cw[j:j + 1]
    ext_ref[0:SUBLANES, :] = xr[tt - SUBLANES:, :]

    a_parts, u_parts = _lru_gates(xc, wg_ref, ba_ref[...], bi_ref[...], lam_ref[...])
    for gi in range(N_LRU_GROUPS):
        cols = slice(gi * LRU_GROUP, (gi + 1) * LRU_GROUP)
        a_ref[:, :, cols] = a_parts[gi].reshape(n_grp, SUBLANES, LRU_GROUP)
        u_ref[:, :, cols] = u_parts[gi].reshape(n_grp, SUBLANES, LRU_GROUP)

    sub = lax.broadcasted_iota(jnp.int32, (n_grp, SUBLANES, LANES), 1)
    for ci in range(D_RNN // LANES):
        cols = slice(ci * LANES, (ci + 1) * LANES)
        a3 = a_ref[:, :, cols]
        u3 = u_ref[:, :, cols]
        d = 1
        while d < SUBLANES:
            keep = sub >= d
            a_sh = jnp.where(keep, pltpu.roll(a3, d, 1), 1.0)
            u_sh = jnp.where(keep, pltpu.roll(u3, d, 1), 0.0)
            u3 = a3 * u_sh + u3
            a3 = a3 * a_sh
            d *= 2
        h_prev = jnp.broadcast_to(carry_ref[:, cols], (SUBLANES, LANES))
        for g in range(0, n_grp, 2):
            h0 = a3[g] * h_prev + u3[g]
            h_prev = jnp.broadcast_to(h0[SUBLANES - 1:, :], (SUBLANES, LANES))
            h1 = a3[g + 1] * h_prev + u3[g + 1]
            h_prev = jnp.broadcast_to(h1[SUBLANES - 1:, :], (SUBLANES, LANES))
            rows = slice(g * SUBLANES, (g + 2) * SUBLANES)
            h = jnp.concatenate([h0, h1], axis=0)
            y_ref[rows, cols] = (jax.nn.gelu(gate_ref[rows, cols]) * h).astype(BF16)
        carry_ref[:, cols] = h_prev[0:1, :]
        hlast_ref[:, cols] = h_prev[0:1, :]


def lru_seq(gx, conv_w, conv_b, wg, b_a, b_i, lam, *, batch, seq_len, tt):
    m = gx.shape[0]
    nt = seq_len // tt
    vec = lambda: pl.BlockSpec((1, D_RNN), lambda b, t: (0, 0))
    y, h_last = pl.pallas_call(
        _lru_seq_kernel,
        grid=(batch, nt),
        in_specs=[
            pl.BlockSpec((tt, D_RNN), lambda b, t: (b * nt + t, 0)),
            pl.BlockSpec((tt, D_RNN), lambda b, t: (b * nt + t, 1)),
            pl.BlockSpec((CONV_A, D_RNN), lambda b, t: (0, 0)),
            vec(),
            pl.BlockSpec((N_LRU_GROUPS, LRU_GROUP, 2 * LRU_GROUP), lambda b, t: (0, 0, 0)),
            vec(), vec(), vec(),
        ],
        out_specs=[
            pl.BlockSpec((tt, D_RNN), lambda b, t: (b * nt + t, 0)),
            pl.BlockSpec((None, 1, D_RNN), lambda b, t: (b, 0, 0)),
        ],
        out_shape=[jax.ShapeDtypeStruct((m, D_RNN), BF16), jax.ShapeDtypeStruct((batch, 1, D_RNN), F32)],
        scratch_shapes=[pltpu.VMEM((tt + SUBLANES, D_RNN), F32), pltpu.VMEM((tt // SUBLANES, SUBLANES, D_RNN), F32),
                        pltpu.VMEM((tt // SUBLANES, SUBLANES, D_RNN), F32), pltpu.VMEM((1, D_RNN), F32)],
        compiler_params=_params(2),
        name="lru_seq",
    )(gx, gx, conv_w, conv_b.reshape(1, D_RNN), wg, b_a.reshape(1, D_RNN), b_i.reshape(1, D_RNN),
      lam.reshape(1, D_RNN))
    return y, h_last.reshape(batch, D_RNN)


def _lru_step_kernel(gx_ref, h0_ref, cbuf_ref, cw_ref, cb_ref, wg_ref, ba_ref, bi_ref, lam_ref, y_ref, h_ref):
    gate = gx_ref[:, :D_RNN]
    xr = gx_ref[:, D_RNN:]
    cw = cw_ref[...]
    xc = cb_ref[...] + xr * cw[CONV_A - 1:CONV_A]
    for j in range(CONV_A - 1):
        xc = xc + cbuf_ref[j] * cw[j:j + 1]
    a_parts, u_parts = _lru_gates(xc, wg_ref, ba_ref[...], bi_ref[...], lam_ref[...])
    a = jnp.concatenate(a_parts, axis=1)
    u = jnp.concatenate(u_parts, axis=1)
    h = u + a * h0_ref[...]
    h_ref[...] = h
    y_ref[...] = (jax.nn.gelu(gate) * h).astype(BF16)


def lru_step(gx, h0, cbuf, conv_w, conv_b, wg, b_a, b_i, lam):
    m = gx.shape[0]
    return pl.pallas_call(
        _lru_step_kernel,
        out_shape=[jax.ShapeDtypeStruct((m, D_RNN), BF16), jax.ShapeDtypeStruct((m, D_RNN), F32)],
        compiler_params=pltpu.CompilerParams(vmem_limit_bytes=V7X_VMEM_LIMIT),
        name="lru_step",
    )(gx, h0, cbuf, conv_w, conv_b.reshape(1, D_RNN), wg, b_a.reshape(1, D_RNN), b_i.reshape(1, D_RNN),
      lam.reshape(1, D_RNN))


def _lru_gate_weights(w_a, w_i):
    per = LRU_GROUP // LRU_BLOCK

    def blockdiag(w):
        w = w.reshape(N_LRU_GROUPS, per, LRU_BLOCK, LRU_BLOCK)
        eye = jnp.eye(per, dtype=w.dtype)
        return jnp.einsum('gpcd,pq->gpcqd', w, eye).reshape(N_LRU_GROUPS, LRU_GROUP, LRU_GROUP)

    return jnp.concatenate([blockdiag(w_a), blockdiag(w_i)], axis=2).astype(BF16)


def _rope_tables(pos):
    half = ROT_DIM // 2
    inv = ROPE_THETA ** (-jnp.arange(half, dtype=F32) * (2.0 / ROT_DIM))
    ang = pos[:, None] * inv[None, :]
    cos, sin = jnp.cos(ang), jnp.sin(ang)
    rows = pos.shape[0]
    ones = jnp.ones((rows, HEAD_DIM - ROT_DIM), F32)
    zeros_h = jnp.zeros((rows, half), F32)
    zeros_t = jnp.zeros((rows, HEAD_DIM - ROT_DIM), F32)
    c = jnp.concatenate([cos, cos, ones], axis=1)
    s_lo = jnp.concatenate([-sin, zeros_h, zeros_t], axis=1)
    s_hi = jnp.concatenate([zeros_h, sin, zeros_t], axis=1)
    rep = LANES // HEAD_DIM
    return tuple(jnp.tile(t, (1, rep)) for t in (c, s_lo, s_hi))


def _rope_kernel(qk_ref, c_ref, slo_ref, shi_ref, q_ref, k_ref):
    c, s_lo, s_hi = c_ref[...], slo_ref[...], shi_ref[...]
    half = ROT_DIM // 2
    n_q = N_HEADS * HEAD_DIM // LANES
    for ci in range(QK_COLS // LANES):
        x = qk_ref[:, ci * LANES:(ci + 1) * LANES]
        rot = x * c + pltpu.roll(x, LANES - half, 1) * s_lo + pltpu.roll(x, half, 1) * s_hi
        if ci < n_q:
            q_ref[:, ci * LANES:(ci + 1) * LANES] = rot.astype(BF16)
        else:
            k_ref[:, (ci - n_q) * LANES:(ci - n_q + 1) * LANES] = rot


def rope(qkv, tables, *, tm):
    m = qkv.shape[0]
    nt = tables[0].shape[0] // tm
    tab = lambda: pl.BlockSpec((tm, LANES), lambda i: (i % nt, 0))
    return pl.pallas_call(
        _rope_kernel,
        grid=(m // tm,),
        in_specs=[pl.BlockSpec((tm, QK_COLS), lambda i: (i, 0)), tab(), tab(), tab()],
        out_specs=[pl.BlockSpec((tm, N_HEADS * HEAD_DIM), lambda i: (i, 0)),
                   pl.BlockSpec((tm, KV_COLS), lambda i: (i, 0))],
        out_shape=[jax.ShapeDtypeStruct((m, N_HEADS * HEAD_DIM), BF16), jax.ShapeDtypeStruct((m, KV_COLS), F32)],
        compiler_params=_params(1),
        name="rope",
    )(qkv, *tables)


def _swa_seq_kernel(sink_ref, q_ref, kp_ref, kc_ref, vp_ref, vc_ref, o_ref):
    qi = pl.program_id(1)
    tq = q_ref.shape[0]
    nk = 2 * tq
    kk = jnp.concatenate([kp_ref[...], kc_ref[...]], axis=0) * (HEAD_DIM ** -0.5)
    vv = jnp.concatenate([vp_ref[...], vc_ref[...]], axis=0).astype(BF16)
    key = lax.broadcasted_iota(jnp.int32, (nk, tq), 0)
    qry = lax.broadcasted_iota(jnp.int32, (nk, tq), 1)
    rel = tq + qry - key
    valid = (rel >= 0) & (rel < WINDOW) & ((qi > 0) | (key >= tq))
    low_k = lax.broadcasted_iota(jnp.int32, (nk, LANES), 1) < HEAD_DIM
    low_o = lax.broadcasted_iota(jnp.int32, (tq, LANES), 1) < HEAD_DIM
    for kh in range(N_KV):
        blk = slice((kh // 2) * LANES, (kh // 2 + 1) * LANES)
        in_low = kh % 2 == 0
        k_blk = kk[:, blk]
        k_swap = pltpu.roll(k_blk, HEAD_DIM, 1)
        k_lo = jnp.where(low_k, k_blk if in_low else k_swap, 0.0).astype(BF16)
        k_hi = jnp.where(low_k, 0.0, k_swap if in_low else k_blk).astype(BF16)
        v_blk = vv[:, blk]
        for hp in range(GROUP // 2):
            h0 = kh * GROUP + 2 * hp
            hcols = slice(h0 * HEAD_DIM, (h0 + 2) * HEAD_DIM)
            q_pair = q_ref[:, hcols]
            outs = []
            for which, k_pad in enumerate((k_lo, k_hi)):
                s = jnp.where(valid, _dot_nt(k_pad, q_pair), -jnp.inf)
                sink = sink_ref[h0 + which]
                mx = jnp.maximum(jnp.max(s, axis=0, keepdims=True), sink)
                e = jnp.exp(s - mx)
                denom = jnp.sum(e, axis=0, keepdims=True) + jnp.exp(sink - mx)
                p = (e * (1.0 / denom)).astype(BF16)
                outs.append(_dot_tn(p, v_blk))
            if in_low:
                o_pair = jnp.where(low_o, outs[0], pltpu.roll(outs[1], HEAD_DIM, 1))
            else:
                o_pair = jnp.where(low_o, pltpu.roll(outs[0], HEAD_DIM, 1), outs[1])
            o_ref[:, hcols] = o_pair.astype(BF16)


def swa_seq(q_rot, k_rot, qkv, sinks, *, batch, seq_len):
    m = q_rot.shape[0]
    tq = WINDOW
    nq = seq_len // tq
    v_col = QK_COLS // KV_COLS
    prev = lambda col: pl.BlockSpec((tq, KV_COLS), lambda b, i: (b * nq + jnp.maximum(i - 1, 0), col))
    cur = lambda col: pl.BlockSpec((tq, KV_COLS), lambda b, i: (b * nq + i, col))
    return pl.pallas_call(
        _swa_seq_kernel,
        grid=(batch, nq),
        in_specs=[
            pl.BlockSpec(memory_space=pltpu.SMEM),
            pl.BlockSpec((tq, N_HEADS * HEAD_DIM), lambda b, i: (b * nq + i, 0)),
            prev(0), cur(0), prev(v_col), cur(v_col),
        ],
        out_specs=pl.BlockSpec((tq, N_HEADS * HEAD_DIM), lambda b, i: (b * nq + i, 0)),
        out_shape=jax.ShapeDtypeStruct((m, N_HEADS * HEAD_DIM), BF16),
        compiler_params=_params(2),
        name="swa_seq",
    )(sinks, q_rot, k_rot, k_rot, qkv, qkv)


def _swa_step_kernel(q_ref, kn_ref, vn_ref, ck_ref, cv_ref, sink_ref, o_ref):
    ck = ck_ref[...].astype(BF16)
    cv = cv_ref[...].astype(BF16)
    col = lax.broadcasted_iota(jnp.int32, (GROUP, WINDOW), 1)
    for kh in range(N_KV):
        kcols = slice(kh * HEAD_DIM, (kh + 1) * HEAD_DIM)
        heads = slice(kh * GROUP, (kh + 1) * GROUP)
        q = q_ref[heads, :]
        k_new = kn_ref[kh:kh + 1, :].astype(BF16).astype(F32)
        v_new = vn_ref[kh:kh + 1, :].astype(BF16).astype(F32)
        scale = HEAD_DIM ** -0.5
        s_c = _dot_nt(q, ck[:, kcols]) * scale
        s_c = jnp.where(col >= 1, s_c, -jnp.inf)
        s_n = jnp.sum(q.astype(F32) * k_new, axis=-1, keepdims=True) * scale
        sink = sink_ref[heads, :]
        mx = jnp.maximum(jnp.maximum(jnp.max(s_c, axis=-1, keepdims=True), s_n), sink)
        p_c = jnp.exp(s_c - mx)
        p_n = jnp.exp(s_n - mx)
        denom = jnp.sum(p_c, axis=-1, keepdims=True) + p_n + jnp.exp(sink - mx)
        o = _dot(p_c.astype(BF16), cv[:, kcols]) + p_n.astype(BF16).astype(F32) * v_new
        o_ref[heads, :] = (o / denom).astype(BF16)


def swa_step(q3, k_new, v_new, cache_k, cache_v, sinks):
    b = q3.shape[0]
    return pl.pallas_call(
        _swa_step_kernel,
        grid=(b,),
        in_specs=[
            pl.BlockSpec((None, N_HEADS, HEAD_DIM), lambda i: (i, 0, 0)),
            pl.BlockSpec((None, N_KV, HEAD_DIM), lambda i: (i, 0, 0)),
            pl.BlockSpec((None, N_KV, HEAD_DIM), lambda i: (i, 0, 0)),
            pl.BlockSpec((None, WINDOW, KV_COLS), lambda i: (i, 0, 0)),
            pl.BlockSpec((None, WINDOW, KV_COLS), lambda i: (i, 0, 0)),
            pl.BlockSpec((N_HEADS, 1), lambda i: (0, 0)),
        ],
        out_specs=pl.BlockSpec((None, N_HEADS, HEAD_DIM), lambda i: (i, 0, 0)),
        out_shape=jax.ShapeDtypeStruct((b, N_HEADS, HEAD_DIM), BF16),
        compiler_params=_params(1),
        name="swa_step",
    )(q3, k_new, v_new, cache_k, cache_v, sinks.reshape(N_HEADS, 1))


def _hgrn_lower_bound(logits, layer):
    mx = jnp.max(logits, axis=0, keepdims=True)
    e = jnp.exp(logits - mx)
    sm = e / jnp.sum(e, axis=0, keepdims=True)
    lb = jnp.zeros_like(sm[0:1])
    for i in range(1, layer + 1):
        lb = lb + sm[i:i + 1]
    return lb


def _hgrn_consts():
    c = HG_CHUNK
    t = np.arange(c)[:, None]
    s = np.arange(c)[None, :]
    tri = (s <= t).astype(np.float32)
    sel = [tri]
    msk = [(s == t)]
    for lvl in range(1, HG_LEVELS + 1):
        w = 1 << (lvl - 1)
        ref_row = (t // (2 * w)) * (2 * w) + w - 1
        upper_t = (t % (2 * w)) >= w
        lower_s = (s % (2 * w)) < w
        sign = np.where(upper_t, 1.0, -1.0)
        sel.append(sign * (tri - (s <= ref_row)))
        msk.append(((t // (2 * w)) == (s // (2 * w))) & upper_t & lower_s)
    sel.append(1.0 - tri)
    sel = np.concatenate(sel, axis=0).astype(np.float32)
    sel2 = np.concatenate([sel, sel], axis=1)
    msk = np.stack(msk, axis=0).astype(np.float32)
    return jnp.asarray(sel2, BF16), jnp.asarray(msk, F32)


def _hgrn_seq_kernel(q_ref, z_ref, v_ref, g_ref, lbl_ref, gn_ref, sel_ref, msk_ref, y_ref, s_out_ref, st_ref,
                     *, layer, heads_per_step):
    ti = pl.program_id(2)
    tt = q_ref.shape[0]
    c = HG_CHUNK

    @pl.when(ti == 0)
    def _():
        st_ref[...] = jnp.zeros(st_ref.shape, F32)

    lb_all = _hgrn_lower_bound(lbl_ref[...], layer)
    gn = gn_ref[...]
    row = lax.broadcasted_iota(jnp.int32, (c, heads_per_step * HG_DK), 0)
    upper = [None] + [((row >> (lvl - 1)) & 1) == 1 for lvl in range(1, HG_LEVELS + 1)]
    states = [st_ref[hh] for hh in range(heads_per_step)]

    head_cols = [slice(hh * HG_DK, (hh + 1) * HG_DK) for hh in range(heads_per_step)]

    def decays(ci):
        rows = slice(ci * c, (ci + 1) * c)
        q = q_ref[rows, :]
        sq = q * _sigmoid(q)
        sz = _sigmoid(z_ref[rows, :])
        log_f = jnp.log(lb_all + (1.0 - lb_all) * sz)
        k = (1.0 - lb_all) * (1.0 - sz)
        hi = log_f.astype(BF16)
        lo = (log_f - hi.astype(F32)).astype(BF16)
        ex = _dot(sel_ref[...], jnp.concatenate([hi, lo], axis=0))
        e_b = jnp.exp(ex[0:c])
        xs = [(jnp.where(upper[lvl], sq, k) * jnp.exp(ex[lvl * c:(lvl + 1) * c])).astype(BF16)
              for lvl in range(1, HG_LEVELS + 1)]
        return dict(rows=rows, v=v_ref[rows, :].astype(BF16), sq=sq.astype(BF16), k=k.astype(BF16), xs=xs,
                    q_dec=(sq * e_b).astype(BF16), k_dec=(k * jnp.exp(ex[(HG_LEVELS + 1) * c:])).astype(BF16),
                    e_last=e_b[c - 1:c, :])

    def scores(d):
        atts = []
        for cols in head_cols:
            att = _dot_nt(d["sq"][:, cols], d["k"][:, cols]) * msk_ref[0]
            for lvl in range(1, HG_LEVELS + 1):
                x = d["xs"][lvl - 1][:, cols]
                att = att + _dot_nt(x, x) * msk_ref[lvl]
            atts.append(att.astype(BF16))
        return atts

    def outputs(d, atts):
        outs = []
        for hh, cols in enumerate(head_cols):
            st = states[hh]
            o = _dot(atts[hh], d["v"][:, cols]) + _dot_nt(d["q_dec"][:, cols], st.astype(BF16))
            states[hh] = st * d["e_last"][:, cols] + _dot_tn(d["v"][:, cols], d["k_dec"][:, cols])
            outs.append(_rms_rows(o, gn))
        g = g_ref[d["rows"], :]
        y_ref[d["rows"], :] = (jnp.concatenate(outs, axis=1) * (g * _sigmoid(g))).astype(BF16)

    n_chunks = tt // c
    stage_a, stage_b = {}, {}
    for step in range(n_chunks + 2):
        if step < n_chunks:
            stage_a[step] = decays(step)
        if 0 <= step - 1 < n_chunks:
            stage_b[step - 1] = scores(stage_a[step - 1])
        if 0 <= step - 2 < n_chunks:
            outputs(stage_a.pop(step - 2), stage_b.pop(step - 2))

    for hh in range(heads_per_step):
        st_ref[hh] = states[hh]

    @pl.when(ti == pl.num_programs(2) - 1)
    def _():
        for hh in range(heads_per_step):
            s_out_ref[hh] = states[hh].T


def hgrn_seq(qzvg, lb_logits, g_norm, *, layer, batch, seq_len, tt, heads_per_step):
    m = qzvg.shape[0]
    nt = seq_len // tt
    hw = heads_per_step * HG_DK
    nh = HG_HEADS // heads_per_step
    sel, msk = _hgrn_consts()
    part = lambda p: pl.BlockSpec((tt, hw), lambda b, h, t: (b * nt + t, p * nh + h))
    y, s_out = pl.pallas_call(
        functools.partial(_hgrn_seq_kernel, layer=layer, heads_per_step=heads_per_step),
        grid=(batch, nh, nt),
        in_specs=[
            part(0), part(1), part(2), part(3),
            pl.BlockSpec((DEPTH, hw), lambda b, h, t: (0, h)),
            pl.BlockSpec((1, HG_DV), lambda b, h, t: (0, 0)),
            pl.BlockSpec(sel.shape, lambda b, h, t: (0, 0)),
            pl.BlockSpec(msk.shape, lambda b, h, t: (0, 0, 0)),
        ],
        out_specs=[
            pl.BlockSpec((tt, hw), lambda b, h, t: (b * nt + t, h)),
            pl.BlockSpec((None, heads_per_step, HG_DK, HG_DV), lambda b, h, t: (b, h, 0, 0)),
        ],
        out_shape=[jax.ShapeDtypeStruct((m, HG_HEADS * HG_DV), BF16),
                   jax.ShapeDtypeStruct((batch, HG_HEADS, HG_DK, HG_DV), F32)],
        scratch_shapes=[pltpu.VMEM((heads_per_step, HG_DV, HG_DK), F32)],
        compiler_params=_params(3),
        name="hgrn_seq",
    )(qzvg, qzvg, qzvg, qzvg, lb_logits, g_norm.reshape(1, HG_DV), sel, msk)
    return y, s_out


def _hgrn_step_kernel(x_ref, s_ref, lbl_ref, gn_ref, y_ref, s_out_ref, *, layer):
    nh = HG_HEADS
    q = x_ref[0:nh, :]
    z = x_ref[nh:2 * nh, :]
    v = x_ref[2 * nh:3 * nh, :]
    g = x_ref[3 * nh:4 * nh, :]
    lb = _hgrn_lower_bound(lbl_ref[...], layer)[0]
    sq = q * _sigmoid(q)
    sz = _sigmoid(z)
    f = lb + (1.0 - lb) * sz
    k = (1.0 - lb) * (1.0 - sz)
    pad = jnp.zeros((LANES - 3 * nh, HG_DK), F32)
    cols = jnp.concatenate([f, k, sq, pad], axis=0).T
    outs = []
    for h in range(nh):
        f_col = cols[:, h:h + 1]
        k_col = cols[:, nh + h:nh + h + 1]
        q_col = cols[:, 2 * nh + h:2 * nh + h + 1]
        s_new = s_ref[h] * f_col + k_col * v[h:h + 1, :]
        s_out_ref[h] = s_new
        outs.append(jnp.sum(s_new * q_col, axis=0, keepdims=True))
    o = jnp.concatenate(outs, axis=0)
    y_ref[...] = (_rms_rows(o, gn_ref[...]) * (g * _sigmoid(g))).astype(BF16)


def hgrn_step(x4, state, lb_logits, g_norm, *, layer):
    b = x4.shape[0]
    return pl.pallas_call(
        functools.partial(_hgrn_step_kernel, layer=layer),
        grid=(b,),
        in_specs=[
            pl.BlockSpec((None, 4 * HG_HEADS, HG_DK), lambda i: (i, 0, 0)),
            pl.BlockSpec((None, HG_HEADS, HG_DK, HG_DV), lambda i: (i, 0, 0, 0)),
            pl.BlockSpec((DEPTH, HG_HEADS, HG_DK), lambda i: (0, 0, 0)),
            pl.BlockSpec((1, HG_DV), lambda i: (0, 0)),
        ],
        out_specs=[
            pl.BlockSpec((None, HG_HEADS, HG_DV), lambda i: (i, 0, 0)),
            pl.BlockSpec((None, HG_HEADS, HG_DK, HG_DV), lambda i: (i, 0, 0, 0)),
        ],
        out_shape=[jax.ShapeDtypeStruct((b, HG_HEADS, HG_DV), BF16),
                   jax.ShapeDtypeStruct((b, HG_HEADS, HG_DK, HG_DV), F32)],
        compiler_params=_params(1),
        name="hgrn_step",
    )(x4, state, lb_logits.reshape(DEPTH, HG_HEADS, HG_DK), g_norm.reshape(1, HG_DV))


def _trunk(x3, xs3, states, w):
    batch, seq_len, _ = x3.shape
    m = batch * seq_len
    ns = xs3.shape[0]
    x = x3.reshape(m, D_MODEL)
    xs = xs3.reshape(1, ns, D_MODEL)
    tm = seq_len
    tn_in, tn_out, tf = 512, 256, 256
    zeros = lambda n: jnp.zeros((n,), F32)
    lru_wg = [_lru_gate_weights(w["lru_w_a"][j], w["lru_w_i"][j]) for j in range(w["lru_w_a"].shape[0])]
    rope_seq = _rope_tables(jnp.arange(seq_len, dtype=F32))
    rope_step = _rope_tables(jnp.full((ns,), PAST_LEN, F32))

    def out_proj(a, a_s, w_stack, li, b, res, res_s):
        k_blocks = 1
        while tm * (a.shape[1] // k_blocks) * a.dtype.itemsize >= SINGLE_BUFFER_BYTES:
            k_blocks *= 2
        k = a.shape[1] // k_blocks
        cap = max(LANES, min(tn_out, W_TILE_BYTES // (4 * k) // LANES * LANES))
        for ki in range(k_blocks):
            bias = b if ki == k_blocks - 1 else jnp.zeros_like(b)
            res, res_s = matmul_residual(a, a_s, w_stack, li, bias, res, res_s, tm=tm,
                                         tn=_largest_tile(D_MODEL, cap), k_blocks=k_blocks, k_index=ki)
        return res, res_s

    new_p = {"lru_h": [], "lru_conv": [], "swa_k": [], "swa_v": [], "hgrn": [], "ffn_conv": []}
    new_s = {"lru_h": [], "lru_conv": [], "swa_k": [], "swa_v": [], "hgrn": [], "ffn_conv": []}
    for layer in range(DEPTH):
        kind, j = LAYER_MIXER[layer], LAYER_SLOT[layer]
        g_mix = w["norm_mix"][layer]
        if kind == 0:
            n = 2 * D_RNN
            gx, gxs = norm_matmul(x, xs, g_mix, w["lru_w_in"], j, zeros(n), tm=tm, tn=_largest_tile(n, tn_in))
            gxs = gxs[0]
            lru_w = (w["lru_conv_w"][j], w["lru_conv_b"][j], lru_wg[j], w["lru_b_a"][j], w["lru_b_i"][j],
                     w["lru_lambda"][j])
            y, h_new = lru_seq(gx, *lru_w, batch=batch, seq_len=seq_len, tt=256)
            new_p["lru_h"].append(h_new)
            new_p["lru_conv"].append(gx.reshape(batch, seq_len, n)[:, seq_len - (CONV_A - 1):, D_RNN:])
            cbuf = states["lru_conv"][j]
            ys, hs_new = lru_step(gxs, states["lru_h"][j], jnp.swapaxes(cbuf, 0, 1), *lru_w)
            new_s["lru_h"].append(hs_new)
            new_s["lru_conv"].append(jnp.concatenate([cbuf[:, 1:], gxs[:, None, D_RNN:]], axis=1))
            x, xs = out_proj(y, ys[None], w["lru_w_out"], j, zeros(D_MODEL), x, xs)
        elif kind == 1:
            n = QK_COLS + KV_COLS
            qkv, qkvs = norm_matmul(x, xs, g_mix, w["swa_w_qkv"], j, w["swa_b_qkv"][j], tm=tm,
                                    tn=_largest_tile(n, tn_in))
            qkvs = qkvs[0]
            q_rot, k_rot = rope(qkv, rope_seq, tm=1024)
            v = qkv[:, QK_COLS:]
            o = swa_seq(q_rot, k_rot, qkv, w["swa_sinks"][j], batch=batch, seq_len=seq_len)
            new_p["swa_k"].append(k_rot.reshape(batch, seq_len, N_KV, HEAD_DIM)[:, seq_len - WINDOW:])
            new_p["swa_v"].append(v.reshape(batch, seq_len, N_KV, HEAD_DIM)[:, seq_len - WINDOW:])
            qs_rot, ks_rot = rope(qkvs, rope_step, tm=ns)
            ck = states["swa_k"][j]
            cv = states["swa_v"][j]
            k_new = ks_rot.reshape(ns, N_KV, HEAD_DIM)
            v_new = qkvs[:, QK_COLS:].reshape(ns, N_KV, HEAD_DIM)
            os_ = swa_step(qs_rot.reshape(ns, N_HEADS, HEAD_DIM), k_new, v_new,
                           ck.reshape(ns, WINDOW, KV_COLS), cv.reshape(ns, WINDOW, KV_COLS), w["swa_sinks"][j])
            new_s["swa_k"].append(jnp.concatenate([ck[:, 1:], k_new[:, None]], axis=1))
            new_s["swa_v"].append(jnp.concatenate([cv[:, 1:], v_new[:, None]], axis=1))
            x, xs = out_proj(o, os_.reshape(1, ns, N_HEADS * HEAD_DIM), w["swa_w_o"], j, w["swa_b_o"][j], x, xs)
        else:
            n = 2 * HG_HEADS * HG_DK + 2 * HG_HEADS * HG_DV
            qzvg, qzvgs = norm_matmul(x, xs, g_mix, w["hg_w_in"], j, zeros(n), tm=tm, tn=_largest_tile(n, tn_in))
            y, s_new = hgrn_seq(qzvg, w["hg_lb_logits"], w["hg_norm"][j], layer=layer, batch=batch,
                                seq_len=seq_len, tt=512, heads_per_step=4)
            new_p["hgrn"].append(s_new)
            ys, ss_new = hgrn_step(qzvgs[0].reshape(ns, 4 * HG_HEADS, HG_DK), states["hgrn"][j], w["hg_lb_logits"],
                                   w["hg_norm"][j], layer=layer)
            new_s["hgrn"].append(ss_new)
            x, xs = out_proj(y, ys.reshape(1, ns, HG_HEADS * HG_DV), w["hg_w_o"], j, zeros(D_MODEL), x, xs)

        fbuf = states["ffn_conv"][layer]
        act, act_s, tail, gate_s = ffn_in(x, xs, w["norm_ffn"][layer], w["ffn_w_in"], w["ffn_conv_w"],
                                          w["ffn_conv_b"], layer, jnp.swapaxes(fbuf, 0, 1), tm=tm, tf=tf)
        new_p["ffn_conv"].append(tail[:, SUBLANES - (CONV_F - 1):])
        new_s["ffn_conv"].append(jnp.concatenate([fbuf[:, 1:], gate_s[0][:, None]], axis=1))
        x, xs = out_proj(act, act_s, w["ffn_w_out"], layer, zeros(D_MODEL), x, xs)

    y = final_norm(x, w["norm_final"], tm=1024).reshape(batch, seq_len, D_MODEL)
    ys = final_norm(xs[0], w["norm_final"], tm=ns).reshape(ns, 1, D_MODEL)
    order = ("lru_h", "lru_conv", "swa_k", "swa_v", "hgrn", "ffn_conv")
    return ((y, ys) + tuple(jnp.stack(new_p[k]) for k in order) + tuple(jnp.stack(new_s[k]) for k in order))


def kernel(x_prompt, x_sample, state_lru_h, state_lru_conv, cache_swa_k, cache_swa_v, state_hgrn, state_ffn_conv,
           norm_mix, norm_ffn, norm_final,
           lru_w_in, lru_conv_w, lru_conv_b, lru_w_a, lru_b_a, lru_w_i, lru_b_i, lru_lambda, lru_w_out,
           swa_w_qkv, swa_b_qkv, swa_sinks, swa_w_o, swa_b_o,
           hg_w_in, hg_lb_logits, hg_norm, hg_w_o,
           ffn_w_in, ffn_conv_w, ffn_conv_b, ffn_w_out):
    w = dict(norm_mix=norm_mix, norm_ffn=norm_ffn, norm_final=norm_final,
             lru_w_in=lru_w_in, lru_conv_w=lru_conv_w, lru_conv_b=lru_conv_b, lru_w_a=lru_w_a, lru_b_a=lru_b_a,
             lru_w_i=lru_w_i, lru_b_i=lru_b_i, lru_lambda=lru_lambda, lru_w_out=lru_w_out,
             swa_w_qkv=swa_w_qkv, swa_b_qkv=swa_b_qkv, swa_sinks=swa_sinks, swa_w_o=swa_w_o, swa_b_o=swa_b_o,
             hg_w_in=hg_w_in, hg_lb_logits=hg_lb_logits, hg_norm=hg_norm, hg_w_o=hg_w_o,
             ffn_w_in=ffn_w_in, ffn_conv_w=ffn_conv_w, ffn_conv_b=ffn_conv_b, ffn_w_out=ffn_w_out)
    states = dict(lru_h=state_lru_h, lru_conv=state_lru_conv, swa_k=cache_swa_k, swa_v=cache_swa_v,
                  hgrn=state_hgrn, ffn_conv=state_ffn_conv)
    return _trunk(x_prompt, x_sample, states, w)
```

```python
import functools

import numpy as np
import jax
import jax.numpy as jnp
from jax import lax
from jax.experimental import pallas as pl
from jax.experimental.pallas import tpu as pltpu

F32 = jnp.float32
BF16 = jnp.bfloat16

D_MODEL = 2048
DEPTH = 4
PAST_LEN = 16384
EPS = 1e-6
LAYER_MIXER = tuple(i % 3 for i in range(DEPTH))
LAYER_SLOT = tuple(LAYER_MIXER[:i].count(LAYER_MIXER[i]) for i in range(DEPTH))

D_RNN = 2560
LRU_BLOCKS = 16
LRU_BLOCK = D_RNN // LRU_BLOCKS
LRU_GROUP = 640
N_LRU_GROUPS = D_RNN // LRU_GROUP
CONV_A = 4
LRU_C = 8.0

N_HEADS = 32
N_KV = 4
HEAD_DIM = 64
GROUP = N_HEADS // N_KV
WINDOW = 128
ROT_DIM = HEAD_DIM // 4
ROPE_THETA = 500000.0
QK_COLS = (N_HEADS + N_KV) * HEAD_DIM
KV_COLS = N_KV * HEAD_DIM

HG_HEADS = 16
HG_DK = 128
HG_DV = 128
HG_CHUNK = 64
HG_LEVELS = 6

D_FF = 3 * D_MODEL
CONV_F = 3

LANES = 128
SUBLANES = 8
V7X_VMEM_LIMIT = 56 * 1024 * 1024
SINGLE_BUFFER_BYTES = 14 * 1024 * 1024
W_TILE_BYTES = 8 * 1024 * 1024
DOT_ROWS = 256


def _params(n_axes, vmem=V7X_VMEM_LIMIT):
    return pltpu.CompilerParams(dimension_semantics=("arbitrary",) * n_axes, vmem_limit_bytes=vmem)


def _sigmoid(x):
    return 1.0 / (1.0 + jnp.exp(-x))


def _rms_rows(x, g):
    var = jnp.mean(x * x, axis=-1, keepdims=True)
    return x * lax.rsqrt(var + EPS) * g


def _dot(a, b):
    return jnp.dot(a, b, preferred_element_type=F32)


def _dot_nt(a, b):
    return lax.dot_general(a, b, (((1,), (1,)), ((), ())), preferred_element_type=F32)


def _dot_tn(a, b):
    return lax.dot_general(a, b, (((0,), (0,)), ((), ())), preferred_element_type=F32)


def _largest_tile(n, cap):
    t = cap
    while n % t:
        t -= LANES
    return t


def _row_tile_spec(shape, index_map, dtype):
    if shape[0] * shape[1] * jnp.dtype(dtype).itemsize >= SINGLE_BUFFER_BYTES:
        return pl.BlockSpec(shape, index_map, pipeline_mode=pl.Buffered(1))
    return pl.BlockSpec(shape, index_map)


def _row_blocks(n_rows):
    step = min(n_rows, DOT_ROWS)
    return [slice(r, r + step) for r in range(0, n_rows, step)]


def _norm_matmul_kernel(x_ref, xs_ref, g_ref, w_ref, b_ref, o_ref, os_ref, hn_ref):
    tm, ns = x_ref.shape[0], xs_ref.shape[0]

    @pl.when(pl.program_id(1) == 0)
    def _():
        for rows in _row_blocks(tm):
            hn_ref[rows, :] = _rms_rows(x_ref[rows, :], g_ref[...]).astype(BF16)
        hn_ref[tm:, :] = _rms_rows(xs_ref[...], g_ref[...]).astype(BF16)

    w = w_ref[...].astype(BF16)
    blocks = _row_blocks(tm)
    for rows in blocks[:-1]:
        o_ref[rows, :] = _dot(hn_ref[rows, :], w) + b_ref[...]
    last = blocks[-1]
    res = _dot(hn_ref[last.start:, :], w) + b_ref[...]
    o_ref[last, :] = res[:tm - last.start]
    os_ref[...] = res[tm - last.start:]


def norm_matmul(x, xs, g, w_stack, li, b, *, tm, tn):
    m, k = x.shape
    ns = xs.shape[1]
    n = w_stack.shape[2]
    return pl.pallas_call(
        _norm_matmul_kernel,
        grid=(m // tm, n // tn),
        in_specs=[
            _row_tile_spec((tm, k), lambda i, j: (i, 0), F32),
            pl.BlockSpec((None, ns, k), lambda i, j: (0, 0, 0)),
            pl.BlockSpec((1, k), lambda i, j: (0, 0)),
            pl.BlockSpec((None, k, tn), lambda i, j: (li, 0, j)),
            pl.BlockSpec((1, tn), lambda i, j: (0, j)),
        ],
        out_specs=[pl.BlockSpec((tm, tn), lambda i, j: (i, j)),
                   pl.BlockSpec((None, ns, tn), lambda i, j: (i, 0, j))],
        out_shape=[jax.ShapeDtypeStruct((m, n), F32), jax.ShapeDtypeStruct((m // tm, ns, n), F32)],
        scratch_shapes=[pltpu.VMEM((tm + ns, k), BF16)],
        compiler_params=_params(2),
        name="norm_matmul",
    )(x, xs, g.reshape(1, k), w_stack, b.reshape(1, n))


def _matmul_residual_kernel(a_ref, as_ref, w_ref, b_ref, x_ref, xs_ref, o_ref, os_ref):
    tm = a_ref.shape[0]
    w = w_ref[...].astype(BF16)
    blocks = _row_blocks(tm)
    for rows in blocks[:-1]:
        o_ref[rows, :] = x_ref[rows, :] + (_dot(a_ref[rows, :], w) + b_ref[...])
    last = blocks[-1]
    lhs = jnp.concatenate([a_ref[last, :], as_ref[...]], axis=0)
    res = _dot(lhs, w) + b_ref[...]
    o_ref[last, :] = x_ref[last, :] + res[:tm - last.start]
    os_ref[...] = xs_ref[...] + res[tm - last.start:]


def matmul_residual(a, a_s, w_stack, li, b, x, xs, *, tm, tn, k_blocks=1, k_index=0):
    m = a.shape[0]
    k = a.shape[1] // k_blocks
    ns = xs.shape[1]
    n = w_stack.shape[2]
    return pl.pallas_call(
        _matmul_residual_kernel,
        grid=(m // tm, n // tn),
        in_specs=[
            _row_tile_spec((tm, k), lambda i, j: (i, k_index), a.dtype),
            pl.BlockSpec((None, ns, k), lambda i, j: (0, 0, k_index)),
            pl.BlockSpec((None, k, tn), lambda i, j: (li, k_index, j)),
            pl.BlockSpec((1, tn), lambda i, j: (0, j)),
            pl.BlockSpec((tm, tn), lambda i, j: (i, j)),
            pl.BlockSpec((None, ns, tn), lambda i, j: (0, 0, j)),
        ],
        out_specs=[pl.BlockSpec((tm, tn), lambda i, j: (i, j)),
                   pl.BlockSpec((None, ns, tn), lambda i, j: (i, 0, j))],
        out_shape=[jax.ShapeDtypeStruct((m, n), F32), jax.ShapeDtypeStruct((m // tm, ns, n), F32)],
        compiler_params=_params(2),
        name="matmul_residual",
    )(a, a_s, w_stack, b.reshape(1, n), x, xs)


def _norm_kernel(x_ref, g_ref, o_ref):
    o_ref[...] = _rms_rows(x_ref[...], g_ref[...])


def final_norm(x, g, *, tm):
    m, k = x.shape
    return pl.pallas_call(
        _norm_kernel,
        grid=(m // tm,),
        in_specs=[pl.BlockSpec((tm, k), lambda i: (i, 0)), pl.BlockSpec((1, k), lambda i: (0, 0))],
        out_specs=pl.BlockSpec((tm, k), lambda i: (i, 0)),
        out_shape=jax.ShapeDtypeStruct((m, k), F32),
        compiler_params=_params(1),
        name="final_norm",
    )(x, g.reshape(1, k))


def _ffn_in_kernel(x_ref, xs_ref, g_ref, wg_ref, wu_ref, cw_ref, cb_ref, prev_ref,
                   act_ref, acts_ref, gt_ref, gts_ref, hn_ref, ext_ref):
    tm = x_ref.shape[0]

    @pl.when(pl.program_id(1) == 0)
    def _():
        for rows in _row_blocks(tm):
            hn_ref[rows, :] = _rms_rows(x_ref[rows, :], g_ref[...]).astype(BF16)
        hn_ref[tm:, :] = _rms_rows(xs_ref[...], g_ref[...]).astype(BF16)
        ext_ref[0:SUBLANES, :] = jnp.zeros((SUBLANES, ext_ref.shape[1]), F32)

    wg = wg_ref[...].astype(BF16)
    wu = wu_ref[...].astype(BF16)
    cw = cw_ref[...]

    def activation(gate, up, prev1, prev2):
        conv = cb_ref[...] + gate * cw[2:3] + prev2 * cw[0:1] + prev1 * cw[1:2]
        return (jax.nn.gelu(conv) * up).astype(BF16)

    def finish(rows, gate, up):
        r0, n = rows.start, rows.stop - rows.start
        ext_ref[SUBLANES + r0:SUBLANES + r0 + n, :] = gate[:n]
        prev1 = ext_ref[SUBLANES - 1 + r0:SUBLANES - 1 + r0 + n, :]
        prev2 = ext_ref[SUBLANES - 2 + r0:SUBLANES - 2 + r0 + n, :]
        act_ref[rows, :] = activation(gate[:n], up[:n], prev1, prev2)
        if rows.stop == tm:
            gt_ref[...] = gate[n - SUBLANES:n, :]
            gts_ref[...] = gate[n:, :]
            acts_ref[...] = activation(gate[n:], up[n:], prev_ref[1], prev_ref[0])

    pending = None
    for rows in _row_blocks(tm):
        hn = hn_ref[rows.start:, :] if rows.stop == tm else hn_ref[rows, :]
        dots = (rows, _dot(hn, wg), _dot(hn, wu))
        if pending is not None:
            finish(*pending)
        pending = dots
    finish(*pending)


def ffn_in(x, xs, g, w_in, conv_w, conv_b, li, prev, *, tm, tf):
    m, k = x.shape
    ns = xs.shape[1]
    nf = D_FF // tf
    n_i = m // tm
    act, act_s, tail, gate_s = pl.pallas_call(
        _ffn_in_kernel,
        grid=(n_i, nf),
        in_specs=[
            _row_tile_spec((tm, k), lambda i, f: (i, 0), F32),
            pl.BlockSpec((None, ns, k), lambda i, f: (0, 0, 0)),
            pl.BlockSpec((1, k), lambda i, f: (0, 0)),
            pl.BlockSpec((None, k, tf), lambda i, f: (li, 0, f)),
            pl.BlockSpec((None, k, tf), lambda i, f: (li, 0, nf + f)),
            pl.BlockSpec((None, CONV_F, tf), lambda i, f: (li, 0, f)),
            pl.BlockSpec((None, 1, tf), lambda i, f: (li, 0, f)),
            pl.BlockSpec((2, ns, tf), lambda i, f: (0, 0, f)),
        ],
        out_specs=[
            pl.BlockSpec((tm, tf), lambda i, f: (i, f)),
            pl.BlockSpec((None, ns, tf), lambda i, f: (i, 0, f)),
            pl.BlockSpec((None, SUBLANES, tf), lambda i, f: (i, 0, f)),
            pl.BlockSpec((None, ns, tf), lambda i, f: (i, 0, f)),
        ],
        out_shape=[jax.ShapeDtypeStruct((m, D_FF), BF16), jax.ShapeDtypeStruct((n_i, ns, D_FF), BF16),
                   jax.ShapeDtypeStruct((n_i, SUBLANES, D_FF), F32), jax.ShapeDtypeStruct((n_i, ns, D_FF), F32)],
        scratch_shapes=[pltpu.VMEM((tm + ns, k), BF16), pltpu.VMEM((tm + SUBLANES, tf), F32)],
        compiler_params=_params(2),
        name="ffn_in",
    )(x, xs, g.reshape(1, k), w_in, w_in, conv_w, conv_b.reshape(DEPTH, 1, D_FF), prev)
    return act, act_s, tail, gate_s


def _lru_gates(xc, wg_ref, ba, bi, lam):
    lam_abs = jnp.abs(lam)
    softplus_neg = jnp.maximum(-lam, 0.0) + jnp.log(1.0 + jnp.exp(-lam_abs))
    a_parts, u_parts = [], []
    for gi in range(N_LRU_GROUPS):
        cols = slice(gi * LRU_GROUP, (gi + 1) * LRU_GROUP)
        xg = xc[:, cols]
        proj = _dot(xg.astype(BF16), wg_ref[gi])
        r = _sigmoid(proj[:, :LRU_GROUP] + ba[:, cols])
        i = _sigmoid(proj[:, LRU_GROUP:] + bi[:, cols])
        log_a = -LRU_C * r * softplus_neg[:, cols]
        a = jnp.exp(log_a)
        a_parts.append(a)
        u_parts.append(jnp.sqrt(1.0 - a * a) * (i * xg))
    return a_parts, u_parts


def _lru_seq_kernel(gate_ref, xr_ref, cw_ref, cb_ref, wg_ref, ba_ref, bi_ref, lam_ref,
                    y_ref, hlast_ref, ext_ref, a_ref, u_ref, carry_ref):
    tt = xr_ref.shape[0]
    n_grp = tt // SUBLANES

    @pl.when(pl.program_id(1) == 0)
    def _():
        ext_ref[0:SUBLANES, :] = jnp.zeros((SUBLANES, D_RNN), F32)
        carry_ref[...] = jnp.zeros((1, D_RNN), F32)

    xr = xr_ref[...]
    ext_ref[SUBLANES:, :] = xr
    cw = cw_ref[...]
    xc = cb_ref[...] + xr * cw[CONV_A - 1:CONV_A]
    for j in range(CONV_A - 1):
        off = SUBLANES - (CONV_A - 1) + j
        xc = xc + ext_ref[off:off + tt, :] * cw[j:j + 1]
    ext_ref[0:SUBLANES, :] = xr[tt - SUBLANES:, :]

    a_parts, u_parts = _lru_gates(xc, wg_ref, ba_ref[...], bi_ref[...], lam_ref[...])
    for gi in range(N_LRU_GROUPS):
        cols = slice(gi * LRU_GROUP, (gi + 1) * LRU_GROUP)
        a_ref[:, :, cols] = a_parts[gi].reshape(n_grp, SUBLANES, LRU_GROUP)
        u_ref[:, :, cols] = u_parts[gi].reshape(n_grp, SUBLANES, LRU_GROUP)

    sub = lax.broadcasted_iota(jnp.int32, (n_grp, SUBLANES, LANES), 1)
    for ci in range(D_RNN // LANES):
        cols = slice(ci * LANES, (ci + 1) * LANES)
        a3 = a_ref[:, :, cols]
        u3 = u_ref[:, :, cols]
        d = 1
        while d < SUBLANES:
            keep = sub >= d
            a_sh = jnp.where(keep, pltpu.roll(a3, d, 1), 1.0)
            u_sh = jnp.where(keep, pltpu.roll(u3, d, 1), 0.0)
            u3 = a3 * u_sh + u3
            a3 = a3 * a_sh
            d *= 2
        h_prev = jnp.broadcast_to(carry_ref[:, cols], (SUBLANES, LANES))
        for g in range(0, n_grp, 2):
            h0 = a3[g] * h_prev + u3[g]
            h_prev = jnp.broadcast_to(h0[SUBLANES - 1:, :], (SUBLANES, LANES))
            h1 = a3[g + 1] * h_prev + u3[g + 1]
            h_prev = jnp.broadcast_to(h1[SUBLANES - 1:, :], (SUBLANES, LANES))
            rows = slice(g * SUBLANES, (g + 2) * SUBLANES)
            h = jnp.concatenate([h0, h1], axis=0)
            y_ref[rows, cols] = (jax.nn.gelu(gate_ref[rows, cols]) * h).astype(BF16)
        carry_ref[:, cols] = h_prev[0:1, :]
        hlast_ref[:, cols] = h_prev[0:1, :]


def lru_seq(gx, conv_w, conv_b, wg, b_a, b_i, lam, *, batch, seq_len, tt):
    m = gx.shape[0]
    nt = seq_len // tt
    vec = lambda: pl.BlockSpec((1, D_RNN), lambda b, t: (0, 0))
    y, h_last = pl.pallas_call(
        _lru_seq_kernel,
        grid=(batch, nt),
        in_specs=[
            pl.BlockSpec((tt, D_RNN), lambda b, t: (b * nt + t, 0)),
            pl.BlockSpec((tt, D_RNN), lambda b, t: (b * nt + t, 1)),
            pl.BlockSpec((CONV_A, D_RNN), lambda b, t: (0, 0)),
            vec(),
            pl.BlockSpec((N_LRU_GROUPS, LRU_GROUP, 2 * LRU_GROUP), lambda b, t: (0, 0, 0)),
            vec(), vec(), vec(),
        ],
        out_specs=[
            pl.BlockSpec((tt, D_RNN), lambda b, t: (b * nt + t, 0)),
            pl.BlockSpec((None, 1, D_RNN), lambda b, t: (b, 0, 0)),
        ],
        out_shape=[jax.ShapeDtypeStruct((m, D_RNN), BF16), jax.ShapeDtypeStruct((batch, 1, D_RNN), F32)],
        scratch_shapes=[pltpu.VMEM((tt + SUBLANES, D_RNN), F32), pltpu.VMEM((tt // SUBLANES, SUBLANES, D_RNN), F32),
                        pltpu.VMEM((tt // SUBLANES, SUBLANES, D_RNN), F32), pltpu.VMEM((1, D_RNN), F32)],
        compiler_params=_params(2),
        name="lru_seq",
    )(gx, gx, conv_w, conv_b.reshape(1, D_RNN), wg, b_a.reshape(1, D_RNN), b_i.reshape(1, D_RNN),
      lam.reshape(1, D_RNN))
    return y, h_last.reshape(batch, D_RNN)


def _lru_step_kernel(gx_ref, h0_ref, cbuf_ref, cw_ref, cb_ref, wg_ref, ba_ref, bi_ref, lam_ref, y_ref, h_ref):
    gate = gx_ref[:, :D_RNN]
    xr = gx_ref[:, D_RNN:]
    cw = cw_ref[...]
    xc = cb_ref[...] + xr * cw[CONV_A - 1:CONV_A]
    for j in range(CONV_A - 1):
        xc = xc + cbuf_ref[j] * cw[j:j + 1]
    a_parts, u_parts = _lru_gates(xc, wg_ref, ba_ref[...], bi_ref[...], lam_ref[...])
    a = jnp.concatenate(a_parts, axis=1)
    u = jnp.concatenate(u_parts, axis=1)
    h = u + a * h0_ref[...]
    h_ref[...] = h
    y_ref[...] = (jax.nn.gelu(gate) * h).astype(BF16)


def lru_step(gx, h0, cbuf, conv_w, conv_b, wg, b_a, b_i, lam):
    m = gx.shape[0]
    return pl.pallas_call(
        _lru_step_kernel,
        out_shape=[jax.ShapeDtypeStruct((m, D_RNN), BF16), jax.ShapeDtypeStruct((m, D_RNN), F32)],
        compiler_params=pltpu.CompilerParams(vmem_limit_bytes=V7X_VMEM_LIMIT),
        name="lru_step",
    )(gx, h0, cbuf, conv_w, conv_b.reshape(1, D_RNN), wg, b_a.reshape(1, D_RNN), b_i.reshape(1, D_RNN),
      lam.reshape(1, D_RNN))


def _lru_gate_weights(w_a, w_i):
    per = LRU_GROUP // LRU_BLOCK

    def blockdiag(w):
        w = w.reshape(N_LRU_GROUPS, per, LRU_BLOCK, LRU_BLOCK)
        eye = jnp.eye(per, dtype=w.dtype)
        return jnp.einsum('gpcd,pq->gpcqd', w, eye).reshape(N_LRU_GROUPS, LRU_GROUP, LRU_GROUP)

    return jnp.concatenate([blockdiag(w_a), blockdiag(w_i)], axis=2).astype(BF16)


def _rope_tables(pos):
    half = ROT_DIM // 2
    inv = ROPE_THETA ** (-jnp.arange(half, dtype=F32) * (2.0 / ROT_DIM))
    ang = pos[:, None] * inv[None, :]
    cos, sin = jnp.cos(ang), jnp.sin(ang)
    rows = pos.shape[0]
    ones = jnp.ones((rows, HEAD_DIM - ROT_DIM), F32)
    zeros_h = jnp.zeros((rows, half), F32)
    zeros_t = jnp.zeros((rows, HEAD_DIM - ROT_DIM), F32)
    c = jnp.concatenate([cos, cos, ones], axis=1)
    s_lo = jnp.concatenate([-sin, zeros_h, zeros_t], axis=1)
    s_hi = jnp.concatenate([zeros_h, sin, zeros_t], axis=1)
    rep = LANES // HEAD_DIM
    return tuple(jnp.tile(t, (1, rep)) for t in (c, s_lo, s_hi))


def _rope_kernel(qk_ref, c_ref, slo_ref, shi_ref, q_ref, k_ref):
    c, s_lo, s_hi = c_ref[...], slo_ref[...], shi_ref[...]
    half = ROT_DIM // 2
    n_q = N_HEADS * HEAD_DIM // LANES
    for ci in range(QK_COLS // LANES):
        x = qk_ref[:, ci * LANES:(ci + 1) * LANES]
        rot = x * c + pltpu.roll(x, LANES - half, 1) * s_lo + pltpu.roll(x, half, 1) * s_hi
        if ci < n_q:
            q_ref[:, ci * LANES:(ci + 1) * LANES] = rot.astype(BF16)
        else:
            k_ref[:, (ci - n_q) * LANES:(ci - n_q + 1) * LANES] = rot


def rope(qkv, tables, *, tm):
    m = qkv.shape[0]
    nt = tables[0].shape[0] // tm
    tab = lambda: pl.BlockSpec((tm, LANES), lambda i: (i % nt, 0))
    return pl.pallas_call(
        _rope_kernel,
        grid=(m // tm,),
        in_specs=[pl.BlockSpec((tm, QK_COLS), lambda i: (i, 0)), tab(), tab(), tab()],
        out_specs=[pl.BlockSpec((tm, N_HEADS * HEAD_DIM), lambda i: (i, 0)),
                   pl.BlockSpec((tm, KV_COLS), lambda i: (i, 0))],
        out_shape=[jax.ShapeDtypeStruct((m, N_HEADS * HEAD_DIM), BF16), jax.ShapeDtypeStruct((m, KV_COLS), F32)],
        compiler_params=_params(1),
        name="rope",
    )(qkv, *tables)


def _swa_seq_kernel(sink_ref, q_ref, kp_ref, kc_ref, vp_ref, vc_ref, o_ref):
    qi = pl.program_id(1)
    tq = q_ref.shape[0]
    nk = 2 * tq
    kk = jnp.concatenate([kp_ref[...], kc_ref[...]], axis=0) * (HEAD_DIM ** -0.5)
    vv = jnp.concatenate([vp_ref[...], vc_ref[...]], axis=0).astype(BF16)
    key = lax.broadcasted_iota(jnp.int32, (nk, tq), 0)
    qry = lax.broadcasted_iota(jnp.int32, (nk, tq), 1)
    rel = tq + qry - key
    valid = (rel >= 0) & (rel < WINDOW) & ((qi > 0) | (key >= tq))
    low_k = lax.broadcasted_iota(jnp.int32, (nk, LANES), 1) < HEAD_DIM
    low_o = lax.broadcasted_iota(jnp.int32, (tq, LANES), 1) < HEAD_DIM
    for kh in range(N_KV):
        blk = slice((kh // 2) * LANES, (kh // 2 + 1) * LANES)
        in_low = kh % 2 == 0
        k_blk = kk[:, blk]
        k_swap = pltpu.roll(k_blk, HEAD_DIM, 1)
        k_lo = jnp.where(low_k, k_blk if in_low else k_swap, 0.0).astype(BF16)
        k_hi = jnp.where(low_k, 0.0, k_swap if in_low else k_blk).astype(BF16)
        v_blk = vv[:, blk]
        for hp in range(GROUP // 2):
            h0 = kh * GROUP + 2 * hp
            hcols = slice(h0 * HEAD_DIM, (h0 + 2) * HEAD_DIM)
            q_pair = q_ref[:, hcols]
            outs = []
            for which, k_pad in enumerate((k_lo, k_hi)):
                s = jnp.where(valid, _dot_nt(k_pad, q_pair), -jnp.inf)
                sink = sink_ref[h0 + which]
                mx = jnp.maximum(jnp.max(s, axis=0, keepdims=True), sink)
                e = jnp.exp(s - mx)
                denom = jnp.sum(e, axis=0, keepdims=True) + jnp.exp(sink - mx)
                p = (e * (1.0 / denom)).astype(BF16)
                outs.append(_dot_tn(p, v_blk))
            if in_low:
                o_pair = jnp.where(low_o, outs[0], pltpu.roll(outs[1], HEAD_DIM, 1))
            else:
                o_pair = jnp.where(low_o, pltpu.roll(outs[0], HEAD_DIM, 1), outs[1])
            o_ref[:, hcols] = o_pair.astype(BF16)


def swa_seq(q_rot, k_rot, qkv, sinks, *, batch, seq_len):
    m = q_rot.shape[0]
    tq = WINDOW
    nq = seq_len // tq
    v_col = QK_COLS // KV_COLS
    prev = lambda col: pl.BlockSpec((tq, KV_COLS), lambda b, i: (b * nq + jnp.maximum(i - 1, 0), col))
    cur = lambda col: pl.BlockSpec((tq, KV_COLS), lambda b, i: (b * nq + i, col))
    return pl.pallas_call(
        _swa_seq_kernel,
        grid=(batch, nq),
        in_specs=[
            pl.BlockSpec(memory_space=pltpu.SMEM),
            pl.BlockSpec((tq, N_HEADS * HEAD_DIM), lambda b, i: (b * nq + i, 0)),
            prev(0), cur(0), prev(v_col), cur(v_col),
        ],
        out_specs=pl.BlockSpec((tq, N_HEADS * HEAD_DIM), lambda b, i: (b * nq + i, 0)),
        out_shape=jax.ShapeDtypeStruct((m, N_HEADS * HEAD_DIM), BF16),
        compiler_params=_params(2),
        name="swa_seq",
    )(sinks, q_rot, k_rot, k_rot, qkv, qkv)


def _swa_step_kernel(q_ref, kn_ref, vn_ref, ck_ref, cv_ref, sink_ref, o_ref):
    col = lax.broadcasted_iota(jnp.int32, (GROUP, WINDOW), 1)
    scale = HEAD_DIM ** -0.5
    items = [(bi, kh) for bi in range(q_ref.shape[0]) for kh in range(N_KV)]
    heads = lambda kh: slice(kh * GROUP, (kh + 1) * GROUP)
    kcols = lambda kh: slice(kh * HEAD_DIM, (kh + 1) * HEAD_DIM)
    scores = []
    for bi, kh in items:
        q = q_ref[bi, heads(kh), :]
        k_new = kn_ref[bi, kh:kh + 1, :].astype(BF16).astype(F32)
        s_c = _dot_nt(q, ck_ref[bi, :, kcols(kh)].astype(BF16)) * scale
        s_c = jnp.where(col >= 1, s_c, -jnp.inf)
        s_n = jnp.sum(q.astype(F32) * k_new, axis=-1, keepdims=True) * scale
        scores.append((s_c, s_n))
    probs = []
    for (bi, kh), (s_c, s_n) in zip(items, scores):
        sink = sink_ref[heads(kh), :]
        mx = jnp.maximum(jnp.maximum(jnp.max(s_c, axis=-1, keepdims=True), s_n), sink)
        p_c = jnp.exp(s_c - mx)
        p_n = jnp.exp(s_n - mx)
        denom = jnp.sum(p_c, axis=-1, keepdims=True) + p_n + jnp.exp(sink - mx)
        probs.append((p_c.astype(BF16), p_n.astype(BF16).astype(F32), denom))
    for (bi, kh), (p_c, p_n, denom) in zip(items, probs):
        v_new = vn_ref[bi, kh:kh + 1, :].astype(BF16).astype(F32)
        o = _dot(p_c, cv_ref[bi, :, kcols(kh)].astype(BF16)) + p_n * v_new
        o_ref[bi, heads(kh), :] = (o / denom).astype(BF16)


def swa_step(q3, k_new, v_new, cache_k, cache_v, sinks, *, seqs_per_step=8):
    b = q3.shape[0]
    nb = seqs_per_step
    return pl.pallas_call(
        _swa_step_kernel,
        grid=(b // nb,),
        in_specs=[
            pl.BlockSpec((nb, N_HEADS, HEAD_DIM), lambda i: (i, 0, 0)),
            pl.BlockSpec((nb, N_KV, HEAD_DIM), lambda i: (i, 0, 0)),
            pl.BlockSpec((nb, N_KV, HEAD_DIM), lambda i: (i, 0, 0)),
            pl.BlockSpec((nb, WINDOW, KV_COLS), lambda i: (i, 0, 0)),
            pl.BlockSpec((nb, WINDOW, KV_COLS), lambda i: (i, 0, 0)),
            pl.BlockSpec((N_HEADS, 1), lambda i: (0, 0)),
        ],
        out_specs=pl.BlockSpec((nb, N_HEADS, HEAD_DIM), lambda i: (i, 0, 0)),
        out_shape=jax.ShapeDtypeStruct((b, N_HEADS, HEAD_DIM), BF16),
        compiler_params=_params(1),
        name="swa_step",
    )(q3, k_new, v_new, cache_k, cache_v, sinks.reshape(N_HEADS, 1))


def _hgrn_lower_bound(logits, layer):
    mx = jnp.max(logits, axis=0, keepdims=True)
    e = jnp.exp(logits - mx)
    sm = e / jnp.sum(e, axis=0, keepdims=True)
    lb = jnp.zeros_like(sm[0:1])
    for i in range(1, layer + 1):
        lb = lb + sm[i:i + 1]
    return lb


def _hgrn_consts():
    c = HG_CHUNK
    t = np.arange(c)[:, None]
    s = np.arange(c)[None, :]
    tri = (s <= t).astype(np.float32)
    sel = [tri]
    msk = [(s == t)]
    for lvl in range(1, HG_LEVELS + 1):
        w = 1 << (lvl - 1)
        ref_row = (t // (2 * w)) * (2 * w) + w - 1
        upper_t = (t % (2 * w)) >= w
        lower_s = (s % (2 * w)) < w
        sign = np.where(upper_t, 1.0, -1.0)
        sel.append(sign * (tri - (s <= ref_row)))
        msk.append(((t // (2 * w)) == (s // (2 * w))) & upper_t & lower_s)
    sel.append(1.0 - tri)
    sel = np.concatenate(sel, axis=0).astype(np.float32)
    sel2 = np.concatenate([sel, sel], axis=1)
    msk = np.stack(msk, axis=0).astype(np.float32)
    return jnp.asarray(sel2, BF16), jnp.asarray(msk, F32)


def _hgrn_seq_kernel(q_ref, z_ref, v_ref, g_ref, lbl_ref, gn_ref, sel_ref, msk_ref, y_ref, s_out_ref, st_ref,
                     *, layer, heads_per_step):
    ti = pl.program_id(2)
    tt = q_ref.shape[0]
    c = HG_CHUNK

    @pl.when(ti == 0)
    def _():
        st_ref[...] = jnp.zeros(st_ref.shape, F32)

    lb_all = _hgrn_lower_bound(lbl_ref[...], layer)
    gn = gn_ref[...]
    row = lax.broadcasted_iota(jnp.int32, (c, heads_per_step * HG_DK), 0)
    upper = [None] + [((row >> (lvl - 1)) & 1) == 1 for lvl in range(1, HG_LEVELS + 1)]
    states = [st_ref[hh] for hh in range(heads_per_step)]

    head_cols = [slice(hh * HG_DK, (hh + 1) * HG_DK) for hh in range(heads_per_step)]

    def decays(ci):
        rows = slice(ci * c, (ci + 1) * c)
        q = q_ref[rows, :]
        sq = q * _sigmoid(q)
        sz = _sigmoid(z_ref[rows, :])
        log_f = jnp.log(lb_all + (1.0 - lb_all) * sz)
        k = (1.0 - lb_all) * (1.0 - sz)
        hi = log_f.astype(BF16)
        lo = (log_f - hi.astype(F32)).astype(BF16)
        ex = _dot(sel_ref[...], jnp.concatenate([hi, lo], axis=0))
        e_b = jnp.exp(ex[0:c])
        xs = [(jnp.where(upper[lvl], sq, k) * jnp.exp(ex[lvl * c:(lvl + 1) * c])).astype(BF16)
              for lvl in range(1, HG_LEVELS + 1)]
        return dict(rows=rows, v=v_ref[rows, :].astype(BF16), sq=sq.astype(BF16), k=k.astype(BF16), xs=xs,
                    q_dec=(sq * e_b).astype(BF16), k_dec=(k * jnp.exp(ex[(HG_LEVELS + 1) * c:])).astype(BF16),
                    e_last=e_b[c - 1:c, :])

    def scores(d):
        atts = []
        for cols in head_cols:
            att = _dot_nt(d["sq"][:, cols], d["k"][:, cols]) * msk_ref[0]
            for lvl in range(1, HG_LEVELS + 1):
                x = d["xs"][lvl - 1][:, cols]
                att = att + _dot_nt(x, x) * msk_ref[lvl]
            atts.append(att.astype(BF16))
        return atts

    def outputs(d, atts):
        outs = []
        for hh, cols in enumerate(head_cols):
            st = states[hh]
            o = _dot(atts[hh], d["v"][:, cols]) + _dot_nt(d["q_dec"][:, cols], st.astype(BF16))
            states[hh] = st * d["e_last"][:, cols] + _dot_tn(d["v"][:, cols], d["k_dec"][:, cols])
            outs.append(_rms_rows(o, gn))
        g = g_ref[d["rows"], :]
        y_ref[d["rows"], :] = (jnp.concatenate(outs, axis=1) * (g * _sigmoid(g))).astype(BF16)

    n_chunks = tt // c
    stage_a, stage_b = {}, {}
    for step in range(n_chunks + 2):
        if step < n_chunks:
            stage_a[step] = decays(step)
        if 0 <= step - 1 < n_chunks:
            stage_b[step - 1] = scores(stage_a[step - 1])
        if 0 <= step - 2 < n_chunks:
            outputs(stage_a.pop(step - 2), stage_b.pop(step - 2))

    for hh in range(heads_per_step):
        st_ref[hh] = states[hh]

    @pl.when(ti == pl.num_programs(2) - 1)
    def _():
        for hh in range(heads_per_step):
            s_out_ref[hh] = states[hh].T


def hgrn_seq(qzvg, lb_logits, g_norm, *, layer, batch, seq_len, tt, heads_per_step):
    m = qzvg.shape[0]
    nt = seq_len // tt
    hw = heads_per_step * HG_DK
    nh = HG_HEADS // heads_per_step
    sel, msk = _hgrn_consts()
    part = lambda p: pl.BlockSpec((tt, hw), lambda b, h, t: (b * nt + t, p * nh + h))
    y, s_out = pl.pallas_call(
        functools.partial(_hgrn_seq_kernel, layer=layer, heads_per_step=heads_per_step),
        grid=(batch, nh, nt),
        in_specs=[
            part(0), part(1), part(2), part(3),
            pl.BlockSpec((DEPTH, hw), lambda b, h, t: (0, h)),
            pl.BlockSpec((1, HG_DV), lambda b, h, t: (0, 0)),
            pl.BlockSpec(sel.shape, lambda b, h, t: (0, 0)),
            pl.BlockSpec(msk.shape, lambda b, h, t: (0, 0, 0)),
        ],
        out_specs=[
            pl.BlockSpec((tt, hw), lambda b, h, t: (b * nt + t, h)),
            pl.BlockSpec((None, heads_per_step, HG_DK, HG_DV), lambda b, h, t: (b, h, 0, 0)),
        ],
        out_shape=[jax.ShapeDtypeStruct((m, HG_HEADS * HG_DV), BF16),
                   jax.ShapeDtypeStruct((batch, HG_HEADS, HG_DK, HG_DV), F32)],
        scratch_shapes=[pltpu.VMEM((heads_per_step, HG_DV, HG_DK), F32)],
        compiler_params=_params(3),
        name="hgrn_seq",
    )(qzvg, qzvg, qzvg, qzvg, lb_logits, g_norm.reshape(1, HG_DV), sel, msk)
    return y, s_out


def _hgrn_step_kernel(x_ref, s_ref, lbl_ref, gn_ref, y_ref, s_out_ref, *, layer):
    nh = HG_HEADS
    lb = _hgrn_lower_bound(lbl_ref[...], layer)[0]
    pad = jnp.zeros((LANES - 3 * nh, HG_DK), F32)
    for bi in range(x_ref.shape[0]):
        q = x_ref[bi, 0:nh, :]
        z = x_ref[bi, nh:2 * nh, :]
        v = x_ref[bi, 2 * nh:3 * nh, :]
        g = x_ref[bi, 3 * nh:4 * nh, :]
        sq = q * _sigmoid(q)
        sz = _sigmoid(z)
        f = lb + (1.0 - lb) * sz
        k = (1.0 - lb) * (1.0 - sz)
        cols = jnp.concatenate([f, k, sq, pad], axis=0).T
        outs = []
        for h in range(nh):
            f_col = cols[:, h:h + 1]
            k_col = cols[:, nh + h:nh + h + 1]
            q_col = cols[:, 2 * nh + h:2 * nh + h + 1]
            s_new = s_ref[bi, h] * f_col + k_col * v[h:h + 1, :]
            s_out_ref[bi, h] = s_new
            outs.append(jnp.sum(s_new * q_col, axis=0, keepdims=True))
        o = jnp.concatenate(outs, axis=0)
        y_ref[bi] = (_rms_rows(o, gn_ref[...]) * (g * _sigmoid(g))).astype(BF16)


def hgrn_step(x4, state, lb_logits, g_norm, *, layer, seqs_per_step=4):
    b = x4.shape[0]
    nb = seqs_per_step
    return pl.pallas_call(
        functools.partial(_hgrn_step_kernel, layer=layer),
        grid=(b // nb,),
        in_specs=[
            pl.BlockSpec((nb, 4 * HG_HEADS, HG_DK), lambda i: (i, 0, 0)),
            pl.BlockSpec((nb, HG_HEADS, HG_DK, HG_DV), lambda i: (i, 0, 0, 0)),
            pl.BlockSpec((DEPTH, HG_HEADS, HG_DK), lambda i: (0, 0, 0)),
            pl.BlockSpec((1, HG_DV), lambda i: (0, 0)),
        ],
        out_specs=[
            pl.BlockSpec((nb, HG_HEADS, HG_DV), lambda i: (i, 0, 0)),
            pl.BlockSpec((nb, HG_HEADS, HG_DK, HG_DV), lambda i: (i, 0, 0, 0)),
        ],
        out_shape=[jax.ShapeDtypeStruct((b, HG_HEADS, HG_DV), BF16),
                   jax.ShapeDtypeStruct((b, HG_HEADS, HG_DK, HG_DV), F32)],
        compiler_params=_params(1),
        name="hgrn_step",
    )(x4, state, lb_logits.reshape(DEPTH, HG_HEADS, HG_DK), g_norm.reshape(1, HG_DV))


def _trunk(x3, xs3, states, w):
    batch, seq_len, _ = x3.shape
    m = batch * seq_len
    ns = xs3.shape[0]
    x = x3.reshape(m, D_MODEL)
    xs = xs3.reshape(1, ns, D_MODEL)
    tm = seq_len
    tn_in, tn_out, tf = 512, 256, 512
    zeros = lambda n: jnp.zeros((n,), F32)
    lru_wg = [_lru_gate_weights(w["lru_w_a"][j], w["lru_w_i"][j]) for j in range(w["lru_w_a"].shape[0])]
    rope_seq = _rope_tables(jnp.arange(seq_len, dtype=F32))
    rope_step = _rope_tables(jnp.full((ns,), PAST_LEN, F32))

    def out_proj(a, a_s, w_stack, li, b, res, res_s):
        k_blocks = 1
        while tm * (a.shape[1] // k_blocks) * a.dtype.itemsize >= SINGLE_BUFFER_BYTES:
            k_blocks *= 2
        k = a.shape[1] // k_blocks
        cap = max(LANES, min(tn_out, W_TILE_BYTES // (4 * k) // LANES * LANES))
        for ki in range(k_blocks):
            bias = b if ki == k_blocks - 1 else jnp.zeros_like(b)
            res, res_s = matmul_residual(a, a_s, w_stack, li, bias, res, res_s, tm=tm,
                                         tn=_largest_tile(D_MODEL, cap), k_blocks=k_blocks, k_index=ki)
        return res, res_s

    new_p = {"lru_h": [], "lru_conv": [], "swa_k": [], "swa_v": [], "hgrn": [], "ffn_conv": []}
    new_s = {"lru_h": [], "lru_conv": [], "swa_k": [], "swa_v": [], "hgrn": [], "ffn_conv": []}
    for layer in range(DEPTH):
        kind, j = LAYER_MIXER[layer], LAYER_SLOT[layer]
        g_mix = w["norm_mix"][layer]
        if kind == 0:
            n = 2 * D_RNN
            gx, gxs = norm_matmul(x, xs, g_mix, w["lru_w_in"], j, zeros(n), tm=tm, tn=_largest_tile(n, tn_in))
            gxs = gxs[0]
            lru_w = (w["lru_conv_w"][j], w["lru_conv_b"][j], lru_wg[j], w["lru_b_a"][j], w["lru_b_i"][j],
                     w["lru_lambda"][j])
            y, h_new = lru_seq(gx, *lru_w, batch=batch, seq_len=seq_len, tt=256)
            new_p["lru_h"].append(h_new)
            new_p["lru_conv"].append(gx.reshape(batch, seq_len, n)[:, seq_len - (CONV_A - 1):, D_RNN:])
            cbuf = states["lru_conv"][j]
            ys, hs_new = lru_step(gxs, states["lru_h"][j], jnp.swapaxes(cbuf, 0, 1), *lru_w)
            new_s["lru_h"].append(hs_new)
            new_s["lru_conv"].append(jnp.concatenate([cbuf[:, 1:], gxs[:, None, D_RNN:]], axis=1))
            x, xs = out_proj(y, ys[None], w["lru_w_out"], j, zeros(D_MODEL), x, xs)
        elif kind == 1:
            n = QK_COLS + KV_COLS
            qkv, qkvs = norm_matmul(x, xs, g_mix, w["swa_w_qkv"], j, w["swa_b_qkv"][j], tm=tm,
                                    tn=_largest_tile(n, tn_in))
            qkvs = qkvs[0]
            q_rot, k_rot = rope(qkv, rope_seq, tm=1024)
            v = qkv[:, QK_COLS:]
            o = swa_seq(q_rot, k_rot, qkv, w["swa_sinks"][j], batch=batch, seq_len=seq_len)
            new_p["swa_k"].append(k_rot.reshape(batch, seq_len, N_KV, HEAD_DIM)[:, seq_len - WINDOW:])
            new_p["swa_v"].append(v.reshape(batch, seq_len, N_KV, HEAD_DIM)[:, seq_len - WINDOW:])
            qs_rot, ks_rot = rope(qkvs, rope_step, tm=ns)
            ck = states["swa_k"][j]
            cv = states["swa_v"][j]
            k_new = ks_rot.reshape(ns, N_KV, HEAD_DIM)
            v_new = qkvs[:, QK_COLS:].reshape(ns, N_KV, HEAD_DIM)
            os_ = swa_step(qs_rot.reshape(ns, N_HEADS, HEAD_DIM), k_new, v_new,
                           ck.reshape(ns, WINDOW, KV_COLS), cv.reshape(ns, WINDOW, KV_COLS), w["swa_sinks"][j])
            new_s["swa_k"].append(jnp.concatenate([ck[:, 1:], k_new[:, None]], axis=1))
            new_s["swa_v"].append(jnp.concatenate([cv[:, 1:], v_new[:, None]], axis=1))
            x, xs = out_proj(o, os_.reshape(1, ns, N_HEADS * HEAD_DIM), w["swa_w_o"], j, w["swa_b_o"][j], x, xs)
        else:
            n = 2 * HG_HEADS * HG_DK + 2 * HG_HEADS * HG_DV
            qzvg, qzvgs = norm_matmul(x, xs, g_mix, w["hg_w_in"], j, zeros(n), tm=tm, tn=_largest_tile(n, tn_in))
            y, s_new = hgrn_seq(qzvg, w["hg_lb_logits"], w["hg_norm"][j], layer=layer, batch=batch,
                                seq_len=seq_len, tt=512, heads_per_step=4)
            new_p["hgrn"].append(s_new)
            ys, ss_new = hgrn_step(qzvgs[0].reshape(ns, 4 * HG_HEADS, HG_DK), states["hgrn"][j], w["hg_lb_logits"],
                                   w["hg_norm"][j], layer=layer)
            new_s["hgrn"].append(ss_new)
            x, xs = out_proj(y, ys.reshape(1, ns, HG_HEADS * HG_DV), w["hg_w_o"], j, zeros(D_MODEL), x, xs)

        fbuf = states["ffn_conv"][layer]
        act, act_s, tail, gate_s = ffn_in(x, xs, w["norm_ffn"][layer], w["ffn_w_in"], w["ffn_conv_w"],
                                          w["ffn_conv_b"], layer, jnp.swapaxes(fbuf, 0, 1), tm=tm, tf=tf)
        new_p["ffn_conv"].append(tail[:, SUBLANES - (CONV_F - 1):])
        new_s["ffn_conv"].append(jnp.concatenate([fbuf[:, 1:], gate_s[0][:, None]], axis=1))
        x, xs = out_proj(act, act_s, w["ffn_w_out"], layer, zeros(D_MODEL), x, xs)

    y = final_norm(x, w["norm_final"], tm=1024).reshape(batch, seq_len, D_MODEL)
    ys = final_norm(xs[0], w["norm_final"], tm=ns).reshape(ns, 1, D_MODEL)
    order = ("lru_h", "lru_conv", "swa_k", "swa_v", "hgrn", "ffn_conv")
    return ((y, ys) + tuple(jnp.stack(new_p[k]) for k in order) + tuple(jnp.stack(new_s[k]) for k in order))


def kernel(x_prompt, x_sample, state_lru_h, state_lru_conv, cache_swa_k, cache_swa_v, state_hgrn, state_ffn_conv,
           norm_mix, norm_ffn, norm_final,
           lru_w_in, lru_conv_w, lru_conv_b, lru_w_a, lru_b_a, lru_w_i, lru_b_i, lru_lambda, lru_w_out,
           swa_w_qkv, swa_b_qkv, swa_sinks, swa_w_o, swa_b_o,
           hg_w_in, hg_lb_logits, hg_norm, hg_w_o,
           ffn_w_in, ffn_conv_w, ffn_conv_b, ffn_w_out):
    w = dict(norm_mix=norm_mix, norm_ffn=norm_ffn, norm_final=norm_final,
             lru_w_in=lru_w_in, lru_conv_w=lru_conv_w, lru_conv_b=lru_conv_b, lru_w_a=lru_w_a, lru_b_a=lru_b_a,
             lru_w_i=lru_w_i, lru_b_i=lru_b_i, lru_lambda=lru_lambda, lru_w_out=lru_w_out,
             swa_w_qkv=swa_w_qkv, swa_b_qkv=swa_b_qkv, swa_sinks=swa_sinks, swa_w_o=swa_w_o, swa_b_o=swa_b_o,
             hg_w_in=hg_w_in, hg_lb_logits=hg_lb_logits, hg_norm=hg_norm, hg_w_o=hg_w_o,
             ffn_w_in=ffn_w_in, ffn_conv_w=ffn_conv_w, ffn_conv_b=ffn_conv_b, ffn_w_out=ffn_w_out)
    states = dict(lru_h=state_lru_h, lru_conv=state_lru_conv, swa_k=cache_swa_k, swa_v=cache_swa_v,
                  hgrn=state_hgrn, ffn_conv=state_ffn_conv)
    return _trunk(x_prompt, x_sample, states, w)
```

```python
import functools

import numpy as np
import jax
import jax.numpy as jnp
from jax import lax
from jax.experimental import pallas as pl
from jax.experimental.pallas import tpu as pltpu

F32 = jnp.float32
BF16 = jnp.bfloat16

D_MODEL = 2048
DEPTH = 4
PAST_LEN = 16384
EPS = 1e-6
LAYER_MIXER = tuple(i % 3 for i in range(DEPTH))
LAYER_SLOT = tuple(LAYER_MIXER[:i].count(LAYER_MIXER[i]) for i in range(DEPTH))

D_RNN = 2560
LRU_BLOCKS = 16
LRU_BLOCK = D_RNN // LRU_BLOCKS
LRU_GROUP = 640
N_LRU_GROUPS = D_RNN // LRU_GROUP
CONV_A = 4
LRU_C = 8.0

N_HEADS = 32
N_KV = 4
HEAD_DIM = 64
GROUP = N_HEADS // N_KV
WINDOW = 128
ROT_DIM = HEAD_DIM // 4
ROPE_THETA = 500000.0
QK_COLS = (N_HEADS + N_KV) * HEAD_DIM
KV_COLS = N_KV * HEAD_DIM

HG_HEADS = 16
HG_DK = 128
HG_DV = 128
HG_CHUNK = 64
HG_LEVELS = 6

D_FF = 3 * D_MODEL
CONV_F = 3

LANES = 128
SUBLANES = 8
V7X_VMEM_LIMIT = 56 * 1024 * 1024
SINGLE_BUFFER_BYTES = 14 * 1024 * 1024
W_TILE_BYTES = 8 * 1024 * 1024
DOT_ROWS = 256


def _params(n_axes, vmem=V7X_VMEM_LIMIT):
    return pltpu.CompilerParams(dimension_semantics=("arbitrary",) * n_axes, vmem_limit_bytes=vmem)


def _sigmoid(x):
    return 1.0 / (1.0 + jnp.exp(-x))


def _rms_rows(x, g):
    var = jnp.mean(x * x, axis=-1, keepdims=True)
    return x * lax.rsqrt(var + EPS) * g


def _dot(a, b):
    return jnp.dot(a, b, preferred_element_type=F32)


def _dot_nt(a, b):
    return lax.dot_general(a, b, (((1,), (1,)), ((), ())), preferred_element_type=F32)


def _dot_tn(a, b):
    return lax.dot_general(a, b, (((0,), (0,)), ((), ())), preferred_element_type=F32)


def _largest_tile(n, cap):
    t = cap
    while n % t:
        t -= LANES
    return t


def _row_tile_spec(shape, index_map, dtype):
    if shape[0] * shape[1] * jnp.dtype(dtype).itemsize >= SINGLE_BUFFER_BYTES:
        return pl.BlockSpec(shape, index_map, pipeline_mode=pl.Buffered(1))
    return pl.BlockSpec(shape, index_map)


def _row_blocks(n_rows):
    step = min(n_rows, DOT_ROWS)
    return [slice(r, r + step) for r in range(0, n_rows, step)]


def _norm_matmul_kernel(x_ref, xs_ref, g_ref, w_ref, b_ref, o_ref, os_ref, hn_ref):
    tm, ns = x_ref.shape[0], xs_ref.shape[0]

    @pl.when(pl.program_id(1) == 0)
    def _():
        for rows in _row_blocks(tm):
            hn_ref[rows, :] = _rms_rows(x_ref[rows, :], g_ref[...]).astype(BF16)
        hn_ref[tm:, :] = _rms_rows(xs_ref[...], g_ref[...]).astype(BF16)

    w = w_ref[...].astype(BF16)
    blocks = _row_blocks(tm)
    for rows in blocks[:-1]:
        o_ref[rows, :] = _dot(hn_ref[rows, :], w) + b_ref[...]
    last = blocks[-1]
    res = _dot(hn_ref[last.start:, :], w) + b_ref[...]
    o_ref[last, :] = res[:tm - last.start]
    os_ref[...] = res[tm - last.start:]


def norm_matmul(x, xs, g, w_stack, li, b, *, tm, tn):
    m, k = x.shape
    ns = xs.shape[1]
    n = w_stack.shape[2]
    return pl.pallas_call(
        _norm_matmul_kernel,
        grid=(m // tm, n // tn),
        in_specs=[
            _row_tile_spec((tm, k), lambda i, j: (i, 0), F32),
            pl.BlockSpec((None, ns, k), lambda i, j: (0, 0, 0)),
            pl.BlockSpec((1, k), lambda i, j: (0, 0)),
            pl.BlockSpec((None, k, tn), lambda i, j: (li, 0, j)),
            pl.BlockSpec((1, tn), lambda i, j: (0, j)),
        ],
        out_specs=[pl.BlockSpec((tm, tn), lambda i, j: (i, j)),
                   pl.BlockSpec((None, ns, tn), lambda i, j: (i, 0, j))],
        out_shape=[jax.ShapeDtypeStruct((m, n), F32), jax.ShapeDtypeStruct((m // tm, ns, n), F32)],
        scratch_shapes=[pltpu.VMEM((tm + ns, k), BF16)],
        compiler_params=_params(2),
        name="norm_matmul",
    )(x, xs, g.reshape(1, k), w_stack, b.reshape(1, n))


def _matmul_residual_kernel(a_ref, as_ref, w_ref, b_ref, x_ref, xs_ref, o_ref, os_ref):
    tm = a_ref.shape[0]
    w = w_ref[...].astype(BF16)
    blocks = _row_blocks(tm)
    for rows in blocks[:-1]:
        o_ref[rows, :] = x_ref[rows, :] + (_dot(a_ref[rows, :], w) + b_ref[...])
    last = blocks[-1]
    lhs = jnp.concatenate([a_ref[last, :], as_ref[...]], axis=0)
    res = _dot(lhs, w) + b_ref[...]
    o_ref[last, :] = x_ref[last, :] + res[:tm - last.start]
    os_ref[...] = xs_ref[...] + res[tm - last.start:]


def matmul_residual(a, a_s, w_stack, li, b, x, xs, *, tm, tn, k_blocks=1, k_index=0):
    m = a.shape[0]
    k = a.shape[1] // k_blocks
    ns = xs.shape[1]
    n = w_stack.shape[2]
    return pl.pallas_call(
        _matmul_residual_kernel,
        grid=(m // tm, n // tn),
        in_specs=[
            _row_tile_spec((tm, k), lambda i, j: (i, k_index), a.dtype),
            pl.BlockSpec((None, ns, k), lambda i, j: (0, 0, k_index)),
            pl.BlockSpec((None, k, tn), lambda i, j: (li, k_index, j)),
            pl.BlockSpec((1, tn), lambda i, j: (0, j)),
            pl.BlockSpec((tm, tn), lambda i, j: (i, j)),
            pl.BlockSpec((None, ns, tn), lambda i, j: (0, 0, j)),
        ],
        out_specs=[pl.BlockSpec((tm, tn), lambda i, j: (i, j)),
                   pl.BlockSpec((None, ns, tn), lambda i, j: (i, 0, j))],
        out_shape=[jax.ShapeDtypeStruct((m, n), F32), jax.ShapeDtypeStruct((m // tm, ns, n), F32)],
        compiler_params=_params(2),
        name="matmul_residual",
    )(a, a_s, w_stack, b.reshape(1, n), x, xs)


def _norm_kernel(x_ref, g_ref, o_ref):
    o_ref[...] = _rms_rows(x_ref[...], g_ref[...])


def final_norm(x, g, *, tm):
    m, k = x.shape
    return pl.pallas_call(
        _norm_kernel,
        grid=(m // tm,),
        in_specs=[pl.BlockSpec((tm, k), lambda i: (i, 0)), pl.BlockSpec((1, k), lambda i: (0, 0))],
        out_specs=pl.BlockSpec((tm, k), lambda i: (i, 0)),
        out_shape=jax.ShapeDtypeStruct((m, k), F32),
        compiler_params=_params(1),
        name="final_norm",
    )(x, g.reshape(1, k))


def _ffn_in_kernel(x_ref, xs_ref, g_ref, wg_ref, wu_ref, cw_ref, cb_ref, prev_ref,
                   act_ref, acts_ref, gt_ref, gts_ref, hn_ref, ext_ref):
    tm = x_ref.shape[0]

    @pl.when(pl.program_id(1) == 0)
    def _():
        for rows in _row_blocks(tm):
            hn_ref[rows, :] = _rms_rows(x_ref[rows, :], g_ref[...]).astype(BF16)
        hn_ref[tm:, :] = _rms_rows(xs_ref[...], g_ref[...]).astype(BF16)
        ext_ref[0:SUBLANES, :] = jnp.zeros((SUBLANES, ext_ref.shape[1]), F32)

    wg = wg_ref[...].astype(BF16)
    wu = wu_ref[...].astype(BF16)
    cw = cw_ref[...]

    def activation(gate, up, prev1, prev2):
        conv = cb_ref[...] + gate * cw[2:3] + prev2 * cw[0:1] + prev1 * cw[1:2]
        return (jax.nn.gelu(conv) * up).astype(BF16)

    def finish(rows, gate, up):
        r0, n = rows.start, rows.stop - rows.start
        ext_ref[SUBLANES + r0:SUBLANES + r0 + n, :] = gate[:n]
        prev1 = ext_ref[SUBLANES - 1 + r0:SUBLANES - 1 + r0 + n, :]
        prev2 = ext_ref[SUBLANES - 2 + r0:SUBLANES - 2 + r0 + n, :]
        act_ref[rows, :] = activation(gate[:n], up[:n], prev1, prev2)
        if rows.stop == tm:
            gt_ref[...] = gate[n - SUBLANES:n, :]
            gts_ref[...] = gate[n:, :]
            acts_ref[...] = activation(gate[n:], up[n:], prev_ref[1], prev_ref[0])

    pending = None
    for rows in _row_blocks(tm):
        hn = hn_ref[rows.start:, :] if rows.stop == tm else hn_ref[rows, :]
        dots = (rows, _dot(hn, wg), _dot(hn, wu))
        if pending is not None:
            finish(*pending)
        pending = dots
    finish(*pending)


def ffn_in(x, xs, g, w_in, conv_w, conv_b, li, prev, *, tm, tf):
    m, k = x.shape
    ns = xs.shape[1]
    nf = D_FF // tf
    n_i = m // tm
    act, act_s, tail, gate_s = pl.pallas_call(
        _ffn_in_kernel,
        grid=(n_i, nf),
        in_specs=[
            _row_tile_spec((tm, k), lambda i, f: (i, 0), F32),
            pl.BlockSpec((None, ns, k), lambda i, f: (0, 0, 0)),
            pl.BlockSpec((1, k), lambda i, f: (0, 0)),
            pl.BlockSpec((None, k, tf), lambda i, f: (li, 0, f)),
            pl.BlockSpec((None, k, tf), lambda i, f: (li, 0, nf + f)),
            pl.BlockSpec((None, CONV_F, tf), lambda i, f: (li, 0, f)),
            pl.BlockSpec((None, 1, tf), lambda i, f: (li, 0, f)),
            pl.BlockSpec((2, ns, tf), lambda i, f: (0, 0, f)),
        ],
        out_specs=[
            pl.BlockSpec((tm, tf), lambda i, f: (i, f)),
            pl.BlockSpec((None, ns, tf), lambda i, f: (i, 0, f)),
            pl.BlockSpec((None, SUBLANES, tf), lambda i, f: (i, 0, f)),
            pl.BlockSpec((None, ns, tf), lambda i, f: (i, 0, f)),
        ],
        out_shape=[jax.ShapeDtypeStruct((m, D_FF), BF16), jax.ShapeDtypeStruct((n_i, ns, D_FF), BF16),
                   jax.ShapeDtypeStruct((n_i, SUBLANES, D_FF), F32), jax.ShapeDtypeStruct((n_i, ns, D_FF), F32)],
        scratch_shapes=[pltpu.VMEM((tm + ns, k), BF16), pltpu.VMEM((tm + SUBLANES, tf), F32)],
        compiler_params=_params(2),
        name="ffn_in",
    )(x, xs, g.reshape(1, k), w_in, w_in, conv_w, conv_b.reshape(DEPTH, 1, D_FF), prev)
    return act, act_s, tail, gate_s


def _lru_gates(xc, wg_ref, ba, bi, lam):
    lam_abs = jnp.abs(lam)
    softplus_neg = jnp.maximum(-lam, 0.0) + jnp.log(1.0 + jnp.exp(-lam_abs))
    a_parts, u_parts = [], []
    for gi in range(N_LRU_GROUPS):
        cols = slice(gi * LRU_GROUP, (gi + 1) * LRU_GROUP)
        xg = xc[:, cols]
        proj = _dot(xg.astype(BF16), wg_ref[gi])
        r = _sigmoid(proj[:, :LRU_GROUP] + ba[:, cols])
        i = _sigmoid(proj[:, LRU_GROUP:] + bi[:, cols])
        log_a = -LRU_C * r * softplus_neg[:, cols]
        a = jnp.exp(log_a)
        a_parts.append(a)
        u_parts.append(jnp.sqrt(1.0 - a * a) * (i * xg))
    return a_parts, u_parts


def _lru_seq_kernel(gate_ref, xr_ref, cw_ref, cb_ref, wg_ref, ba_ref, bi_ref, lam_ref,
                    y_ref, hlast_ref, ext_ref, a_ref, u_ref, carry_ref):
    tt = xr_ref.shape[0]
    n_grp = tt // SUBLANES

    @pl.when(pl.program_id(1) == 0)
    def _():
        ext_ref[0:SUBLANES, :] = jnp.zeros((SUBLANES, D_RNN), F32)
        carry_ref[...] = jnp.zeros((1, D_RNN), F32)

    xr = xr_ref[...]
    ext_ref[SUBLANES:, :] = xr
    cw = cw_ref[...]
    xc = cb_ref[...] + xr * cw[CONV_A - 1:CONV_A]
    for j in range(CONV_A - 1):
        off = SUBLANES - (CONV_A - 1) + j
        xc = xc + ext_ref[off:off + tt, :] * cw[j:j + 1]
    ext_ref[0:SUBLANES, :] = xr[tt - SUBLANES:, :]

    a_parts, u_parts = _lru_gates(xc, wg_ref, ba_ref[...], bi_ref[...], lam_ref[...])
    for gi in range(N_LRU_GROUPS):
        cols = slice(gi * LRU_GROUP, (gi + 1) * LRU_GROUP)
        a_ref[:, :, cols] = a_parts[gi].reshape(n_grp, SUBLANES, LRU_GROUP)
        u_ref[:, :, cols] = u_parts[gi].reshape(n_grp, SUBLANES, LRU_GROUP)

    sub = lax.broadcasted_iota(jnp.int32, (n_grp, SUBLANES, LANES), 1)
    for ci in range(D_RNN // LANES):
        cols = slice(ci * LANES, (ci + 1) * LANES)
        a3 = a_ref[:, :, cols]
        u3 = u_ref[:, :, cols]
        d = 1
        while d < SUBLANES:
            keep = sub >= d
            a_sh = jnp.where(keep, pltpu.roll(a3, d, 1), 1.0)
            u_sh = jnp.where(keep, pltpu.roll(u3, d, 1), 0.0)
            u3 = a3 * u_sh + u3
            a3 = a3 * a_sh
            d *= 2
        h_prev = jnp.broadcast_to(carry_ref[:, cols], (SUBLANES, LANES))
        for g in range(0, n_grp, 2):
            h0 = a3[g] * h_prev + u3[g]
            h_prev = jnp.broadcast_to(h0[SUBLANES - 1:, :], (SUBLANES, LANES))
            h1 = a3[g + 1] * h_prev + u3[g + 1]
            h_prev = jnp.broadcast_to(h1[SUBLANES - 1:, :], (SUBLANES, LANES))
            rows = slice(g * SUBLANES, (g + 2) * SUBLANES)
            h = jnp.concatenate([h0, h1], axis=0)
            y_ref[rows, cols] = (jax.nn.gelu(gate_ref[rows, cols]) * h).astype(BF16)
        carry_ref[:, cols] = h_prev[0:1, :]
        hlast_ref[:, cols] = h_prev[0:1, :]


def lru_seq(gx, conv_w, conv_b, wg, b_a, b_i, lam, *, batch, seq_len, tt):
    m = gx.shape[0]
    nt = seq_len // tt
    vec = lambda: pl.BlockSpec((1, D_RNN), lambda b, t: (0, 0))
    y, h_last = pl.pallas_call(
        _lru_seq_kernel,
        grid=(batch, nt),
        in_specs=[
            pl.BlockSpec((tt, D_RNN), lambda b, t: (b * nt + t, 0)),
            pl.BlockSpec((tt, D_RNN), lambda b, t: (b * nt + t, 1)),
            pl.BlockSpec((CONV_A, D_RNN), lambda b, t: (0, 0)),
            vec(),
            pl.BlockSpec((N_LRU_GROUPS, LRU_GROUP, 2 * LRU_GROUP), lambda b, t: (0, 0, 0)),
            vec(), vec(), vec(),
        ],
        out_specs=[
            pl.BlockSpec((tt, D_RNN), lambda b, t: (b * nt + t, 0)),
            pl.BlockSpec((None, 1, D_RNN), lambda b, t: (b, 0, 0)),
        ],
        out_shape=[jax.ShapeDtypeStruct((m, D_RNN), BF16), jax.ShapeDtypeStruct((batch, 1, D_RNN), F32)],
        scratch_shapes=[pltpu.VMEM((tt + SUBLANES, D_RNN), F32), pltpu.VMEM((tt // SUBLANES, SUBLANES, D_RNN), F32),
                        pltpu.VMEM((tt // SUBLANES, SUBLANES, D_RNN), F32), pltpu.VMEM((1, D_RNN), F32)],
        compiler_params=_params(2),
        name="lru_seq",
    )(gx, gx, conv_w, conv_b.reshape(1, D_RNN), wg, b_a.reshape(1, D_RNN), b_i.reshape(1, D_RNN),
      lam.reshape(1, D_RNN))
    return y, h_last.reshape(batch, D_RNN)


def _lru_step_kernel(gx_ref, h0_ref, cbuf_ref, cw_ref, cb_ref, wg_ref, ba_ref, bi_ref, lam_ref, y_ref, h_ref):
    gate = gx_ref[:, :D_RNN]
    xr = gx_ref[:, D_RNN:]
    cw = cw_ref[...]
    xc = cb_ref[...] + xr * cw[CONV_A - 1:CONV_A]
    for j in range(CONV_A - 1):
        xc = xc + cbuf_ref[j] * cw[j:j + 1]
    a_parts, u_parts = _lru_gates(xc, wg_ref, ba_ref[...], bi_ref[...], lam_ref[...])
    a = jnp.concatenate(a_parts, axis=1)
    u = jnp.concatenate(u_parts, axis=1)
    h = u + a * h0_ref[...]
    h_ref[...] = h
    y_ref[...] = (jax.nn.gelu(gate) * h).astype(BF16)


def lru_step(gx, h0, cbuf, conv_w, conv_b, wg, b_a, b_i, lam):
    m = gx.shape[0]
    return pl.pallas_call(
        _lru_step_kernel,
        out_shape=[jax.ShapeDtypeStruct((m, D_RNN), BF16), jax.ShapeDtypeStruct((m, D_RNN), F32)],
        compiler_params=pltpu.CompilerParams(vmem_limit_bytes=V7X_VMEM_LIMIT),
        name="lru_step",
    )(gx, h0, cbuf, conv_w, conv_b.reshape(1, D_RNN), wg, b_a.reshape(1, D_RNN), b_i.reshape(1, D_RNN),
      lam.reshape(1, D_RNN))


def _lru_gate_weights(w_a, w_i):
    per = LRU_GROUP // LRU_BLOCK
    row_blk = np.arange(LRU_GROUP)[:, None] // LRU_BLOCK
    col_blk = np.arange(2 * LRU_GROUP)[None, :] % LRU_GROUP // LRU_BLOCK
    on_diagonal = jnp.asarray(row_blk == col_blk)
    rows = lambda w: w.astype(BF16).reshape(N_LRU_GROUPS, LRU_GROUP, LRU_BLOCK)
    tiled = jnp.concatenate([jnp.tile(rows(w_a), (1, 1, per)), jnp.tile(rows(w_i), (1, 1, per))], axis=2)
    return jnp.where(on_diagonal, tiled, jnp.zeros((), BF16))


def _rope_tables(pos):
    half = ROT_DIM // 2
    inv = ROPE_THETA ** (-jnp.arange(half, dtype=F32) * (2.0 / ROT_DIM))
    ang = pos[:, None] * inv[None, :]
    cos, sin = jnp.cos(ang), jnp.sin(ang)
    rows = pos.shape[0]
    ones = jnp.ones((rows, HEAD_DIM - ROT_DIM), F32)
    zeros_h = jnp.zeros((rows, half), F32)
    zeros_t = jnp.zeros((rows, HEAD_DIM - ROT_DIM), F32)
    c = jnp.concatenate([cos, cos, ones], axis=1)
    s_lo = jnp.concatenate([-sin, zeros_h, zeros_t], axis=1)
    s_hi = jnp.concatenate([zeros_h, sin, zeros_t], axis=1)
    rep = LANES // HEAD_DIM
    return tuple(jnp.tile(t, (1, rep)) for t in (c, s_lo, s_hi))


def _rope_lanes(x, tables):
    c, s_lo, s_hi = tables
    half = ROT_DIM // 2
    return x * c + pltpu.roll(x, LANES - half, 1) * s_lo + pltpu.roll(x, half, 1) * s_hi


def _rope_kernel(qk_ref, c_ref, slo_ref, shi_ref, q_ref, k_ref):
    tables = (c_ref[...], slo_ref[...], shi_ref[...])
    n_q = N_HEADS * HEAD_DIM // LANES
    for ci in range(QK_COLS // LANES):
        rot = _rope_lanes(qk_ref[:, ci * LANES:(ci + 1) * LANES], tables)
        if ci < n_q:
            q_ref[:, ci * LANES:(ci + 1) * LANES] = rot.astype(BF16)
        else:
            k_ref[:, (ci - n_q) * LANES:(ci - n_q + 1) * LANES] = rot


def rope(qkv, tables, *, tm):
    m = qkv.shape[0]
    nt = tables[0].shape[0] // tm
    tab = lambda: pl.BlockSpec((tm, LANES), lambda i: (i % nt, 0))
    return pl.pallas_call(
        _rope_kernel,
        grid=(m // tm,),
        in_specs=[pl.BlockSpec((tm, QK_COLS), lambda i: (i, 0)), tab(), tab(), tab()],
        out_specs=[pl.BlockSpec((tm, N_HEADS * HEAD_DIM), lambda i: (i, 0)),
                   pl.BlockSpec((tm, KV_COLS), lambda i: (i, 0))],
        out_shape=[jax.ShapeDtypeStruct((m, N_HEADS * HEAD_DIM), BF16), jax.ShapeDtypeStruct((m, KV_COLS), F32)],
        compiler_params=_params(1),
        name="rope",
    )(qkv, *tables)


def _swa_seq_kernel(sink_ref, q_ref, kp_ref, kc_ref, vp_ref, vc_ref, cp_ref, slop_ref, ship_ref,
                    cc_ref, sloc_ref, shic_ref, o_ref, krot_ref):
    qi = pl.program_id(1)
    tq = q_ref.shape[0]
    nk = 2 * tq
    rope_prev = (cp_ref[...], slop_ref[...], ship_ref[...])
    rope_cur = (cc_ref[...], sloc_ref[...], shic_ref[...])
    lane_blocks = [slice(i * LANES, (i + 1) * LANES) for i in range(KV_COLS // LANES)]
    k_prev = jnp.concatenate([_rope_lanes(kp_ref[:, blk], rope_prev) for blk in lane_blocks], axis=1)
    k_cur = jnp.concatenate([_rope_lanes(kc_ref[:, blk], rope_cur) for blk in lane_blocks], axis=1)
    krot_ref[...] = k_cur
    kk = jnp.concatenate([k_prev, k_cur], axis=0) * (HEAD_DIM ** -0.5)
    vv = jnp.concatenate([vp_ref[...], vc_ref[...]], axis=0).astype(BF16)
    key = lax.broadcasted_iota(jnp.int32, (nk, tq), 0)
    qry = lax.broadcasted_iota(jnp.int32, (nk, tq), 1)
    rel = tq + qry - key
    valid = (rel >= 0) & (rel < WINDOW) & ((qi > 0) | (key >= tq))
    low_k = lax.broadcasted_iota(jnp.int32, (nk, LANES), 1) < HEAD_DIM
    low_o = lax.broadcasted_iota(jnp.int32, (tq, LANES), 1) < HEAD_DIM
    for kh in range(N_KV):
        blk = slice((kh // 2) * LANES, (kh // 2 + 1) * LANES)
        in_low = kh % 2 == 0
        k_blk = kk[:, blk]
        k_swap = pltpu.roll(k_blk, HEAD_DIM, 1)
        k_lo = jnp.where(low_k, k_blk if in_low else k_swap, 0.0).astype(BF16)
        k_hi = jnp.where(low_k, 0.0, k_swap if in_low else k_blk).astype(BF16)
        v_blk = vv[:, blk]
        for hp in range(GROUP // 2):
            h0 = kh * GROUP + 2 * hp
            hcols = slice(h0 * HEAD_DIM, (h0 + 2) * HEAD_DIM)
            q_pair = _rope_lanes(q_ref[:, hcols], rope_cur).astype(BF16)
            outs = []
            for which, k_pad in enumerate((k_lo, k_hi)):
                s = jnp.where(valid, _dot_nt(k_pad, q_pair), -jnp.inf)
                sink = sink_ref[h0 + which]
                mx = jnp.maximum(jnp.max(s, axis=0, keepdims=True), sink)
                e = jnp.exp(s - mx)
                denom = jnp.sum(e, axis=0, keepdims=True) + jnp.exp(sink - mx)
                p = (e * (1.0 / denom)).astype(BF16)
                outs.append(_dot_tn(p, v_blk))
            if in_low:
                o_pair = jnp.where(low_o, outs[0], pltpu.roll(outs[1], HEAD_DIM, 1))
            else:
                o_pair = jnp.where(low_o, pltpu.roll(outs[0], HEAD_DIM, 1), outs[1])
            o_ref[:, hcols] = o_pair.astype(BF16)


def swa_seq(qkv, rope_tables, sinks, *, batch, seq_len):
    m = qkv.shape[0]
    tq = WINDOW
    nq = seq_len // tq
    k_col = N_HEADS * HEAD_DIM // KV_COLS
    v_col = QK_COLS // KV_COLS
    prev = lambda col: pl.BlockSpec((tq, KV_COLS), lambda b, i: (b * nq + jnp.maximum(i - 1, 0), col))
    cur = lambda col: pl.BlockSpec((tq, KV_COLS), lambda b, i: (b * nq + i, col))
    tab_prev = lambda: pl.BlockSpec((tq, LANES), lambda b, i: (jnp.maximum(i - 1, 0), 0))
    tab_cur = lambda: pl.BlockSpec((tq, LANES), lambda b, i: (i, 0))
    return pl.pallas_call(
        _swa_seq_kernel,
        grid=(batch, nq),
        in_specs=[
            pl.BlockSpec(memory_space=pltpu.SMEM),
            pl.BlockSpec((tq, N_HEADS * HEAD_DIM), lambda b, i: (b * nq + i, 0)),
            prev(k_col), cur(k_col), prev(v_col), cur(v_col),
            tab_prev(), tab_prev(), tab_prev(), tab_cur(), tab_cur(), tab_cur(),
        ],
        out_specs=[pl.BlockSpec((tq, N_HEADS * HEAD_DIM), lambda b, i: (b * nq + i, 0)),
                   pl.BlockSpec((tq, KV_COLS), lambda b, i: (b * nq + i, 0))],
        out_shape=[jax.ShapeDtypeStruct((m, N_HEADS * HEAD_DIM), BF16), jax.ShapeDtypeStruct((m, KV_COLS), F32)],
        compiler_params=_params(2),
        name="swa_seq",
    )(sinks, qkv, qkv, qkv, qkv, qkv, *rope_tables, *rope_tables)


def _swa_step_kernel(q_ref, kn_ref, vn_ref, ck_ref, cv_ref, sink_ref, o_ref):
    col = lax.broadcasted_iota(jnp.int32, (GROUP, WINDOW), 1)
    scale = HEAD_DIM ** -0.5
    items = [(bi, kh) for bi in range(q_ref.shape[0]) for kh in range(N_KV)]
    heads = lambda kh: slice(kh * GROUP, (kh + 1) * GROUP)
    kcols = lambda kh: slice(kh * HEAD_DIM, (kh + 1) * HEAD_DIM)
    scores = []
    for bi, kh in items:
        q = q_ref[bi, heads(kh), :]
        k_new = kn_ref[bi, kh:kh + 1, :].astype(BF16).astype(F32)
        s_c = _dot_nt(q, ck_ref[bi, :, kcols(kh)].astype(BF16)) * scale
        s_c = jnp.where(col >= 1, s_c, -jnp.inf)
        s_n = jnp.sum(q.astype(F32) * k_new, axis=-1, keepdims=True) * scale
        scores.append((s_c, s_n))
    probs = []
    for (bi, kh), (s_c, s_n) in zip(items, scores):
        sink = sink_ref[heads(kh), :]
        mx = jnp.maximum(jnp.maximum(jnp.max(s_c, axis=-1, keepdims=True), s_n), sink)
        p_c = jnp.exp(s_c - mx)
        p_n = jnp.exp(s_n - mx)
        denom = jnp.sum(p_c, axis=-1, keepdims=True) + p_n + jnp.exp(sink - mx)
        probs.append((p_c.astype(BF16), p_n.astype(BF16).astype(F32), denom))
    for (bi, kh), (p_c, p_n, denom) in zip(items, probs):
        v_new = vn_ref[bi, kh:kh + 1, :].astype(BF16).astype(F32)
        o = _dot(p_c, cv_ref[bi, :, kcols(kh)].astype(BF16)) + p_n * v_new
        o_ref[bi, heads(kh), :] = (o / denom).astype(BF16)


def swa_step(q3, k_new, v_new, cache_k, cache_v, sinks, *, seqs_per_step=8):
    b = q3.shape[0]
    nb = seqs_per_step
    return pl.pallas_call(
        _swa_step_kernel,
        grid=(b // nb,),
        in_specs=[
            pl.BlockSpec((nb, N_HEADS, HEAD_DIM), lambda i: (i, 0, 0)),
            pl.BlockSpec((nb, N_KV, HEAD_DIM), lambda i: (i, 0, 0)),
            pl.BlockSpec((nb, N_KV, HEAD_DIM), lambda i: (i, 0, 0)),
            pl.BlockSpec((nb, WINDOW, KV_COLS), lambda i: (i, 0, 0)),
            pl.BlockSpec((nb, WINDOW, KV_COLS), lambda i: (i, 0, 0)),
            pl.BlockSpec((N_HEADS, 1), lambda i: (0, 0)),
        ],
        out_specs=pl.BlockSpec((nb, N_HEADS, HEAD_DIM), lambda i: (i, 0, 0)),
        out_shape=jax.ShapeDtypeStruct((b, N_HEADS, HEAD_DIM), BF16),
        compiler_params=_params(1),
        name="swa_step",
    )(q3, k_new, v_new, cache_k, cache_v, sinks.reshape(N_HEADS, 1))


def _hgrn_lower_bound(logits, layer):
    mx = jnp.max(logits, axis=0, keepdims=True)
    e = jnp.exp(logits - mx)
    sm = e / jnp.sum(e, axis=0, keepdims=True)
    lb = jnp.zeros_like(sm[0:1])
    for i in range(1, layer + 1):
        lb = lb + sm[i:i + 1]
    return lb


def _hgrn_consts():
    c = HG_CHUNK
    t = np.arange(c)[:, None]
    s = np.arange(c)[None, :]
    tri = (s <= t).astype(np.float32)
    sel = [tri]
    msk = [(s == t)]
    for lvl in range(1, HG_LEVELS + 1):
        w = 1 << (lvl - 1)
        ref_row = (t // (2 * w)) * (2 * w) + w - 1
        upper_t = (t % (2 * w)) >= w
        lower_s = (s % (2 * w)) < w
        sign = np.where(upper_t, 1.0, -1.0)
        sel.append(sign * (tri - (s <= ref_row)))
        msk.append(((t // (2 * w)) == (s // (2 * w))) & upper_t & lower_s)
    sel.append(1.0 - tri)
    sel = np.concatenate(sel, axis=0).astype(np.float32)
    sel2 = np.concatenate([sel, sel], axis=1)
    msk = np.stack(msk, axis=0).astype(np.float32)
    return jnp.asarray(sel2, BF16), jnp.asarray(msk, F32)


def _hgrn_seq_kernel(q_ref, z_ref, v_ref, g_ref, lbl_ref, gn_ref, sel_ref, msk_ref, y_ref, s_out_ref, st_ref,
                     *, layer, heads_per_step):
    ti = pl.program_id(2)
    tt = q_ref.shape[0]
    c = HG_CHUNK

    @pl.when(ti == 0)
    def _():
        st_ref[...] = jnp.zeros(st_ref.shape, F32)

    lb_all = _hgrn_lower_bound(lbl_ref[...], layer)
    gn = gn_ref[...]
    row = lax.broadcasted_iota(jnp.int32, (c, heads_per_step * HG_DK), 0)
    upper = [None] + [((row >> (lvl - 1)) & 1) == 1 for lvl in range(1, HG_LEVELS + 1)]
    states = [st_ref[hh] for hh in range(heads_per_step)]

    head_cols = [slice(hh * HG_DK, (hh + 1) * HG_DK) for hh in range(heads_per_step)]

    def decays(ci):
        rows = slice(ci * c, (ci + 1) * c)
        q = q_ref[rows, :]
        sq = q * _sigmoid(q)
        sz = _sigmoid(z_ref[rows, :])
        log_f = jnp.log(lb_all + (1.0 - lb_all) * sz)
        k = (1.0 - lb_all) * (1.0 - sz)
        hi = log_f.astype(BF16)
        lo = (log_f - hi.astype(F32)).astype(BF16)
        ex = _dot(sel_ref[...], jnp.concatenate([hi, lo], axis=0))
        e_b = jnp.exp(ex[0:c])
        xs = [(jnp.where(upper[lvl], sq, k) * jnp.exp(ex[lvl * c:(lvl + 1) * c])).astype(BF16)
              for lvl in range(1, HG_LEVELS + 1)]
        return dict(rows=rows, v=v_ref[rows, :].astype(BF16), sq=sq.astype(BF16), k=k.astype(BF16), xs=xs,
                    q_dec=(sq * e_b).astype(BF16), k_dec=(k * jnp.exp(ex[(HG_LEVELS + 1) * c:])).astype(BF16),
                    e_last=e_b[c - 1:c, :])

    def scores(d):
        atts = []
        for cols in head_cols:
            att = _dot_nt(d["sq"][:, cols], d["k"][:, cols]) * msk_ref[0]
            for lvl in range(1, HG_LEVELS + 1):
                x = d["xs"][lvl - 1][:, cols]
                att = att + _dot_nt(x, x) * msk_ref[lvl]
            atts.append(att.astype(BF16))
        return atts

    def outputs(d, atts):
        outs = []
        for hh, cols in enumerate(head_cols):
            st = states[hh]
            o = _dot(atts[hh], d["v"][:, cols]) + _dot_nt(d["q_dec"][:, cols], st.astype(BF16))
            states[hh] = st * d["e_last"][:, cols] + _dot_tn(d["v"][:, cols], d["k_dec"][:, cols])
            outs.append(_rms_rows(o, gn))
        g = g_ref[d["rows"], :]
        y_ref[d["rows"], :] = (jnp.concatenate(outs, axis=1) * (g * _sigmoid(g))).astype(BF16)

    n_chunks = tt // c
    stage_a, stage_b = {}, {}
    for step in range(n_chunks + 2):
        if step < n_chunks:
            stage_a[step] = decays(step)
        if 0 <= step - 1 < n_chunks:
            stage_b[step - 1] = scores(stage_a[step - 1])
        if 0 <= step - 2 < n_chunks:
            outputs(stage_a.pop(step - 2), stage_b.pop(step - 2))

    for hh in range(heads_per_step):
        st_ref[hh] = states[hh]

    @pl.when(ti == pl.num_programs(2) - 1)
    def _():
        for hh in range(heads_per_step):
            s_out_ref[hh] = states[hh].T


def hgrn_seq(qzvg, lb_logits, g_norm, *, layer, batch, seq_len, tt, heads_per_step):
    m = qzvg.shape[0]
    nt = seq_len // tt
    hw = heads_per_step * HG_DK
    nh = HG_HEADS // heads_per_step
    sel, msk = _hgrn_consts()
    part = lambda p: pl.BlockSpec((tt, hw), lambda b, h, t: (b * nt + t, p * nh + h))
    y, s_out = pl.pallas_call(
        functools.partial(_hgrn_seq_kernel, layer=layer, heads_per_step=heads_per_step),
        grid=(batch, nh, nt),
        in_specs=[
            part(0), part(1), part(2), part(3),
            pl.BlockSpec((DEPTH, hw), lambda b, h, t: (0, h)),
            pl.BlockSpec((1, HG_DV), lambda b, h, t: (0, 0)),
            pl.BlockSpec(sel.shape, lambda b, h, t: (0, 0)),
            pl.BlockSpec(msk.shape, lambda b, h, t: (0, 0, 0)),
        ],
        out_specs=[
            pl.BlockSpec((tt, hw), lambda b, h, t: (b * nt + t, h)),
            pl.BlockSpec((None, heads_per_step, HG_DK, HG_DV), lambda b, h, t: (b, h, 0, 0)),
        ],
        out_shape=[jax.ShapeDtypeStruct((m, HG_HEADS * HG_DV), BF16),
                   jax.ShapeDtypeStruct((batch, HG_HEADS, HG_DK, HG_DV), F32)],
        scratch_shapes=[pltpu.VMEM((heads_per_step, HG_DV, HG_DK), F32)],
        compiler_params=_params(3),
        name="hgrn_seq",
    )(qzvg, qzvg, qzvg, qzvg, lb_logits, g_norm.reshape(1, HG_DV), sel, msk)
    return y, s_out


def _hgrn_step_kernel(x_ref, s_ref, lbl_ref, gn_ref, y_ref, s_out_ref, *, layer):
    nh = HG_HEADS
    lb = _hgrn_lower_bound(lbl_ref[...], layer)[0]
    pad = jnp.zeros((LANES - 3 * nh, HG_DK), F32)
    for bi in range(x_ref.shape[0]):
        q = x_ref[bi, 0:nh, :]
        z = x_ref[bi, nh:2 * nh, :]
        v = x_ref[bi, 2 * nh:3 * nh, :]
        g = x_ref[bi, 3 * nh:4 * nh, :]
        sq = q * _sigmoid(q)
        sz = _sigmoid(z)
        f = lb + (1.0 - lb) * sz
        k = (1.0 - lb) * (1.0 - sz)
        cols = jnp.concatenate([f, k, sq, pad], axis=0).T
        outs = []
        for h in range(nh):
            f_col = cols[:, h:h + 1]
            k_col = cols[:, nh + h:nh + h + 1]
            q_col = cols[:, 2 * nh + h:2 * nh + h + 1]
            s_new = s_ref[bi, h] * f_col + k_col * v[h:h + 1, :]
            s_out_ref[bi, h] = s_new
            outs.append(jnp.sum(s_new * q_col, axis=0, keepdims=True))
        o = jnp.concatenate(outs, axis=0)
        y_ref[bi] = (_rms_rows(o, gn_ref[...]) * (g * _sigmoid(g))).astype(BF16)


def hgrn_step(x4, state, lb_logits, g_norm, *, layer, seqs_per_step=4):
    b = x4.shape[0]
    nb = seqs_per_step
    return pl.pallas_call(
        functools.partial(_hgrn_step_kernel, layer=layer),
        grid=(b // nb,),
        in_specs=[
            pl.BlockSpec((nb, 4 * HG_HEADS, HG_DK), lambda i: (i, 0, 0)),
            pl.BlockSpec((nb, HG_HEADS, HG_DK, HG_DV), lambda i: (i, 0, 0, 0)),
            pl.BlockSpec((DEPTH, HG_HEADS, HG_DK), lambda i: (0, 0, 0)),
            pl.BlockSpec((1, HG_DV), lambda i: (0, 0)),
        ],
        out_specs=[
            pl.BlockSpec((nb, HG_HEADS, HG_DV), lambda i: (i, 0, 0)),
            pl.BlockSpec((nb, HG_HEADS, HG_DK, HG_DV), lambda i: (i, 0, 0, 0)),
        ],
        out_shape=[jax.ShapeDtypeStruct((b, HG_HEADS, HG_DV), BF16),
                   jax.ShapeDtypeStruct((b, HG_HEADS, HG_DK, HG_DV), F32)],
        compiler_params=_params(1),
        name="hgrn_step",
    )(x4, state, lb_logits.reshape(DEPTH, HG_HEADS, HG_DK), g_norm.reshape(1, HG_DV))


def _trunk(x3, xs3, states, w):
    batch, seq_len, _ = x3.shape
    m = batch * seq_len
    ns = xs3.shape[0]
    x = x3.reshape(m, D_MODEL)
    xs = xs3.reshape(1, ns, D_MODEL)
    tm = seq_len
    tn_in, tn_out, tf = 512, 256, 512
    zeros = lambda n: jnp.zeros((n,), F32)
    lru_wg = [_lru_gate_weights(w["lru_w_a"][j], w["lru_w_i"][j]) for j in range(w["lru_w_a"].shape[0])]
    rope_seq = _rope_tables(jnp.arange(seq_len, dtype=F32))
    rope_step = _rope_tables(jnp.full((ns,), PAST_LEN, F32))

    def out_proj(a, a_s, w_stack, li, b, res, res_s):
        k_blocks = 1
        while tm * (a.shape[1] // k_blocks) * a.dtype.itemsize >= SINGLE_BUFFER_BYTES:
            k_blocks *= 2
        k = a.shape[1] // k_blocks
        cap = max(LANES, min(tn_out, W_TILE_BYTES // (4 * k) // LANES * LANES))
        for ki in range(k_blocks):
            bias = b if ki == k_blocks - 1 else jnp.zeros_like(b)
            res, res_s = matmul_residual(a, a_s, w_stack, li, bias, res, res_s, tm=tm,
                                         tn=_largest_tile(D_MODEL, cap), k_blocks=k_blocks, k_index=ki)
        return res, res_s

    new_p = {"lru_h": [], "lru_conv": [], "swa_k": [], "swa_v": [], "hgrn": [], "ffn_conv": []}
    new_s = {"lru_h": [], "lru_conv": [], "swa_k": [], "swa_v": [], "hgrn": [], "ffn_conv": []}
    for layer in range(DEPTH):
        kind, j = LAYER_MIXER[layer], LAYER_SLOT[layer]
        g_mix = w["norm_mix"][layer]
        if kind == 0:
            n = 2 * D_RNN
            gx, gxs = norm_matmul(x, xs, g_mix, w["lru_w_in"], j, zeros(n), tm=tm, tn=_largest_tile(n, tn_in))
            gxs = gxs[0]
            lru_w = (w["lru_conv_w"][j], w["lru_conv_b"][j], lru_wg[j], w["lru_b_a"][j], w["lru_b_i"][j],
                     w["lru_lambda"][j])
            y, h_new = lru_seq(gx, *lru_w, batch=batch, seq_len=seq_len, tt=256)
            new_p["lru_h"].append(h_new)
            new_p["lru_conv"].append(gx.reshape(batch, seq_len, n)[:, seq_len - (CONV_A - 1):, D_RNN:])
            cbuf = states["lru_conv"][j]
            ys, hs_new = lru_step(gxs, states["lru_h"][j], jnp.swapaxes(cbuf, 0, 1), *lru_w)
            new_s["lru_h"].append(hs_new)
            new_s["lru_conv"].append(jnp.concatenate([cbuf[:, 1:], gxs[:, None, D_RNN:]], axis=1))
            x, xs = out_proj(y, ys[None], w["lru_w_out"], j, zeros(D_MODEL), x, xs)
        elif kind == 1:
            n = QK_COLS + KV_COLS
            qkv, qkvs = norm_matmul(x, xs, g_mix, w["swa_w_qkv"], j, w["swa_b_qkv"][j], tm=tm,
                                    tn=_largest_tile(n, tn_in))
            qkvs = qkvs[0]
            o, k_rot = swa_seq(qkv, rope_seq, w["swa_sinks"][j], batch=batch, seq_len=seq_len)
            k_win = k_rot.reshape(batch, seq_len, KV_COLS)[:, seq_len - WINDOW:]
            v_win = qkv.reshape(batch, seq_len, n)[:, seq_len - WINDOW:, QK_COLS:]
            new_p["swa_k"].append(k_win.reshape(batch, WINDOW, N_KV, HEAD_DIM))
            new_p["swa_v"].append(v_win.reshape(batch, WINDOW, N_KV, HEAD_DIM))
            qs_rot, ks_rot = rope(qkvs, rope_step, tm=ns)
            ck = states["swa_k"][j]
            cv = states["swa_v"][j]
            k_new = ks_rot.reshape(ns, N_KV, HEAD_DIM)
            v_new = qkvs[:, QK_COLS:].reshape(ns, N_KV, HEAD_DIM)
            os_ = swa_step(qs_rot.reshape(ns, N_HEADS, HEAD_DIM), k_new, v_new,
                           ck.reshape(ns, WINDOW, KV_COLS), cv.reshape(ns, WINDOW, KV_COLS), w["swa_sinks"][j])
            new_s["swa_k"].append(jnp.concatenate([ck[:, 1:], k_new[:, None]], axis=1))
            new_s["swa_v"].append(jnp.concatenate([cv[:, 1:], v_new[:, None]], axis=1))
            x, xs = out_proj(o, os_.reshape(1, ns, N_HEADS * HEAD_DIM), w["swa_w_o"], j, w["swa_b_o"][j], x, xs)
        else:
            n = 2 * HG_HEADS * HG_DK + 2 * HG_HEADS * HG_DV
            qzvg, qzvgs = norm_matmul(x, xs, g_mix, w["hg_w_in"], j, zeros(n), tm=tm, tn=_largest_tile(n, tn_in))
            y, s_new = hgrn_seq(qzvg, w["hg_lb_logits"], w["hg_norm"][j], layer=layer, batch=batch,
                                seq_len=seq_len, tt=1024, heads_per_step=4)
            new_p["hgrn"].append(s_new)
            ys, ss_new = hgrn_step(qzvgs[0].reshape(ns, 4 * HG_HEADS, HG_DK), states["hgrn"][j], w["hg_lb_logits"],
                                   w["hg_norm"][j], layer=layer)
            new_s["hgrn"].append(ss_new)
            x, xs = out_proj(y, ys.reshape(1, ns, HG_HEADS * HG_DV), w["hg_w_o"], j, zeros(D_MODEL), x, xs)

        fbuf = states["ffn_conv"][layer]
        act, act_s, tail, gate_s = ffn_in(x, xs, w["norm_ffn"][layer], w["ffn_w_in"], w["ffn_conv_w"],
                                          w["ffn_conv_b"], layer, jnp.swapaxes(fbuf, 0, 1), tm=tm, tf=tf)
        new_p["ffn_conv"].append(tail[:, SUBLANES - (CONV_F - 1):])
        new_s["ffn_conv"].append(jnp.concatenate([fbuf[:, 1:], gate_s[0][:, None]], axis=1))
        x, xs = out_proj(act, act_s, w["ffn_w_out"], layer, zeros(D_MODEL), x, xs)

    y = final_norm(x, w["norm_final"], tm=1024).reshape(batch, seq_len, D_MODEL)
    ys = final_norm(xs[0], w["norm_final"], tm=ns).reshape(ns, 1, D_MODEL)
    order = ("lru_h", "lru_conv", "swa_k", "swa_v", "hgrn", "ffn_conv")
    return ((y, ys) + tuple(jnp.stack(new_p[k]) for k in order) + tuple(jnp.stack(new_s[k]) for k in order))


def kernel(x_prompt, x_sample, state_lru_h, state_lru_conv, cache_swa_k, cache_swa_v, state_hgrn, state_ffn_conv,
           norm_mix, norm_ffn, norm_final,
           lru_w_in, lru_conv_w, lru_conv_b, lru_w_a, lru_b_a, lru_w_i, lru_b_i, lru_lambda, lru_w_out,
           swa_w_qkv, swa_b_qkv, swa_sinks, swa_w_o, swa_b_o,
           hg_w_in, hg_lb_logits, hg_norm, hg_w_o,
           ffn_w_in, ffn_conv_w, ffn_conv_b, ffn_w_out):
    w = dict(norm_mix=norm_mix, norm_ffn=norm_ffn, norm_final=norm_final,
             lru_w_in=lru_w_in, lru_conv_w=lru_conv_w, lru_conv_b=lru_conv_b, lru_w_a=lru_w_a, lru_b_a=lru_b_a,
             lru_w_i=lru_w_i, lru_b_i=lru_b_i, lru_lambda=lru_lambda, lru_w_out=lru_w_out,
             swa_w_qkv=swa_w_qkv, swa_b_qkv=swa_b_qkv, swa_sinks=swa_sinks, swa_w_o=swa_w_o, swa_b_o=swa_b_o,
             hg_w_in=hg_w_in, hg_lb_logits=hg_lb_logits, hg_norm=hg_norm, hg_w_o=hg_w_o,
             ffn_w_in=ffn_w_in, ffn_conv_w=ffn_conv_w, ffn_conv_b=ffn_conv_b, ffn_w_out=ffn_w_out)
    states = dict(lru_h=state_lru_h, lru_conv=state_lru_conv, swa_k=cache_swa_k, swa_v=cache_swa_v,
                  hgrn=state_hgrn, ffn_conv=state_ffn_conv)
    return _trunk(x_prompt, x_sample, states, w)
```

```python
import functools

import numpy as np
import jax
import jax.numpy as jnp
from jax import lax
from jax.experimental import pallas as pl
from jax.experimental.pallas import tpu as pltpu

F32 = jnp.float32
BF16 = jnp.bfloat16

D_MODEL = 2048
DEPTH = 4
PAST_LEN = 16384
EPS = 1e-6
LAYER_MIXER = tuple(i % 3 for i in range(DEPTH))
LAYER_SLOT = tuple(LAYER_MIXER[:i].count(LAYER_MIXER[i]) for i in range(DEPTH))

D_RNN = 2560
LRU_BLOCKS = 16
LRU_BLOCK = D_RNN // LRU_BLOCKS
LRU_GROUP = 640
N_LRU_GROUPS = D_RNN // LRU_GROUP
CONV_A = 4
LRU_C = 8.0

N_HEADS = 32
N_KV = 4
HEAD_DIM = 64
GROUP = N_HEADS // N_KV
WINDOW = 128
ROT_DIM = HEAD_DIM // 4
ROPE_THETA = 500000.0
QK_COLS = (N_HEADS + N_KV) * HEAD_DIM
KV_COLS = N_KV * HEAD_DIM

HG_HEADS = 16
HG_DK = 128
HG_DV = 128
HG_CHUNK = 64
HG_LEVELS = 6

D_FF = 3 * D_MODEL
CONV_F = 3

LANES = 128
SUBLANES = 8
V7X_VMEM_LIMIT = 56 * 1024 * 1024
SINGLE_BUFFER_BYTES = 14 * 1024 * 1024
W_TILE_BYTES = 8 * 1024 * 1024
DOT_ROWS = 256


def _params(n_axes, vmem=V7X_VMEM_LIMIT):
    return pltpu.CompilerParams(dimension_semantics=("arbitrary",) * n_axes, vmem_limit_bytes=vmem)


def _sigmoid(x):
    return 1.0 / (1.0 + jnp.exp(-x))


def _rms_rows(x, g):
    var = jnp.mean(x * x, axis=-1, keepdims=True)
    return x * lax.rsqrt(var + EPS) * g


def _dot(a, b):
    return jnp.dot(a, b, preferred_element_type=F32)


def _dot_nt(a, b):
    return lax.dot_general(a, b, (((1,), (1,)), ((), ())), preferred_element_type=F32)


def _dot_tn(a, b):
    return lax.dot_general(a, b, (((0,), (0,)), ((), ())), preferred_element_type=F32)


def _largest_tile(n, cap):
    t = cap
    while n % t:
        t -= LANES
    return t


def _row_tile_spec(shape, index_map, dtype):
    if shape[0] * shape[1] * jnp.dtype(dtype).itemsize >= SINGLE_BUFFER_BYTES:
        return pl.BlockSpec(shape, index_map, pipeline_mode=pl.Buffered(1))
    return pl.BlockSpec(shape, index_map)


def _row_blocks(n_rows):
    step = min(n_rows, DOT_ROWS)
    return [slice(r, r + step) for r in range(0, n_rows, step)]


def _norm_matmul_kernel(x_ref, xs_ref, g_ref, w_ref, b_ref, o_ref, os_ref, hn_ref):
    tm, ns = x_ref.shape[0], xs_ref.shape[0]

    @pl.when(pl.program_id(1) == 0)
    def _():
        for rows in _row_blocks(tm):
            hn_ref[rows, :] = _rms_rows(x_ref[rows, :], g_ref[...]).astype(BF16)
        hn_ref[tm:, :] = _rms_rows(xs_ref[...], g_ref[...]).astype(BF16)

    w = w_ref[...].astype(BF16)
    blocks = _row_blocks(tm)
    for rows in blocks[:-1]:
        o_ref[rows, :] = _dot(hn_ref[rows, :], w) + b_ref[...]
    last = blocks[-1]
    res = _dot(hn_ref[last.start:, :], w) + b_ref[...]
    o_ref[last, :] = res[:tm - last.start]
    os_ref[...] = res[tm - last.start:]


def norm_matmul(x, xs, g, w_stack, li, b, *, tm, tn):
    m, k = x.shape
    ns = xs.shape[1]
    n = w_stack.shape[2]
    return pl.pallas_call(
        _norm_matmul_kernel,
        grid=(m // tm, n // tn),
        in_specs=[
            _row_tile_spec((tm, k), lambda i, j: (i, 0), F32),
            pl.BlockSpec((None, ns, k), lambda i, j: (0, 0, 0)),
            pl.BlockSpec((1, k), lambda i, j: (0, 0)),
            pl.BlockSpec((None, k, tn), lambda i, j: (li, 0, j)),
            pl.BlockSpec((1, tn), lambda i, j: (0, j)),
        ],
        out_specs=[pl.BlockSpec((tm, tn), lambda i, j: (i, j)),
                   pl.BlockSpec((None, ns, tn), lambda i, j: (i, 0, j))],
        out_shape=[jax.ShapeDtypeStruct((m, n), F32), jax.ShapeDtypeStruct((m // tm, ns, n), F32)],
        scratch_shapes=[pltpu.VMEM((tm + ns, k), BF16)],
        compiler_params=_params(2),
        name="norm_matmul",
    )(x, xs, g.reshape(1, k), w_stack, b.reshape(1, n))


def _norm_matmul_gated_kernel(x_ref, xs_ref, g_ref, wa_ref, wb_ref, act_ref, acts_ref, lin_ref, lins_ref, hn_ref):
    tm = x_ref.shape[0]

    @pl.when(pl.program_id(1) == 0)
    def _():
        for rows in _row_blocks(tm):
            hn_ref[rows, :] = _rms_rows(x_ref[rows, :], g_ref[...]).astype(BF16)
        hn_ref[tm:, :] = _rms_rows(xs_ref[...], g_ref[...]).astype(BF16)

    wa = wa_ref[...].astype(BF16)
    wb = wb_ref[...].astype(BF16)
    blocks = _row_blocks(tm)
    for rows in blocks[:-1]:
        hn = hn_ref[rows, :]
        act_ref[rows, :] = jax.nn.gelu(_dot(hn, wa)).astype(BF16)
        lin_ref[rows, :] = _dot(hn, wb)
    last = blocks[-1]
    n = tm - last.start
    hn = hn_ref[last.start:, :]
    act = jax.nn.gelu(_dot(hn, wa)).astype(BF16)
    lin = _dot(hn, wb)
    act_ref[last, :] = act[:n]
    acts_ref[...] = act[n:]
    lin_ref[last, :] = lin[:n]
    lins_ref[...] = lin[n:]


def norm_matmul_gated(x, xs, g, w_stack, li, *, tm, tn):
    m, k = x.shape
    ns = xs.shape[1]
    n = w_stack.shape[2] // 2
    nj = n // tn
    n_i = m // tm
    rows_out = lambda: pl.BlockSpec((tm, tn), lambda i, j: (i, j))
    step_out = lambda: pl.BlockSpec((None, ns, tn), lambda i, j: (i, 0, j))
    return pl.pallas_call(
        _norm_matmul_gated_kernel,
        grid=(n_i, nj),
        in_specs=[
            _row_tile_spec((tm, k), lambda i, j: (i, 0), F32),
            pl.BlockSpec((None, ns, k), lambda i, j: (0, 0, 0)),
            pl.BlockSpec((1, k), lambda i, j: (0, 0)),
            pl.BlockSpec((None, k, tn), lambda i, j: (li, 0, j)),
            pl.BlockSpec((None, k, tn), lambda i, j: (li, 0, nj + j)),
        ],
        out_specs=[rows_out(), step_out(), rows_out(), step_out()],
        out_shape=[jax.ShapeDtypeStruct((m, n), BF16), jax.ShapeDtypeStruct((n_i, ns, n), BF16),
                   jax.ShapeDtypeStruct((m, n), F32), jax.ShapeDtypeStruct((n_i, ns, n), F32)],
        scratch_shapes=[pltpu.VMEM((tm + ns, k), BF16)],
        compiler_params=_params(2),
        name="norm_matmul_gated",
    )(x, xs, g.reshape(1, k), w_stack, w_stack)


def _matmul_residual_kernel(a_ref, as_ref, w_ref, b_ref, x_ref, xs_ref, o_ref, os_ref):
    tm = a_ref.shape[0]
    w = w_ref[...].astype(BF16)
    blocks = _row_blocks(tm)
    for rows in blocks[:-1]:
        o_ref[rows, :] = x_ref[rows, :] + (_dot(a_ref[rows, :], w) + b_ref[...])
    last = blocks[-1]
    lhs = jnp.concatenate([a_ref[last, :], as_ref[...]], axis=0)
    res = _dot(lhs, w) + b_ref[...]
    o_ref[last, :] = x_ref[last, :] + res[:tm - last.start]
    os_ref[...] = xs_ref[...] + res[tm - last.start:]


def matmul_residual(a, a_s, w_stack, li, b, x, xs, *, tm, tn, k_blocks=1, k_index=0):
    m = a.shape[0]
    k = a.shape[1] // k_blocks
    ns = xs.shape[1]
    n = w_stack.shape[2]
    return pl.pallas_call(
        _matmul_residual_kernel,
        grid=(m // tm, n // tn),
        in_specs=[
            _row_tile_spec((tm, k), lambda i, j: (i, k_index), a.dtype),
            pl.BlockSpec((None, ns, k), lambda i, j: (0, 0, k_index)),
            pl.BlockSpec((None, k, tn), lambda i, j: (li, k_index, j)),
            pl.BlockSpec((1, tn), lambda i, j: (0, j)),
            pl.BlockSpec((tm, tn), lambda i, j: (i, j)),
            pl.BlockSpec((None, ns, tn), lambda i, j: (0, 0, j)),
        ],
        out_specs=[pl.BlockSpec((tm, tn), lambda i, j: (i, j)),
                   pl.BlockSpec((None, ns, tn), lambda i, j: (i, 0, j))],
        out_shape=[jax.ShapeDtypeStruct((m, n), F32), jax.ShapeDtypeStruct((m // tm, ns, n), F32)],
        compiler_params=_params(2),
        name="matmul_residual",
    )(a, a_s, w_stack, b.reshape(1, n), x, xs)


def _norm_kernel(x_ref, g_ref, o_ref):
    o_ref[...] = _rms_rows(x_ref[...], g_ref[...])


def final_norm(x, g, *, tm):
    m, k = x.shape
    return pl.pallas_call(
        _norm_kernel,
        grid=(m // tm,),
        in_specs=[pl.BlockSpec((tm, k), lambda i: (i, 0)), pl.BlockSpec((1, k), lambda i: (0, 0))],
        out_specs=pl.BlockSpec((tm, k), lambda i: (i, 0)),
        out_shape=jax.ShapeDtypeStruct((m, k), F32),
        compiler_params=_params(1),
        name="final_norm",
    )(x, g.reshape(1, k))


def _ffn_in_kernel(x_ref, xs_ref, g_ref, wg_ref, wu_ref, cw_ref, cb_ref, prev_ref,
                   act_ref, acts_ref, gt_ref, gts_ref, hn_ref, ext_ref):
    tm = x_ref.shape[0]

    @pl.when(pl.program_id(1) == 0)
    def _():
        for rows in _row_blocks(tm):
            hn_ref[rows, :] = _rms_rows(x_ref[rows, :], g_ref[...]).astype(BF16)
        hn_ref[tm:, :] = _rms_rows(xs_ref[...], g_ref[...]).astype(BF16)
        ext_ref[0:SUBLANES, :] = jnp.zeros((SUBLANES, ext_ref.shape[1]), F32)

    wg = wg_ref[...].astype(BF16)
    wu = wu_ref[...].astype(BF16)
    cw = cw_ref[...]

    def activation(gate, up, prev1, prev2):
        conv = cb_ref[...] + gate * cw[2:3] + prev2 * cw[0:1] + prev1 * cw[1:2]
        return (jax.nn.gelu(conv) * up).astype(BF16)

    def finish(rows, gate, up):
        r0, n = rows.start, rows.stop - rows.start
        ext_ref[SUBLANES + r0:SUBLANES + r0 + n, :] = gate[:n]
        prev1 = ext_ref[SUBLANES - 1 + r0:SUBLANES - 1 + r0 + n, :]
        prev2 = ext_ref[SUBLANES - 2 + r0:SUBLANES - 2 + r0 + n, :]
        act_ref[rows, :] = activation(gate[:n], up[:n], prev1, prev2)
        if rows.stop == tm:
            gt_ref[...] = gate[n - SUBLANES:n, :]
            gts_ref[...] = gate[n:, :]
            acts_ref[...] = activation(gate[n:], up[n:], prev_ref[1], prev_ref[0])

    pending = None
    for rows in _row_blocks(tm):
        hn = hn_ref[rows.start:, :] if rows.stop == tm else hn_ref[rows, :]
        dots = (rows, _dot(hn, wg), _dot(hn, wu))
        if pending is not None:
            finish(*pending)
        pending = dots
    finish(*pending)


def ffn_in(x, xs, g, w_in, conv_w, conv_b, li, prev, *, tm, tf):
    m, k = x.shape
    ns = xs.shape[1]
    nf = D_FF // tf
    n_i = m // tm
    act, act_s, tail, gate_s = pl.pallas_call(
        _ffn_in_kernel,
        grid=(n_i, nf),
        in_specs=[
            _row_tile_spec((tm, k), lambda i, f: (i, 0), F32),
            pl.BlockSpec((None, ns, k), lambda i, f: (0, 0, 0)),
            pl.BlockSpec((1, k), lambda i, f: (0, 0)),
            pl.BlockSpec((None, k, tf), lambda i, f: (li, 0, f)),
            pl.BlockSpec((None, k, tf), lambda i, f: (li, 0, nf + f)),
            pl.BlockSpec((None, CONV_F, tf), lambda i, f: (li, 0, f)),
            pl.BlockSpec((None, 1, tf), lambda i, f: (li, 0, f)),
            pl.BlockSpec((2, ns, tf), lambda i, f: (0, 0, f)),
        ],
        out_specs=[
            pl.BlockSpec((tm, tf), lambda i, f: (i, f)),
            pl.BlockSpec((None, ns, tf), lambda i, f: (i, 0, f)),
            pl.BlockSpec((None, SUBLANES, tf), lambda i, f: (i, 0, f)),
            pl.BlockSpec((None, ns, tf), lambda i, f: (i, 0, f)),
        ],
        out_shape=[jax.ShapeDtypeStruct((m, D_FF), BF16), jax.ShapeDtypeStruct((n_i, ns, D_FF), BF16),
                   jax.ShapeDtypeStruct((n_i, SUBLANES, D_FF), F32), jax.ShapeDtypeStruct((n_i, ns, D_FF), F32)],
        scratch_shapes=[pltpu.VMEM((tm + ns, k), BF16), pltpu.VMEM((tm + SUBLANES, tf), F32)],
        compiler_params=_params(2),
        name="ffn_in",
    )(x, xs, g.reshape(1, k), w_in, w_in, conv_w, conv_b.reshape(DEPTH, 1, D_FF), prev)
    return act, act_s, tail, gate_s


def _lru_gates(xc, wg_ref, ba, bi, lam):
    lam_abs = jnp.abs(lam)
    softplus_neg = jnp.maximum(-lam, 0.0) + jnp.log(1.0 + jnp.exp(-lam_abs))
    a_parts, u_parts = [], []
    for gi in range(N_LRU_GROUPS):
        cols = slice(gi * LRU_GROUP, (gi + 1) * LRU_GROUP)
        xg = xc[:, cols]
        proj = _dot(xg.astype(BF16), wg_ref[gi])
        r = _sigmoid(proj[:, :LRU_GROUP] + ba[:, cols])
        i = _sigmoid(proj[:, LRU_GROUP:] + bi[:, cols])
        log_a = -LRU_C * r * softplus_neg[:, cols]
        a = jnp.exp(log_a)
        a_parts.append(a)
        u_parts.append(jnp.sqrt(1.0 - a * a) * (i * xg))
    return a_parts, u_parts


def _lru_seq_kernel(gate_ref, xr_ref, cw_ref, cb_ref, wg_ref, ba_ref, bi_ref, lam_ref,
                    y_ref, hlast_ref, ext_ref, a_ref, u_ref, carry_ref):
    tt = xr_ref.shape[0]
    n_grp = tt // SUBLANES

    @pl.when(pl.program_id(1) == 0)
    def _():
        ext_ref[0:SUBLANES, :] = jnp.zeros((SUBLANES, D_RNN), F32)
        carry_ref[...] = jnp.zeros((1, D_RNN), F32)

    xr = xr_ref[...]
    ext_ref[SUBLANES:, :] = xr
    cw = cw_ref[...]
    xc = cb_ref[...] + xr * cw[CONV_A - 1:CONV_A]
    for j in range(CONV_A - 1):
        off = SUBLANES - (CONV_A - 1) + j
        xc = xc + ext_ref[off:off + tt, :] * cw[j:j + 1]
    ext_ref[0:SUBLANES, :] = xr[tt - SUBLANES:, :]

    a_parts, u_parts = _lru_gates(xc, wg_ref, ba_ref[...], bi_ref[...], lam_ref[...])
    for gi in range(N_LRU_GROUPS):
        cols = slice(gi * LRU_GROUP, (gi + 1) * LRU_GROUP)
        a_ref[:, :, cols] = a_parts[gi].reshape(n_grp, SUBLANES, LRU_GROUP)
        u_ref[:, :, cols] = u_parts[gi].reshape(n_grp, SUBLANES, LRU_GROUP)

    sub = lax.broadcasted_iota(jnp.int32, (n_grp, SUBLANES, LANES), 1)
    for ci in range(D_RNN // LANES):
        cols = slice(ci * LANES, (ci + 1) * LANES)
        a3 = a_ref[:, :, cols]
        u3 = u_ref[:, :, cols]
        d = 1
        while d < SUBLANES:
            keep = sub >= d
            a_sh = jnp.where(keep, pltpu.roll(a3, d, 1), 1.0)
            u_sh = jnp.where(keep, pltpu.roll(u3, d, 1), 0.0)
            u3 = a3 * u_sh + u3
            a3 = a3 * a_sh
            d *= 2
        h_prev = jnp.broadcast_to(carry_ref[:, cols], (SUBLANES, LANES))
        for g in range(0, n_grp, 2):
            h0 = a3[g] * h_prev + u3[g]
            h_prev = jnp.broadcast_to(h0[SUBLANES - 1:, :], (SUBLANES, LANES))
            h1 = a3[g + 1] * h_prev + u3[g + 1]
            h_prev = jnp.broadcast_to(h1[SUBLANES - 1:, :], (SUBLANES, LANES))
            rows = slice(g * SUBLANES, (g + 2) * SUBLANES)
            h = jnp.concatenate([h0, h1], axis=0)
            y_ref[rows, cols] = (gate_ref[rows, cols].astype(F32) * h).astype(BF16)
        carry_ref[:, cols] = h_prev[0:1, :]
        hlast_ref[:, cols] = h_prev[0:1, :]


def lru_seq(gate_act, xr, conv_w, conv_b, wg, b_a, b_i, lam, *, batch, seq_len, tt):
    m = xr.shape[0]
    nt = seq_len // tt
    vec = lambda: pl.BlockSpec((1, D_RNN), lambda b, t: (0, 0))
    y, h_last = pl.pallas_call(
        _lru_seq_kernel,
        grid=(batch, nt),
        in_specs=[
            pl.BlockSpec((tt, D_RNN), lambda b, t: (b * nt + t, 0)),
            pl.BlockSpec((tt, D_RNN), lambda b, t: (b * nt + t, 0)),
            pl.BlockSpec((CONV_A, D_RNN), lambda b, t: (0, 0)),
            vec(),
            pl.BlockSpec((N_LRU_GROUPS, LRU_GROUP, 2 * LRU_GROUP), lambda b, t: (0, 0, 0)),
            vec(), vec(), vec(),
        ],
        out_specs=[
            pl.BlockSpec((tt, D_RNN), lambda b, t: (b * nt + t, 0)),
            pl.BlockSpec((None, 1, D_RNN), lambda b, t: (b, 0, 0)),
        ],
        out_shape=[jax.ShapeDtypeStruct((m, D_RNN), BF16), jax.ShapeDtypeStruct((batch, 1, D_RNN), F32)],
        scratch_shapes=[pltpu.VMEM((tt + SUBLANES, D_RNN), F32), pltpu.VMEM((tt // SUBLANES, SUBLANES, D_RNN), F32),
                        pltpu.VMEM((tt // SUBLANES, SUBLANES, D_RNN), F32), pltpu.VMEM((1, D_RNN), F32)],
        compiler_params=_params(2),
        name="lru_seq",
    )(gate_act, xr, conv_w, conv_b.reshape(1, D_RNN), wg, b_a.reshape(1, D_RNN), b_i.reshape(1, D_RNN),
      lam.reshape(1, D_RNN))
    return y, h_last.reshape(batch, D_RNN)


def _lru_step_kernel(gate_ref, xr_ref, h0_ref, cbuf_ref, cw_ref, cb_ref, wg_ref, ba_ref, bi_ref, lam_ref,
                     y_ref, h_ref):
    xr = xr_ref[...]
    cw = cw_ref[...]
    xc = cb_ref[...] + xr * cw[CONV_A - 1:CONV_A]
    for j in range(CONV_A - 1):
        xc = xc + cbuf_ref[j] * cw[j:j + 1]
    a_parts, u_parts = _lru_gates(xc, wg_ref, ba_ref[...], bi_ref[...], lam_ref[...])
    a = jnp.concatenate(a_parts, axis=1)
    u = jnp.concatenate(u_parts, axis=1)
    h = u + a * h0_ref[...]
    h_ref[...] = h
    y_ref[...] = (gate_ref[...].astype(F32) * h).astype(BF16)


def lru_step(gate_act, xr, h0, cbuf, conv_w, conv_b, wg, b_a, b_i, lam):
    m = xr.shape[0]
    return pl.pallas_call(
        _lru_step_kernel,
        out_shape=[jax.ShapeDtypeStruct((m, D_RNN), BF16), jax.ShapeDtypeStruct((m, D_RNN), F32)],
        compiler_params=pltpu.CompilerParams(vmem_limit_bytes=V7X_VMEM_LIMIT),
        name="lru_step",
    )(gate_act, xr, h0, cbuf, conv_w, conv_b.reshape(1, D_RNN), wg, b_a.reshape(1, D_RNN), b_i.reshape(1, D_RNN),
      lam.reshape(1, D_RNN))


def _lru_gate_weights(w_a, w_i):
    per = LRU_GROUP // LRU_BLOCK
    row_blk = np.arange(LRU_GROUP)[:, None] // LRU_BLOCK
    col_blk = np.arange(2 * LRU_GROUP)[None, :] % LRU_GROUP // LRU_BLOCK
    on_diagonal = jnp.asarray(row_blk == col_blk)
    rows = lambda w: w.astype(BF16).reshape(N_LRU_GROUPS, LRU_GROUP, LRU_BLOCK)
    tiled = jnp.concatenate([jnp.tile(rows(w_a), (1, 1, per)), jnp.tile(rows(w_i), (1, 1, per))], axis=2)
    return jnp.where(on_diagonal, tiled, jnp.zeros((), BF16))


def _rope_tables(pos):
    half = ROT_DIM // 2
    inv = ROPE_THETA ** (-jnp.arange(half, dtype=F32) * (2.0 / ROT_DIM))
    ang = pos[:, None] * inv[None, :]
    cos, sin = jnp.cos(ang), jnp.sin(ang)
    rows = pos.shape[0]
    ones = jnp.ones((rows, HEAD_DIM - ROT_DIM), F32)
    zeros_h = jnp.zeros((rows, half), F32)
    zeros_t = jnp.zeros((rows, HEAD_DIM - ROT_DIM), F32)
    c = jnp.concatenate([cos, cos, ones], axis=1)
    s_lo = jnp.concatenate([-sin, zeros_h, zeros_t], axis=1)
    s_hi = jnp.concatenate([zeros_h, sin, zeros_t], axis=1)
    rep = LANES // HEAD_DIM
    return tuple(jnp.tile(t, (1, rep)) for t in (c, s_lo, s_hi))


def _rope_lanes(x, tables):
    c, s_lo, s_hi = tables
    half = ROT_DIM // 2
    return x * c + pltpu.roll(x, LANES - half, 1) * s_lo + pltpu.roll(x, half, 1) * s_hi


def _rope_kernel(qk_ref, c_ref, slo_ref, shi_ref, q_ref, k_ref):
    tables = (c_ref[...], slo_ref[...], shi_ref[...])
    n_q = N_HEADS * HEAD_DIM // LANES
    for ci in range(QK_COLS // LANES):
        rot = _rope_lanes(qk_ref[:, ci * LANES:(ci + 1) * LANES], tables)
        if ci < n_q:
            q_ref[:, ci * LANES:(ci + 1) * LANES] = rot.astype(BF16)
        else:
            k_ref[:, (ci - n_q) * LANES:(ci - n_q + 1) * LANES] = rot


def rope(qkv, tables, *, tm):
    m = qkv.shape[0]
    nt = tables[0].shape[0] // tm
    tab = lambda: pl.BlockSpec((tm, LANES), lambda i: (i % nt, 0))
    return pl.pallas_call(
        _rope_kernel,
        grid=(m // tm,),
        in_specs=[pl.BlockSpec((tm, QK_COLS), lambda i: (i, 0)), tab(), tab(), tab()],
        out_specs=[pl.BlockSpec((tm, N_HEADS * HEAD_DIM), lambda i: (i, 0)),
                   pl.BlockSpec((tm, KV_COLS), lambda i: (i, 0))],
        out_shape=[jax.ShapeDtypeStruct((m, N_HEADS * HEAD_DIM), BF16), jax.ShapeDtypeStruct((m, KV_COLS), F32)],
        compiler_params=_params(1),
        name="rope",
    )(qkv, *tables)


def _swa_seq_kernel(sink_ref, q_ref, kp_ref, kc_ref, vp_ref, vc_ref, cp_ref, slop_ref, ship_ref,
                    cc_ref, sloc_ref, shic_ref, o_ref, krot_ref):
    qi = pl.program_id(1)
    tq = q_ref.shape[0]
    nk = 2 * tq
    rope_prev = (cp_ref[...], slop_ref[...], ship_ref[...])
    rope_cur = (cc_ref[...], sloc_ref[...], shic_ref[...])
    lane_blocks = [slice(i * LANES, (i + 1) * LANES) for i in range(KV_COLS // LANES)]
    k_prev = jnp.concatenate([_rope_lanes(kp_ref[:, blk], rope_prev) for blk in lane_blocks], axis=1)
    k_cur = jnp.concatenate([_rope_lanes(kc_ref[:, blk], rope_cur) for blk in lane_blocks], axis=1)
    krot_ref[...] = k_cur
    kk = jnp.concatenate([k_prev, k_cur], axis=0) * (HEAD_DIM ** -0.5)
    vv = jnp.concatenate([vp_ref[...], vc_ref[...]], axis=0).astype(BF16)
    key = lax.broadcasted_iota(jnp.int32, (nk, tq), 0)
    qry = lax.broadcasted_iota(jnp.int32, (nk, tq), 1)
    rel = tq + qry - key
    valid = (rel >= 0) & (rel < WINDOW) & ((qi > 0) | (key >= tq))
    low_k = lax.broadcasted_iota(jnp.int32, (nk, LANES), 1) < HEAD_DIM
    low_o = lax.broadcasted_iota(jnp.int32, (tq, LANES), 1) < HEAD_DIM
    for kh in range(N_KV):
        blk = slice((kh // 2) * LANES, (kh // 2 + 1) * LANES)
        in_low = kh % 2 == 0
        k_blk = kk[:, blk]
        k_swap = pltpu.roll(k_blk, HEAD_DIM, 1)
        k_lo = jnp.where(low_k, k_blk if in_low else k_swap, 0.0).astype(BF16)
        k_hi = jnp.where(low_k, 0.0, k_swap if in_low else k_blk).astype(BF16)
        v_blk = vv[:, blk]
        for hp in range(GROUP // 2):
            h0 = kh * GROUP + 2 * hp
            hcols = slice(h0 * HEAD_DIM, (h0 + 2) * HEAD_DIM)
            q_pair = _rope_lanes(q_ref[:, hcols], rope_cur).astype(BF16)
            outs = []
            for which, k_pad in enumerate((k_lo, k_hi)):
                s = jnp.where(valid, _dot_nt(k_pad, q_pair), -jnp.inf)
                sink = sink_ref[h0 + which]
                mx = jnp.maximum(jnp.max(s, axis=0, keepdims=True), sink)
                e = jnp.exp(s - mx)
                denom = jnp.sum(e, axis=0, keepdims=True) + jnp.exp(sink - mx)
                p = (e * (1.0 / denom)).astype(BF16)
                outs.append(_dot_tn(p, v_blk))
            if in_low:
                o_pair = jnp.where(low_o, outs[0], pltpu.roll(outs[1], HEAD_DIM, 1))
            else:
                o_pair = jnp.where(low_o, pltpu.roll(outs[0], HEAD_DIM, 1), outs[1])
            o_ref[:, hcols] = o_pair.astype(BF16)


def swa_seq(qkv, rope_tables, sinks, *, batch, seq_len):
    m = qkv.shape[0]
    tq = WINDOW
    nq = seq_len // tq
    k_col = N_HEADS * HEAD_DIM // KV_COLS
    v_col = QK_COLS // KV_COLS
    prev = lambda col: pl.BlockSpec((tq, KV_COLS), lambda b, i: (b * nq + jnp.maximum(i - 1, 0), col))
    cur = lambda col: pl.BlockSpec((tq, KV_COLS), lambda b, i: (b * nq + i, col))
    tab_prev = lambda: pl.BlockSpec((tq, LANES), lambda b, i: (jnp.maximum(i - 1, 0), 0))
    tab_cur = lambda: pl.BlockSpec((tq, LANES), lambda b, i: (i, 0))
    return pl.pallas_call(
        _swa_seq_kernel,
        grid=(batch, nq),
        in_specs=[
            pl.BlockSpec(memory_space=pltpu.SMEM),
            pl.BlockSpec((tq, N_HEADS * HEAD_DIM), lambda b, i: (b * nq + i, 0)),
            prev(k_col), cur(k_col), prev(v_col), cur(v_col),
            tab_prev(), tab_prev(), tab_prev(), tab_cur(), tab_cur(), tab_cur(),
        ],
        out_specs=[pl.BlockSpec((tq, N_HEADS * HEAD_DIM), lambda b, i: (b * nq + i, 0)),
                   pl.BlockSpec((tq, KV_COLS), lambda b, i: (b * nq + i, 0))],
        out_shape=[jax.ShapeDtypeStruct((m, N_HEADS * HEAD_DIM), BF16), jax.ShapeDtypeStruct((m, KV_COLS), F32)],
        compiler_params=_params(2),
        name="swa_seq",
    )(sinks, qkv, qkv, qkv, qkv, qkv, *rope_tables, *rope_tables)


def _swa_step_kernel(q_ref, kn_ref, vn_ref, ck_ref, cv_ref, sink_ref, o_ref):
    col = lax.broadcasted_iota(jnp.int32, (GROUP, WINDOW), 1)
    scale = HEAD_DIM ** -0.5
    items = [(bi, kh) for bi in range(q_ref.shape[0]) for kh in range(N_KV)]
    heads = lambda kh: slice(kh * GROUP, (kh + 1) * GROUP)
    kcols = lambda kh: slice(kh * HEAD_DIM, (kh + 1) * HEAD_DIM)
    scores = []
    for bi, kh in items:
        q = q_ref[bi, heads(kh), :]
        k_new = kn_ref[bi, kh:kh + 1, :].astype(BF16).astype(F32)
        s_c = _dot_nt(q, ck_ref[bi, :, kcols(kh)].astype(BF16)) * scale
        s_c = jnp.where(col >= 1, s_c, -jnp.inf)
        s_n = jnp.sum(q.astype(F32) * k_new, axis=-1, keepdims=True) * scale
        scores.append((s_c, s_n))
    probs = []
    for (bi, kh), (s_c, s_n) in zip(items, scores):
        sink = sink_ref[heads(kh), :]
        mx = jnp.maximum(jnp.maximum(jnp.max(s_c, axis=-1, keepdims=True), s_n), sink)
        p_c = jnp.exp(s_c - mx)
        p_n = jnp.exp(s_n - mx)
        denom = jnp.sum(p_c, axis=-1, keepdims=True) + p_n + jnp.exp(sink - mx)
        probs.append((p_c.astype(BF16), p_n.astype(BF16).astype(F32), denom))
    for (bi, kh), (p_c, p_n, denom) in zip(items, probs):
        v_new = vn_ref[bi, kh:kh + 1, :].astype(BF16).astype(F32)
        o = _dot(p_c, cv_ref[bi, :, kcols(kh)].astype(BF16)) + p_n * v_new
        o_ref[bi, heads(kh), :] = (o / denom).astype(BF16)


def swa_step(q3, k_new, v_new, cache_k, cache_v, sinks, *, seqs_per_step=8):
    b = q3.shape[0]
    nb = seqs_per_step
    return pl.pallas_call(
        _swa_step_kernel,
        grid=(b // nb,),
        in_specs=[
            pl.BlockSpec((nb, N_HEADS, HEAD_DIM), lambda i: (i, 0, 0)),
            pl.BlockSpec((nb, N_KV, HEAD_DIM), lambda i: (i, 0, 0)),
            pl.BlockSpec((nb, N_KV, HEAD_DIM), lambda i: (i, 0, 0)),
            pl.BlockSpec((nb, WINDOW, KV_COLS), lambda i: (i, 0, 0)),
            pl.BlockSpec((nb, WINDOW, KV_COLS), lambda i: (i, 0, 0)),
            pl.BlockSpec((N_HEADS, 1), lambda i: (0, 0)),
        ],
        out_specs=pl.BlockSpec((nb, N_HEADS, HEAD_DIM), lambda i: (i, 0, 0)),
        out_shape=jax.ShapeDtypeStruct((b, N_HEADS, HEAD_DIM), BF16),
        compiler_params=_params(1),
        name="swa_step",
    )(q3, k_new, v_new, cache_k, cache_v, sinks.reshape(N_HEADS, 1))


def _hgrn_lower_bound(logits, layer):
    mx = jnp.max(logits, axis=0, keepdims=True)
    e = jnp.exp(logits - mx)
    sm = e / jnp.sum(e, axis=0, keepdims=True)
    lb = jnp.zeros_like(sm[0:1])
    for i in range(1, layer + 1):
        lb = lb + sm[i:i + 1]
    return lb


def _hgrn_consts():
    c = HG_CHUNK
    t = np.arange(c)[:, None]
    s = np.arange(c)[None, :]
    tri = (s <= t).astype(np.float32)
    sel = [tri]
    msk = [(s == t)]
    for lvl in range(1, HG_LEVELS + 1):
        w = 1 << (lvl - 1)
        ref_row = (t // (2 * w)) * (2 * w) + w - 1
        upper_t = (t % (2 * w)) >= w
        lower_s = (s % (2 * w)) < w
        sign = np.where(upper_t, 1.0, -1.0)
        sel.append(sign * (tri - (s <= ref_row)))
        msk.append(((t // (2 * w)) == (s // (2 * w))) & upper_t & lower_s)
    sel.append(1.0 - tri)
    sel = np.concatenate(sel, axis=0).astype(np.float32)
    sel2 = np.concatenate([sel, sel], axis=1)
    msk = np.stack(msk, axis=0).astype(np.float32)
    return jnp.asarray(sel2, BF16), jnp.asarray(msk, F32)


def _hgrn_seq_kernel(q_ref, z_ref, v_ref, g_ref, lbl_ref, gn_ref, sel_ref, msk_ref, y_ref, s_out_ref, st_ref,
                     *, layer, heads_per_step):
    ti = pl.program_id(2)
    tt = q_ref.shape[0]
    c = HG_CHUNK

    @pl.when(ti == 0)
    def _():
        st_ref[...] = jnp.zeros(st_ref.shape, F32)

    lb_all = _hgrn_lower_bound(lbl_ref[...], layer)
    gn = gn_ref[...]
    row = lax.broadcasted_iota(jnp.int32, (c, heads_per_step * HG_DK), 0)
    upper = [None] + [((row >> (lvl - 1)) & 1) == 1 for lvl in range(1, HG_LEVELS + 1)]
    states = [st_ref[hh] for hh in range(heads_per_step)]

    head_cols = [slice(hh * HG_DK, (hh + 1) * HG_DK) for hh in range(heads_per_step)]

    def decays(ci):
        rows = slice(ci * c, (ci + 1) * c)
        q = q_ref[rows, :]
        sq = q * _sigmoid(q)
        sz = _sigmoid(z_ref[rows, :])
        log_f = jnp.log(lb_all + (1.0 - lb_all) * sz)
        k = (1.0 - lb_all) * (1.0 - sz)
        hi = log_f.astype(BF16)
        lo = (log_f - hi.astype(F32)).astype(BF16)
        ex = _dot(sel_ref[...], jnp.concatenate([hi, lo], axis=0))
        e_b = jnp.exp(ex[0:c])
        xs = [(jnp.where(upper[lvl], sq, k) * jnp.exp(ex[lvl * c:(lvl + 1) * c])).astype(BF16)
              for lvl in range(1, HG_LEVELS + 1)]
        return dict(rows=rows, v=v_ref[rows, :].astype(BF16), sq=sq.astype(BF16), k=k.astype(BF16), xs=xs,
                    q_dec=(sq * e_b).astype(BF16), k_dec=(k * jnp.exp(ex[(HG_LEVELS + 1) * c:])).astype(BF16),
                    e_last=e_b[c - 1:c, :])

    def scores(d):
        atts = []
        for cols in head_cols:
            att = _dot_nt(d["sq"][:, cols], d["k"][:, cols]) * msk_ref[0]
            for lvl in range(1, HG_LEVELS + 1):
                x = d["xs"][lvl - 1][:, cols]
                att = att + _dot_nt(x, x) * msk_ref[lvl]
            atts.append(att.astype(BF16))
        return atts

    def outputs(d, atts):
        outs = []
        for hh, cols in enumerate(head_cols):
            st = states[hh]
            o = _dot(atts[hh], d["v"][:, cols]) + _dot_nt(d["q_dec"][:, cols], st.astype(BF16))
            states[hh] = st * d["e_last"][:, cols] + _dot_tn(d["v"][:, cols], d["k_dec"][:, cols])
            outs.append(_rms_rows(o, gn))
        g = g_ref[d["rows"], :]
        y_ref[d["rows"], :] = (jnp.concatenate(outs, axis=1) * (g * _sigmoid(g))).astype(BF16)

    n_chunks = tt // c
    stage_a, stage_b = {}, {}
    for step in range(n_chunks + 2):
        if step < n_chunks:
            stage_a[step] = decays(step)
        if 0 <= step - 1 < n_chunks:
            stage_b[step - 1] = scores(stage_a[step - 1])
        if 0 <= step - 2 < n_chunks:
            outputs(stage_a.pop(step - 2), stage_b.pop(step - 2))

    for hh in range(heads_per_step):
        st_ref[hh] = states[hh]

    @pl.when(ti == pl.num_programs(2) - 1)
    def _():
        for hh in range(heads_per_step):
            s_out_ref[hh] = states[hh].T


def hgrn_seq(qzvg, lb_logits, g_norm, *, layer, batch, seq_len, tt, heads_per_step):
    m = qzvg.shape[0]
    nt = seq_len // tt
    hw = heads_per_step * HG_DK
    nh = HG_HEADS // heads_per_step
    sel, msk = _hgrn_consts()
    part = lambda p: pl.BlockSpec((tt, hw), lambda b, h, t: (b * nt + t, p * nh + h))
    y, s_out = pl.pallas_call(
        functools.partial(_hgrn_seq_kernel, layer=layer, heads_per_step=heads_per_step),
        grid=(batch, nh, nt),
        in_specs=[
            part(0), part(1), part(2), part(3),
            pl.BlockSpec((DEPTH, hw), lambda b, h, t: (0, h)),
            pl.BlockSpec((1, HG_DV), lambda b, h, t: (0, 0)),
            pl.BlockSpec(sel.shape, lambda b, h, t: (0, 0)),
            pl.BlockSpec(msk.shape, lambda b, h, t: (0, 0, 0)),
        ],
        out_specs=[
            pl.BlockSpec((tt, hw), lambda b, h, t: (b * nt + t, h)),
            pl.BlockSpec((None, heads_per_step, HG_DK, HG_DV), lambda b, h, t: (b, h, 0, 0)),
        ],
        out_shape=[jax.ShapeDtypeStruct((m, HG_HEADS * HG_DV), BF16),
                   jax.ShapeDtypeStruct((batch, HG_HEADS, HG_DK, HG_DV), F32)],
        scratch_shapes=[pltpu.VMEM((heads_per_step, HG_DV, HG_DK), F32)],
        compiler_params=_params(3),
        name="hgrn_seq",
    )(qzvg, qzvg, qzvg, qzvg, lb_logits, g_norm.reshape(1, HG_DV), sel, msk)
    return y, s_out


def _hgrn_step_kernel(x_ref, s_ref, lbl_ref, gn_ref, y_ref, s_out_ref, *, layer):
    nh = HG_HEADS
    lb = _hgrn_lower_bound(lbl_ref[...], layer)[0]
    pad = jnp.zeros((LANES - 3 * nh, HG_DK), F32)
    for bi in range(x_ref.shape[0]):
        q = x_ref[bi, 0:nh, :]
        z = x_ref[bi, nh:2 * nh, :]
        v = x_ref[bi, 2 * nh:3 * nh, :]
        g = x_ref[bi, 3 * nh:4 * nh, :]
        sq = q * _sigmoid(q)
        sz = _sigmoid(z)
        f = lb + (1.0 - lb) * sz
        k = (1.0 - lb) * (1.0 - sz)
        cols = jnp.concatenate([f, k, sq, pad], axis=0).T
        outs = []
        for h in range(nh):
            f_col = cols[:, h:h + 1]
            k_col = cols[:, nh + h:nh + h + 1]
            q_col = cols[:, 2 * nh + h:2 * nh + h + 1]
            s_new = s_ref[bi, h] * f_col + k_col * v[h:h + 1, :]
            s_out_ref[bi, h] = s_new
            outs.append(jnp.sum(s_new * q_col, axis=0, keepdims=True))
        o = jnp.concatenate(outs, axis=0)
        y_ref[bi] = (_rms_rows(o, gn_ref[...]) * (g * _sigmoid(g))).astype(BF16)


def hgrn_step(x4, state, lb_logits, g_norm, *, layer, seqs_per_step=4):
    b = x4.shape[0]
    nb = seqs_per_step
    return pl.pallas_call(
        functools.partial(_hgrn_step_kernel, layer=layer),
        grid=(b // nb,),
        in_specs=[
            pl.BlockSpec((nb, 4 * HG_HEADS, HG_DK), lambda i: (i, 0, 0)),
            pl.BlockSpec((nb, HG_HEADS, HG_DK, HG_DV), lambda i: (i, 0, 0, 0)),
            pl.BlockSpec((DEPTH, HG_HEADS, HG_DK), lambda i: (0, 0, 0)),
            pl.BlockSpec((1, HG_DV), lambda i: (0, 0)),
        ],
        out_specs=[
            pl.BlockSpec((nb, HG_HEADS, HG_DV), lambda i: (i, 0, 0)),
            pl.BlockSpec((nb, HG_HEADS, HG_DK, HG_DV), lambda i: (i, 0, 0, 0)),
        ],
        out_shape=[jax.ShapeDtypeStruct((b, HG_HEADS, HG_DV), BF16),
                   jax.ShapeDtypeStruct((b, HG_HEADS, HG_DK, HG_DV), F32)],
        compiler_params=_params(1),
        name="hgrn_step",
    )(x4, state, lb_logits.reshape(DEPTH, HG_HEADS, HG_DK), g_norm.reshape(1, HG_DV))


def _trunk(x3, xs3, states, w):
    batch, seq_len, _ = x3.shape
    m = batch * seq_len
    ns = xs3.shape[0]
    x = x3.reshape(m, D_MODEL)
    xs = xs3.reshape(1, ns, D_MODEL)
    tm = seq_len
    tn_in, tn_out, tf = 512, 256, 512
    zeros = lambda n: jnp.zeros((n,), F32)
    lru_wg = [_lru_gate_weights(w["lru_w_a"][j], w["lru_w_i"][j]) for j in range(w["lru_w_a"].shape[0])]
    rope_seq = _rope_tables(jnp.arange(seq_len, dtype=F32))
    rope_step = _rope_tables(jnp.full((ns,), PAST_LEN, F32))

    def out_proj(a, a_s, w_stack, li, b, res, res_s):
        k_blocks = 1
        while tm * (a.shape[1] // k_blocks) * a.dtype.itemsize >= SINGLE_BUFFER_BYTES:
            k_blocks *= 2
        k = a.shape[1] // k_blocks
        cap = max(LANES, min(tn_out, W_TILE_BYTES // (4 * k) // LANES * LANES))
        for ki in range(k_blocks):
            bias = b if ki == k_blocks - 1 else jnp.zeros_like(b)
            res, res_s = matmul_residual(a, a_s, w_stack, li, bias, res, res_s, tm=tm,
                                         tn=_largest_tile(D_MODEL, cap), k_blocks=k_blocks, k_index=ki)
        return res, res_s

    new_p = {"lru_h": [], "lru_conv": [], "swa_k": [], "swa_v": [], "hgrn": [], "ffn_conv": []}
    new_s = {"lru_h": [], "lru_conv": [], "swa_k": [], "swa_v": [], "hgrn": [], "ffn_conv": []}
    for layer in range(DEPTH):
        kind, j = LAYER_MIXER[layer], LAYER_SLOT[layer]
        g_mix = w["norm_mix"][layer]
        if kind == 0:
            gate, gate_s, xr, xr_s = norm_matmul_gated(x, xs, g_mix, w["lru_w_in"], j, tm=tm, tn=tn_in // 2)
            xr_s = xr_s[0]
            lru_w = (w["lru_conv_w"][j], w["lru_conv_b"][j], lru_wg[j], w["lru_b_a"][j], w["lru_b_i"][j],
                     w["lru_lambda"][j])
            y, h_new = lru_seq(gate, xr, *lru_w, batch=batch, seq_len=seq_len, tt=256)
            new_p["lru_h"].append(h_new)
            new_p["lru_conv"].append(xr.reshape(batch, seq_len, D_RNN)[:, seq_len - (CONV_A - 1):])
            cbuf = states["lru_conv"][j]
            ys, hs_new = lru_step(gate_s[0], xr_s, states["lru_h"][j], jnp.swapaxes(cbuf, 0, 1), *lru_w)
            new_s["lru_h"].append(hs_new)
            new_s["lru_conv"].append(jnp.concatenate([cbuf[:, 1:], xr_s[:, None]], axis=1))
            x, xs = out_proj(y, ys[None], w["lru_w_out"], j, zeros(D_MODEL), x, xs)
        elif kind == 1:
            n = QK_COLS + KV_COLS
            qkv, qkvs = norm_matmul(x, xs, g_mix, w["swa_w_qkv"], j, w["swa_b_qkv"][j], tm=tm,
                                    tn=_largest_tile(n, tn_in))
            qkvs = qkvs[0]
            o, k_rot = swa_seq(qkv, rope_seq, w["swa_sinks"][j], batch=batch, seq_len=seq_len)
            k_win = k_rot.reshape(batch, seq_len, KV_COLS)[:, seq_len - WINDOW:]
            v_win = qkv.reshape(batch, seq_len, n)[:, seq_len - WINDOW:, QK_COLS:]
            new_p["swa_k"].append(k_win.reshape(batch, WINDOW, N_KV, HEAD_DIM))
            new_p["swa_v"].append(v_win.reshape(batch, WINDOW, N_KV, HEAD_DIM))
            qs_rot, ks_rot = rope(qkvs, rope_step, tm=ns)
            ck = states["swa_k"][j]
            cv = states["swa_v"][j]
            k_new = ks_rot.reshape(ns, N_KV, HEAD_DIM)
            v_new = qkvs[:, QK_COLS:].reshape(ns, N_KV, HEAD_DIM)
            os_ = swa_step(qs_rot.reshape(ns, N_HEADS, HEAD_DIM), k_new, v_new,
                           ck.reshape(ns, WINDOW, KV_COLS), cv.reshape(ns, WINDOW, KV_COLS), w["swa_sinks"][j])
            new_s["swa_k"].append(jnp.concatenate([ck[:, 1:], k_new[:, None]], axis=1))
            new_s["swa_v"].append(jnp.concatenate([cv[:, 1:], v_new[:, None]], axis=1))
            x, xs = out_proj(o, os_.reshape(1, ns, N_HEADS * HEAD_DIM), w["swa_w_o"], j, w["swa_b_o"][j], x, xs)
        else:
            n = 2 * HG_HEADS * HG_DK + 2 * HG_HEADS * HG_DV
            qzvg, qzvgs = norm_matmul(x, xs, g_mix, w["hg_w_in"], j, zeros(n), tm=tm, tn=_largest_tile(n, tn_in))
            y, s_new = hgrn_seq(qzvg, w["hg_lb_logits"], w["hg_norm"][j], layer=layer, batch=batch,
                                seq_len=seq_len, tt=1024, heads_per_step=4)
            new_p["hgrn"].append(s_new)
            ys, ss_new = hgrn_step(qzvgs[0].reshape(ns, 4 * HG_HEADS, HG_DK), states["hgrn"][j], w["hg_lb_logits"],
                                   w["hg_norm"][j], layer=layer)
            new_s["hgrn"].append(ss_new)
            x, xs = out_proj(y, ys.reshape(1, ns, HG_HEADS * HG_DV), w["hg_w_o"], j, zeros(D_MODEL), x, xs)

        fbuf = states["ffn_conv"][layer]
        act, act_s, tail, gate_s = ffn_in(x, xs, w["norm_ffn"][layer], w["ffn_w_in"], w["ffn_conv_w"],
                                          w["ffn_conv_b"], layer, jnp.swapaxes(fbuf, 0, 1), tm=tm, tf=tf)
        new_p["ffn_conv"].append(tail[:, SUBLANES - (CONV_F - 1):])
        new_s["ffn_conv"].append(jnp.concatenate([fbuf[:, 1:], gate_s[0][:, None]], axis=1))
        x, xs = out_proj(act, act_s, w["ffn_w_out"], layer, zeros(D_MODEL), x, xs)

    y = final_norm(x, w["norm_final"], tm=1024).reshape(batch, seq_len, D_MODEL)
    ys = final_norm(xs[0], w["norm_final"], tm=ns).reshape(ns, 1, D_MODEL)
    order = ("lru_h", "lru_conv", "swa_k", "swa_v", "hgrn", "ffn_conv")
    return ((y, ys) + tuple(jnp.stack(new_p[k]) for k in order) + tuple(jnp.stack(new_s[k]) for k in order))


def kernel(x_prompt, x_sample, state_lru_h, state_lru_conv, cache_swa_k, cache_swa_v, state_hgrn, state_ffn_conv,
           norm_mix, norm_ffn, norm_final,
           lru_w_in, lru_conv_w, lru_conv_b, lru_w_a, lru_b_a, lru_w_i, lru_b_i, lru_lambda, lru_w_out,
           swa_w_qkv, swa_b_qkv, swa_sinks, swa_w_o, swa_b_o,
           hg_w_in, hg_lb_logits, hg_norm, hg_w_o,
           ffn_w_in, ffn_conv_w, ffn_conv_b, ffn_w_out):
    w = dict(norm_mix=norm_mix, norm_ffn=norm_ffn, norm_final=norm_final,
             lru_w_in=lru_w_in, lru_conv_w=lru_conv_w, lru_conv_b=lru_conv_b, lru_w_a=lru_w_a, lru_b_a=lru_b_a,
             lru_w_i=lru_w_i, lru_b_i=lru_b_i, lru_lambda=lru_lambda, lru_w_out=lru_w_out,
             swa_w_qkv=swa_w_qkv, swa_b_qkv=swa_b_qkv, swa_sinks=swa_sinks, swa_w_o=swa_w_o, swa_b_o=swa_b_o,
             hg_w_in=hg_w_in, hg_lb_logits=hg_lb_logits, hg_norm=hg_norm, hg_w_o=hg_w_o,
             ffn_w_in=ffn_w_in, ffn_conv_w=ffn_conv_w, ffn_conv_b=ffn_conv_b, ffn_w_out=ffn_w_out)
    states = dict(lru_h=state_lru_h, lru_conv=state_lru_conv, swa_k=cache_swa_k, swa_v=cache_swa_v,
                  hgrn=state_hgrn, ffn_conv=state_ffn_conv)
    return _trunk(x_prompt, x_sample, states, w)
```

```python
import functools

import numpy as np
import jax
import jax.numpy as jnp
from jax import lax
from jax.experimental import pallas as pl
from jax.experimental.pallas import tpu as pltpu

F32 = jnp.float32
BF16 = jnp.bfloat16

D_MODEL = 2048
DEPTH = 4
PAST_LEN = 16384
EPS = 1e-6
LAYER_MIXER = tuple(i % 3 for i in range(DEPTH))
LAYER_SLOT = tuple(LAYER_MIXER[:i].count(LAYER_MIXER[i]) for i in range(DEPTH))

D_RNN = 2560
LRU_BLOCKS = 16
LRU_BLOCK = D_RNN // LRU_BLOCKS
LRU_GROUP = 640
N_LRU_GROUPS = D_RNN // LRU_GROUP
CONV_A = 4
LRU_C = 8.0

N_HEADS = 32
N_KV = 4
HEAD_DIM = 64
GROUP = N_HEADS // N_KV
WINDOW = 128
ROT_DIM = HEAD_DIM // 4
ROPE_THETA = 500000.0
QK_COLS = (N_HEADS + N_KV) * HEAD_DIM
KV_COLS = N_KV * HEAD_DIM

HG_HEADS = 16
HG_DK = 128
HG_DV = 128
HG_CHUNK = 64
HG_LEVELS = 6

D_FF = 3 * D_MODEL
CONV_F = 3

LANES = 128
SUBLANES = 8
V7X_VMEM_LIMIT = 56 * 1024 * 1024
SINGLE_BUFFER_BYTES = 14 * 1024 * 1024
W_TILE_BYTES = 8 * 1024 * 1024
DOT_ROWS = 256


def _params(n_axes, vmem=V7X_VMEM_LIMIT):
    return pltpu.CompilerParams(dimension_semantics=("arbitrary",) * n_axes, vmem_limit_bytes=vmem)


def _sigmoid(x):
    return 1.0 / (1.0 + jnp.exp(-x))


def _rms_rows(x, g):
    var = jnp.mean(x * x, axis=-1, keepdims=True)
    return x * lax.rsqrt(var + EPS) * g


def _dot(a, b):
    return jnp.dot(a, b, preferred_element_type=F32)


def _dot_nt(a, b):
    return lax.dot_general(a, b, (((1,), (1,)), ((), ())), preferred_element_type=F32)


def _dot_tn(a, b):
    return lax.dot_general(a, b, (((0,), (0,)), ((), ())), preferred_element_type=F32)


def _largest_tile(n, cap):
    t = cap
    while n % t:
        t -= LANES
    return t


def _row_tile_spec(shape, index_map, dtype):
    if shape[0] * shape[1] * jnp.dtype(dtype).itemsize >= SINGLE_BUFFER_BYTES:
        return pl.BlockSpec(shape, index_map, pipeline_mode=pl.Buffered(1))
    return pl.BlockSpec(shape, index_map)


def _row_blocks(n_rows):
    step = min(n_rows, DOT_ROWS)
    return [slice(r, r + step) for r in range(0, n_rows, step)]


def _for_normalized_blocks(x_ref, xs_ref, g_ref, hn_ref, consume, on_first=None):
    tm = x_ref.shape[0]

    def run(normalize):
        if normalize:
            if on_first is not None:
                on_first()
            hn_ref[tm:, :] = _rms_rows(xs_ref[...], g_ref[...]).astype(BF16)
        for rows in _row_blocks(tm):
            if normalize:
                hn_ref[rows, :] = _rms_rows(x_ref[rows, :], g_ref[...]).astype(BF16)
            lhs = hn_ref[rows.start:, :] if rows.stop == tm else hn_ref[rows, :]
            consume(rows, lhs, rows.stop - rows.start)

    first = pl.program_id(1) == 0
    pl.when(first)(functools.partial(run, True))
    pl.when(jnp.logical_not(first))(functools.partial(run, False))


def _norm_matmul_kernel(x_ref, xs_ref, g_ref, w_ref, b_ref, o_ref, os_ref, hn_ref):
    w = w_ref[...].astype(BF16)

    def consume(rows, lhs, n):
        res = _dot(lhs, w) + b_ref[...]
        o_ref[rows, :] = res[:n]
        if res.shape[0] > n:
            os_ref[...] = res[n:]

    _for_normalized_blocks(x_ref, xs_ref, g_ref, hn_ref, consume)


def norm_matmul(x, xs, g, w_stack, li, b, *, tm, tn):
    m, k = x.shape
    ns = xs.shape[1]
    n = w_stack.shape[2]
    return pl.pallas_call(
        _norm_matmul_kernel,
        grid=(m // tm, n // tn),
        in_specs=[
            _row_tile_spec((tm, k), lambda i, j: (i, 0), F32),
            pl.BlockSpec((None, ns, k), lambda i, j: (0, 0, 0)),
            pl.BlockSpec((1, k), lambda i, j: (0, 0)),
            pl.BlockSpec((None, k, tn), lambda i, j: (li, 0, j)),
            pl.BlockSpec((1, tn), lambda i, j: (0, j)),
        ],
        out_specs=[pl.BlockSpec((tm, tn), lambda i, j: (i, j)),
                   pl.BlockSpec((None, ns, tn), lambda i, j: (i, 0, j))],
        out_shape=[jax.ShapeDtypeStruct((m, n), F32), jax.ShapeDtypeStruct((m // tm, ns, n), F32)],
        scratch_shapes=[pltpu.VMEM((tm + ns, k), BF16)],
        compiler_params=_params(2),
        name="norm_matmul",
    )(x, xs, g.reshape(1, k), w_stack, b.reshape(1, n))


def _norm_matmul_gated_kernel(x_ref, xs_ref, g_ref, wa_ref, wb_ref, act_ref, acts_ref, lin_ref, lins_ref, hn_ref):
    wa = wa_ref[...].astype(BF16)
    wb = wb_ref[...].astype(BF16)

    def consume(rows, lhs, n):
        act = jax.nn.gelu(_dot(lhs, wa)).astype(BF16)
        lin = _dot(lhs, wb)
        act_ref[rows, :] = act[:n]
        lin_ref[rows, :] = lin[:n]
        if act.shape[0] > n:
            acts_ref[...] = act[n:]
            lins_ref[...] = lin[n:]

    _for_normalized_blocks(x_ref, xs_ref, g_ref, hn_ref, consume)


def norm_matmul_gated(x, xs, g, w_stack, li, *, tm, tn):
    m, k = x.shape
    ns = xs.shape[1]
    n = w_stack.shape[2] // 2
    nj = n // tn
    n_i = m // tm
    rows_out = lambda: pl.BlockSpec((tm, tn), lambda i, j: (i, j))
    step_out = lambda: pl.BlockSpec((None, ns, tn), lambda i, j: (i, 0, j))
    return pl.pallas_call(
        _norm_matmul_gated_kernel,
        grid=(n_i, nj),
        in_specs=[
            _row_tile_spec((tm, k), lambda i, j: (i, 0), F32),
            pl.BlockSpec((None, ns, k), lambda i, j: (0, 0, 0)),
            pl.BlockSpec((1, k), lambda i, j: (0, 0)),
            pl.BlockSpec((None, k, tn), lambda i, j: (li, 0, j)),
            pl.BlockSpec((None, k, tn), lambda i, j: (li, 0, nj + j)),
        ],
        out_specs=[rows_out(), step_out(), rows_out(), step_out()],
        out_shape=[jax.ShapeDtypeStruct((m, n), BF16), jax.ShapeDtypeStruct((n_i, ns, n), BF16),
                   jax.ShapeDtypeStruct((m, n), F32), jax.ShapeDtypeStruct((n_i, ns, n), F32)],
        scratch_shapes=[pltpu.VMEM((tm + ns, k), BF16)],
        compiler_params=_params(2),
        name="norm_matmul_gated",
    )(x, xs, g.reshape(1, k), w_stack, w_stack)


def _matmul_residual_kernel(a_ref, as_ref, w_ref, b_ref, x_ref, xs_ref, o_ref, os_ref):
    tm = a_ref.shape[0]
    w = w_ref[...].astype(BF16)
    blocks = _row_blocks(tm)
    for rows in blocks[:-1]:
        o_ref[rows, :] = x_ref[rows, :] + (_dot(a_ref[rows, :], w) + b_ref[...])
    last = blocks[-1]
    lhs = jnp.concatenate([a_ref[last, :], as_ref[...]], axis=0)
    res = _dot(lhs, w) + b_ref[...]
    o_ref[last, :] = x_ref[last, :] + res[:tm - last.start]
    os_ref[...] = xs_ref[...] + res[tm - last.start:]


def matmul_residual(a, a_s, w_stack, li, b, x, xs, *, tm, tn, k_blocks=1, k_index=0):
    m = a.shape[0]
    k = a.shape[1] // k_blocks
    ns = xs.shape[1]
    n = w_stack.shape[2]
    return pl.pallas_call(
        _matmul_residual_kernel,
        grid=(m // tm, n // tn),
        in_specs=[
            _row_tile_spec((tm, k), lambda i, j: (i, k_index), a.dtype),
            pl.BlockSpec((None, ns, k), lambda i, j: (0, 0, k_index)),
            pl.BlockSpec((None, k, tn), lambda i, j: (li, k_index, j)),
            pl.BlockSpec((1, tn), lambda i, j: (0, j)),
            pl.BlockSpec((tm, tn), lambda i, j: (i, j)),
            pl.BlockSpec((None, ns, tn), lambda i, j: (0, 0, j)),
        ],
        out_specs=[pl.BlockSpec((tm, tn), lambda i, j: (i, j)),
                   pl.BlockSpec((None, ns, tn), lambda i, j: (i, 0, j))],
        out_shape=[jax.ShapeDtypeStruct((m, n), F32), jax.ShapeDtypeStruct((m // tm, ns, n), F32)],
        compiler_params=_params(2),
        name="matmul_residual",
    )(a, a_s, w_stack, b.reshape(1, n), x, xs)


def _norm_kernel(x_ref, g_ref, o_ref):
    o_ref[...] = _rms_rows(x_ref[...], g_ref[...])


def final_norm(x, g, *, tm):
    m, k = x.shape
    return pl.pallas_call(
        _norm_kernel,
        grid=(m // tm,),
        in_specs=[pl.BlockSpec((tm, k), lambda i: (i, 0)), pl.BlockSpec((1, k), lambda i: (0, 0))],
        out_specs=pl.BlockSpec((tm, k), lambda i: (i, 0)),
        out_shape=jax.ShapeDtypeStruct((m, k), F32),
        compiler_params=_params(1),
        name="final_norm",
    )(x, g.reshape(1, k))


def _ffn_in_kernel(x_ref, xs_ref, g_ref, wg_ref, wu_ref, cw_ref, cb_ref, prev_ref,
                   act_ref, acts_ref, gt_ref, gts_ref, hn_ref, ext_ref):
    wg = wg_ref[...].astype(BF16)
    wu = wu_ref[...].astype(BF16)
    cw = cw_ref[...]

    def activation(gate, up, prev1, prev2):
        conv = cb_ref[...] + gate * cw[2:3] + prev2 * cw[0:1] + prev1 * cw[1:2]
        return (jax.nn.gelu(conv) * up).astype(BF16)

    def zero_history():
        ext_ref[0:SUBLANES, :] = jnp.zeros((SUBLANES, ext_ref.shape[1]), F32)

    def consume(rows, lhs, n):
        r0 = rows.start
        gate = _dot(lhs, wg)
        up = _dot(lhs, wu)
        ext_ref[SUBLANES + r0:SUBLANES + r0 + n, :] = gate[:n]
        prev1 = ext_ref[SUBLANES - 1 + r0:SUBLANES - 1 + r0 + n, :]
        prev2 = ext_ref[SUBLANES - 2 + r0:SUBLANES - 2 + r0 + n, :]
        act_ref[rows, :] = activation(gate[:n], up[:n], prev1, prev2)
        if gate.shape[0] > n:
            gt_ref[...] = gate[n - SUBLANES:n, :]
            gts_ref[...] = gate[n:, :]
            acts_ref[...] = activation(gate[n:], up[n:], prev_ref[1], prev_ref[0])

    _for_normalized_blocks(x_ref, xs_ref, g_ref, hn_ref, consume, on_first=zero_history)


def ffn_in(x, xs, g, w_in, conv_w, conv_b, li, prev, *, tm, tf):
    m, k = x.shape
    ns = xs.shape[1]
    nf = D_FF // tf
    n_i = m // tm
    act, act_s, tail, gate_s = pl.pallas_call(
        _ffn_in_kernel,
        grid=(n_i, nf),
        in_specs=[
            _row_tile_spec((tm, k), lambda i, f: (i, 0), F32),
            pl.BlockSpec((None, ns, k), lambda i, f: (0, 0, 0)),
            pl.BlockSpec((1, k), lambda i, f: (0, 0)),
            pl.BlockSpec((None, k, tf), lambda i, f: (li, 0, f)),
            pl.BlockSpec((None, k, tf), lambda i, f: (li, 0, nf + f)),
            pl.BlockSpec((None, CONV_F, tf), lambda i, f: (li, 0, f)),
            pl.BlockSpec((None, 1, tf), lambda i, f: (li, 0, f)),
            pl.BlockSpec((2, ns, tf), lambda i, f: (0, 0, f)),
        ],
        out_specs=[
            pl.BlockSpec((tm, tf), lambda i, f: (i, f)),
            pl.BlockSpec((None, ns, tf), lambda i, f: (i, 0, f)),
            pl.BlockSpec((None, SUBLANES, tf), lambda i, f: (i, 0, f)),
            pl.BlockSpec((None, ns, tf), lambda i, f: (i, 0, f)),
        ],
        out_shape=[jax.ShapeDtypeStruct((m, D_FF), BF16), jax.ShapeDtypeStruct((n_i, ns, D_FF), BF16),
                   jax.ShapeDtypeStruct((n_i, SUBLANES, D_FF), F32), jax.ShapeDtypeStruct((n_i, ns, D_FF), F32)],
        scratch_shapes=[pltpu.VMEM((tm + ns, k), BF16), pltpu.VMEM((tm + SUBLANES, tf), F32)],
        compiler_params=_params(2),
        name="ffn_in",
    )(x, xs, g.reshape(1, k), w_in, w_in, conv_w, conv_b.reshape(DEPTH, 1, D_FF), prev)
    return act, act_s, tail, gate_s


def _lru_gates(xc, wg_ref, ba, bi, lam):
    lam_abs = jnp.abs(lam)
    softplus_neg = jnp.maximum(-lam, 0.0) + jnp.log(1.0 + jnp.exp(-lam_abs))
    a_parts, u_parts = [], []
    for gi in range(N_LRU_GROUPS):
        cols = slice(gi * LRU_GROUP, (gi + 1) * LRU_GROUP)
        xg = xc[:, cols]
        proj = _dot(xg.astype(BF16), wg_ref[gi])
        r = _sigmoid(proj[:, :LRU_GROUP] + ba[:, cols])
        i = _sigmoid(proj[:, LRU_GROUP:] + bi[:, cols])
        log_a = -LRU_C * r * softplus_neg[:, cols]
        a = jnp.exp(log_a)
        a_parts.append(a)
        u_parts.append(jnp.sqrt(1.0 - a * a) * (i * xg))
    return a_parts, u_parts


def _lru_seq_kernel(gate_ref, xr_ref, cw_ref, cb_ref, wg_ref, ba_ref, bi_ref, lam_ref,
                    y_ref, hlast_ref, ext_ref, a_ref, u_ref, carry_ref):
    tt = xr_ref.shape[0]
    n_grp = tt // SUBLANES

    @pl.when(pl.program_id(1) == 0)
    def _():
        ext_ref[0:SUBLANES, :] = jnp.zeros((SUBLANES, D_RNN), F32)
        carry_ref[...] = jnp.zeros((1, D_RNN), F32)

    xr = xr_ref[...]
    ext_ref[SUBLANES:, :] = xr
    cw = cw_ref[...]
    xc = cb_ref[...] + xr * cw[CONV_A - 1:CONV_A]
    for j in range(CONV_A - 1):
        off = SUBLANES - (CONV_A - 1) + j
        xc = xc + ext_ref[off:off + tt, :] * cw[j:j + 1]
    ext_ref[0:SUBLANES, :] = xr[tt - SUBLANES:, :]

    a_parts, u_parts = _lru_gates(xc, wg_ref, ba_ref[...], bi_ref[...], lam_ref[...])
    for gi in range(N_LRU_GROUPS):
        cols = slice(gi * LRU_GROUP, (gi + 1) * LRU_GROUP)
        a_ref[:, :, cols] = a_parts[gi].reshape(n_grp, SUBLANES, LRU_GROUP)
        u_ref[:, :, cols] = u_parts[gi].reshape(n_grp, SUBLANES, LRU_GROUP)

    sub = lax.broadcasted_iota(jnp.int32, (n_grp, SUBLANES, LANES), 1)
    for ci in range(D_RNN // LANES):
        cols = slice(ci * LANES, (ci + 1) * LANES)
        a3 = a_ref[:, :, cols]
        u3 = u_ref[:, :, cols]
        d = 1
        while d < SUBLANES:
            keep = sub >= d
            a_sh = jnp.where(keep, pltpu.roll(a3, d, 1), 1.0)
            u_sh = jnp.where(keep, pltpu.roll(u3, d, 1), 0.0)
            u3 = a3 * u_sh + u3
            a3 = a3 * a_sh
            d *= 2
        h_prev = jnp.broadcast_to(carry_ref[:, cols], (SUBLANES, LANES))
        for g in range(0, n_grp, 2):
            h0 = a3[g] * h_prev + u3[g]
            h_prev = jnp.broadcast_to(h0[SUBLANES - 1:, :], (SUBLANES, LANES))
            h1 = a3[g + 1] * h_prev + u3[g + 1]
            h_prev = jnp.broadcast_to(h1[SUBLANES - 1:, :], (SUBLANES, LANES))
            rows = slice(g * SUBLANES, (g + 2) * SUBLANES)
            h = jnp.concatenate([h0, h1], axis=0)
            y_ref[rows, cols] = (gate_ref[rows, cols].astype(F32) * h).astype(BF16)
        carry_ref[:, cols] = h_prev[0:1, :]
        hlast_ref[:, cols] = h_prev[0:1, :]


def lru_seq(gate_act, xr, conv_w, conv_b, wg, b_a, b_i, lam, *, batch, seq_len, tt):
    m = xr.shape[0]
    nt = seq_len // tt
    vec = lambda: pl.BlockSpec((1, D_RNN), lambda b, t: (0, 0))
    y, h_last = pl.pallas_call(
        _lru_seq_kernel,
        grid=(batch, nt),
        in_specs=[
            pl.BlockSpec((tt, D_RNN), lambda b, t: (b * nt + t, 0)),
            pl.BlockSpec((tt, D_RNN), lambda b, t: (b * nt + t, 0)),
            pl.BlockSpec((CONV_A, D_RNN), lambda b, t: (0, 0)),
            vec(),
            pl.BlockSpec((N_LRU_GROUPS, LRU_GROUP, 2 * LRU_GROUP), lambda b, t: (0, 0, 0)),
            vec(), vec(), vec(),
        ],
        out_specs=[
            pl.BlockSpec((tt, D_RNN), lambda b, t: (b * nt + t, 0)),
            pl.BlockSpec((None, 1, D_RNN), lambda b, t: (b, 0, 0)),
        ],
        out_shape=[jax.ShapeDtypeStruct((m, D_RNN), BF16), jax.ShapeDtypeStruct((batch, 1, D_RNN), F32)],
        scratch_shapes=[pltpu.VMEM((tt + SUBLANES, D_RNN), F32), pltpu.VMEM((tt // SUBLANES, SUBLANES, D_RNN), F32),
                        pltpu.VMEM((tt // SUBLANES, SUBLANES, D_RNN), F32), pltpu.VMEM((1, D_RNN), F32)],
        compiler_params=_params(2),
        name="lru_seq",
    )(gate_act, xr, conv_w, conv_b.reshape(1, D_RNN), wg, b_a.reshape(1, D_RNN), b_i.reshape(1, D_RNN),
      lam.reshape(1, D_RNN))
    return y, h_last.reshape(batch, D_RNN)


def _lru_step_kernel(gate_ref, xr_ref, h0_ref, cbuf_ref, cw_ref, cb_ref, wg_ref, ba_ref, bi_ref, lam_ref,
                     y_ref, h_ref):
    xr = xr_ref[...]
    cw = cw_ref[...]
    xc = cb_ref[...] + xr * cw[CONV_A - 1:CONV_A]
    for j in range(CONV_A - 1):
        xc = xc + cbuf_ref[j] * cw[j:j + 1]
    a_parts, u_parts = _lru_gates(xc, wg_ref, ba_ref[...], bi_ref[...], lam_ref[...])
    a = jnp.concatenate(a_parts, axis=1)
    u = jnp.concatenate(u_parts, axis=1)
    h = u + a * h0_ref[...]
    h_ref[...] = h
    y_ref[...] = (gate_ref[...].astype(F32) * h).astype(BF16)


def lru_step(gate_act, xr, h0, cbuf, conv_w, conv_b, wg, b_a, b_i, lam):
    m = xr.shape[0]
    return pl.pallas_call(
        _lru_step_kernel,
        out_shape=[jax.ShapeDtypeStruct((m, D_RNN), BF16), jax.ShapeDtypeStruct((m, D_RNN), F32)],
        compiler_params=pltpu.CompilerParams(vmem_limit_bytes=V7X_VMEM_LIMIT),
        name="lru_step",
    )(gate_act, xr, h0, cbuf, conv_w, conv_b.reshape(1, D_RNN), wg, b_a.reshape(1, D_RNN), b_i.reshape(1, D_RNN),
      lam.reshape(1, D_RNN))


def _lru_gate_weights(w_a, w_i):
    per = LRU_GROUP // LRU_BLOCK
    row_blk = np.arange(LRU_GROUP)[:, None] // LRU_BLOCK
    col_blk = np.arange(2 * LRU_GROUP)[None, :] % LRU_GROUP // LRU_BLOCK
    on_diagonal = jnp.asarray(row_blk == col_blk)
    rows = lambda w: w.astype(BF16).reshape(N_LRU_GROUPS, LRU_GROUP, LRU_BLOCK)
    tiled = jnp.concatenate([jnp.tile(rows(w_a), (1, 1, per)), jnp.tile(rows(w_i), (1, 1, per))], axis=2)
    return jnp.where(on_diagonal, tiled, jnp.zeros((), BF16))


def _rope_tables(pos):
    half = ROT_DIM // 2
    inv = ROPE_THETA ** (-jnp.arange(half, dtype=F32) * (2.0 / ROT_DIM))
    ang = pos[:, None] * inv[None, :]
    cos, sin = jnp.cos(ang), jnp.sin(ang)
    rows = pos.shape[0]
    ones = jnp.ones((rows, HEAD_DIM - ROT_DIM), F32)
    zeros_h = jnp.zeros((rows, half), F32)
    zeros_t = jnp.zeros((rows, HEAD_DIM - ROT_DIM), F32)
    c = jnp.concatenate([cos, cos, ones], axis=1)
    s_lo = jnp.concatenate([-sin, zeros_h, zeros_t], axis=1)
    s_hi = jnp.concatenate([zeros_h, sin, zeros_t], axis=1)
    rep = LANES // HEAD_DIM
    return tuple(jnp.tile(t, (1, rep)) for t in (c, s_lo, s_hi))


def _rope_lanes(x, tables):
    c, s_lo, s_hi = tables
    half = ROT_DIM // 2
    return x * c + pltpu.roll(x, LANES - half, 1) * s_lo + pltpu.roll(x, half, 1) * s_hi


def _rope_kernel(qk_ref, c_ref, slo_ref, shi_ref, q_ref, k_ref):
    tables = (c_ref[...], slo_ref[...], shi_ref[...])
    n_q = N_HEADS * HEAD_DIM // LANES
    for ci in range(QK_COLS // LANES):
        rot = _rope_lanes(qk_ref[:, ci * LANES:(ci + 1) * LANES], tables)
        if ci < n_q:
            q_ref[:, ci * LANES:(ci + 1) * LANES] = rot.astype(BF16)
        else:
            k_ref[:, (ci - n_q) * LANES:(ci - n_q + 1) * LANES] = rot


def rope(qkv, tables, *, tm):
    m = qkv.shape[0]
    nt = tables[0].shape[0] // tm
    tab = lambda: pl.BlockSpec((tm, LANES), lambda i: (i % nt, 0))
    return pl.pallas_call(
        _rope_kernel,
        grid=(m // tm,),
        in_specs=[pl.BlockSpec((tm, QK_COLS), lambda i: (i, 0)), tab(), tab(), tab()],
        out_specs=[pl.BlockSpec((tm, N_HEADS * HEAD_DIM), lambda i: (i, 0)),
                   pl.BlockSpec((tm, KV_COLS), lambda i: (i, 0))],
        out_shape=[jax.ShapeDtypeStruct((m, N_HEADS * HEAD_DIM), BF16), jax.ShapeDtypeStruct((m, KV_COLS), F32)],
        compiler_params=_params(1),
        name="rope",
    )(qkv, *tables)


def _swa_seq_kernel(sink_ref, q_ref, kp_ref, kc_ref, vp_ref, vc_ref, cp_ref, slop_ref, ship_ref,
                    cc_ref, sloc_ref, shic_ref, o_ref, krot_ref):
    qi = pl.program_id(1)
    tq = q_ref.shape[0]
    nk = 2 * tq
    rope_prev = (cp_ref[...], slop_ref[...], ship_ref[...])
    rope_cur = (cc_ref[...], sloc_ref[...], shic_ref[...])
    lane_blocks = [slice(i * LANES, (i + 1) * LANES) for i in range(KV_COLS // LANES)]
    k_prev = jnp.concatenate([_rope_lanes(kp_ref[:, blk], rope_prev) for blk in lane_blocks], axis=1)
    k_cur = jnp.concatenate([_rope_lanes(kc_ref[:, blk], rope_cur) for blk in lane_blocks], axis=1)
    krot_ref[...] = k_cur
    kk = jnp.concatenate([k_prev, k_cur], axis=0) * (HEAD_DIM ** -0.5)
    vv = jnp.concatenate([vp_ref[...], vc_ref[...]], axis=0).astype(BF16)
    key = lax.broadcasted_iota(jnp.int32, (nk, tq), 0)
    qry = lax.broadcasted_iota(jnp.int32, (nk, tq), 1)
    rel = tq + qry - key
    valid = (rel >= 0) & (rel < WINDOW) & ((qi > 0) | (key >= tq))
    low_k = lax.broadcasted_iota(jnp.int32, (nk, LANES), 1) < HEAD_DIM
    low_o = lax.broadcasted_iota(jnp.int32, (tq, LANES), 1) < HEAD_DIM
    for kh in range(N_KV):
        blk = slice((kh // 2) * LANES, (kh // 2 + 1) * LANES)
        in_low = kh % 2 == 0
        k_blk = kk[:, blk]
        k_swap = pltpu.roll(k_blk, HEAD_DIM, 1)
        k_lo = jnp.where(low_k, k_blk if in_low else k_swap, 0.0).astype(BF16)
        k_hi = jnp.where(low_k, 0.0, k_swap if in_low else k_blk).astype(BF16)
        v_blk = vv[:, blk]
        for hp in range(GROUP // 2):
            h0 = kh * GROUP + 2 * hp
            hcols = slice(h0 * HEAD_DIM, (h0 + 2) * HEAD_DIM)
            q_pair = _rope_lanes(q_ref[:, hcols], rope_cur).astype(BF16)
            outs = []
            for which, k_pad in enumerate((k_lo, k_hi)):
                s = jnp.where(valid, _dot_nt(k_pad, q_pair), -jnp.inf)
                sink = sink_ref[h0 + which]
                mx = jnp.maximum(jnp.max(s, axis=0, keepdims=True), sink)
                e = jnp.exp(s - mx)
                denom = jnp.sum(e, axis=0, keepdims=True) + jnp.exp(sink - mx)
                p = (e * (1.0 / denom)).astype(BF16)
                outs.append(_dot_tn(p, v_blk))
            if in_low:
                o_pair = jnp.where(low_o, outs[0], pltpu.roll(outs[1], HEAD_DIM, 1))
            else:
                o_pair = jnp.where(low_o, pltpu.roll(outs[0], HEAD_DIM, 1), outs[1])
            o_ref[:, hcols] = o_pair.astype(BF16)


def swa_seq(qkv, rope_tables, sinks, *, batch, seq_len):
    m = qkv.shape[0]
    tq = WINDOW
    nq = seq_len // tq
    k_col = N_HEADS * HEAD_DIM // KV_COLS
    v_col = QK_COLS // KV_COLS
    prev = lambda col: pl.BlockSpec((tq, KV_COLS), lambda b, i: (b * nq + jnp.maximum(i - 1, 0), col))
    cur = lambda col: pl.BlockSpec((tq, KV_COLS), lambda b, i: (b * nq + i, col))
    tab_prev = lambda: pl.BlockSpec((tq, LANES), lambda b, i: (jnp.maximum(i - 1, 0), 0))
    tab_cur = lambda: pl.BlockSpec((tq, LANES), lambda b, i: (i, 0))
    return pl.pallas_call(
        _swa_seq_kernel,
        grid=(batch, nq),
        in_specs=[
            pl.BlockSpec(memory_space=pltpu.SMEM),
            pl.BlockSpec((tq, N_HEADS * HEAD_DIM), lambda b, i: (b * nq + i, 0)),
            prev(k_col), cur(k_col), prev(v_col), cur(v_col),
            tab_prev(), tab_prev(), tab_prev(), tab_cur(), tab_cur(), tab_cur(),
        ],
        out_specs=[pl.BlockSpec((tq, N_HEADS * HEAD_DIM), lambda b, i: (b * nq + i, 0)),
                   pl.BlockSpec((tq, KV_COLS), lambda b, i: (b * nq + i, 0))],
        out_shape=[jax.ShapeDtypeStruct((m, N_HEADS * HEAD_DIM), BF16), jax.ShapeDtypeStruct((m, KV_COLS), F32)],
        compiler_params=_params(2),
        name="swa_seq",
    )(sinks, qkv, qkv, qkv, qkv, qkv, *rope_tables, *rope_tables)


def _swa_step_kernel(q_ref, kn_ref, vn_ref, ck_ref, cv_ref, sink_ref, o_ref):
    col = lax.broadcasted_iota(jnp.int32, (GROUP, WINDOW), 1)
    scale = HEAD_DIM ** -0.5
    items = [(bi, kh) for bi in range(q_ref.shape[0]) for kh in range(N_KV)]
    heads = lambda kh: slice(kh * GROUP, (kh + 1) * GROUP)
    kcols = lambda kh: slice(kh * HEAD_DIM, (kh + 1) * HEAD_DIM)
    scores = []
    for bi, kh in items:
        q = q_ref[bi, heads(kh), :]
        k_new = kn_ref[bi, kh:kh + 1, :].astype(BF16).astype(F32)
        s_c = _dot_nt(q, ck_ref[bi, :, kcols(kh)].astype(BF16)) * scale
        s_c = jnp.where(col >= 1, s_c, -jnp.inf)
        s_n = jnp.sum(q.astype(F32) * k_new, axis=-1, keepdims=True) * scale
        scores.append((s_c, s_n))
    probs = []
    for (bi, kh), (s_c, s_n) in zip(items, scores):
        sink = sink_ref[heads(kh), :]
        mx = jnp.maximum(jnp.maximum(jnp.max(s_c, axis=-1, keepdims=True), s_n), sink)
        p_c = jnp.exp(s_c - mx)
        p_n = jnp.exp(s_n - mx)
        denom = jnp.sum(p_c, axis=-1, keepdims=True) + p_n + jnp.exp(sink - mx)
        probs.append((p_c.astype(BF16), p_n.astype(BF16).astype(F32), denom))
    for (bi, kh), (p_c, p_n, denom) in zip(items, probs):
        v_new = vn_ref[bi, kh:kh + 1, :].astype(BF16).astype(F32)
        o = _dot(p_c, cv_ref[bi, :, kcols(kh)].astype(BF16)) + p_n * v_new
        o_ref[bi, heads(kh), :] = (o / denom).astype(BF16)


def swa_step(q3, k_new, v_new, cache_k, cache_v, sinks, *, seqs_per_step=8):
    b = q3.shape[0]
    nb = seqs_per_step
    return pl.pallas_call(
        _swa_step_kernel,
        grid=(b // nb,),
        in_specs=[
            pl.BlockSpec((nb, N_HEADS, HEAD_DIM), lambda i: (i, 0, 0)),
            pl.BlockSpec((nb, N_KV, HEAD_DIM), lambda i: (i, 0, 0)),
            pl.BlockSpec((nb, N_KV, HEAD_DIM), lambda i: (i, 0, 0)),
            pl.BlockSpec((nb, WINDOW, KV_COLS), lambda i: (i, 0, 0)),
            pl.BlockSpec((nb, WINDOW, KV_COLS), lambda i: (i, 0, 0)),
            pl.BlockSpec((N_HEADS, 1), lambda i: (0, 0)),
        ],
        out_specs=pl.BlockSpec((nb, N_HEADS, HEAD_DIM), lambda i: (i, 0, 0)),
        out_shape=jax.ShapeDtypeStruct((b, N_HEADS, HEAD_DIM), BF16),
        compiler_params=_params(1),
        name="swa_step",
    )(q3, k_new, v_new, cache_k, cache_v, sinks.reshape(N_HEADS, 1))


def _hgrn_lower_bound(logits, layer):
    mx = jnp.max(logits, axis=0, keepdims=True)
    e = jnp.exp(logits - mx)
    sm = e / jnp.sum(e, axis=0, keepdims=True)
    lb = jnp.zeros_like(sm[0:1])
    for i in range(1, layer + 1):
        lb = lb + sm[i:i + 1]
    return lb


def _hgrn_consts():
    c = HG_CHUNK
    t = np.arange(c)[:, None]
    s = np.arange(c)[None, :]
    tri = (s <= t).astype(np.float32)
    sel = [tri]
    msk = [(s == t)]
    for lvl in range(1, HG_LEVELS + 1):
        w = 1 << (lvl - 1)
        ref_row = (t // (2 * w)) * (2 * w) + w - 1
        upper_t = (t % (2 * w)) >= w
        lower_s = (s % (2 * w)) < w
        sign = np.where(upper_t, 1.0, -1.0)
        sel.append(sign * (tri - (s <= ref_row)))
        msk.append(((t // (2 * w)) == (s // (2 * w))) & upper_t & lower_s)
    sel.append(1.0 - tri)
    sel = np.concatenate(sel, axis=0).astype(np.float32)
    sel2 = np.concatenate([sel, sel], axis=1)
    msk = np.stack(msk, axis=0).astype(np.float32)
    return jnp.asarray(sel2, BF16), jnp.asarray(msk, F32)


def _hgrn_seq_kernel(q_ref, z_ref, v_ref, g_ref, lbl_ref, gn_ref, sel_ref, msk_ref, y_ref, s_out_ref, st_ref,
                     *, layer, heads_per_step):
    ti = pl.program_id(2)
    tt = q_ref.shape[0]
    c = HG_CHUNK

    @pl.when(ti == 0)
    def _():
        st_ref[...] = jnp.zeros(st_ref.shape, F32)

    lb_all = _hgrn_lower_bound(lbl_ref[...], layer)
    gn = gn_ref[...]
    row = lax.broadcasted_iota(jnp.int32, (c, heads_per_step * HG_DK), 0)
    upper = [None] + [((row >> (lvl - 1)) & 1) == 1 for lvl in range(1, HG_LEVELS + 1)]
    states = [st_ref[hh] for hh in range(heads_per_step)]

    head_cols = [slice(hh * HG_DK, (hh + 1) * HG_DK) for hh in range(heads_per_step)]

    def decays(ci):
        rows = slice(ci * c, (ci + 1) * c)
        q = q_ref[rows, :]
        sq = q * _sigmoid(q)
        sz = _sigmoid(z_ref[rows, :])
        log_f = jnp.log(lb_all + (1.0 - lb_all) * sz)
        k = (1.0 - lb_all) * (1.0 - sz)
        hi = log_f.astype(BF16)
        lo = (log_f - hi.astype(F32)).astype(BF16)
        ex = _dot(sel_ref[...], jnp.concatenate([hi, lo], axis=0))
        e_b = jnp.exp(ex[0:c])
        xs = [(jnp.where(upper[lvl], sq, k) * jnp.exp(ex[lvl * c:(lvl + 1) * c])).astype(BF16)
              for lvl in range(1, HG_LEVELS + 1)]
        return dict(rows=rows, v=v_ref[rows, :].astype(BF16), sq=sq.astype(BF16), k=k.astype(BF16), xs=xs,
                    q_dec=(sq * e_b).astype(BF16), k_dec=(k * jnp.exp(ex[(HG_LEVELS + 1) * c:])).astype(BF16),
                    e_last=e_b[c - 1:c, :])

    def scores(d):
        atts = []
        for cols in head_cols:
            att = _dot_nt(d["sq"][:, cols], d["k"][:, cols]) * msk_ref[0]
            for lvl in range(1, HG_LEVELS + 1):
                x = d["xs"][lvl - 1][:, cols]
                att = att + _dot_nt(x, x) * msk_ref[lvl]
            atts.append(att.astype(BF16))
        return atts

    def outputs(d, atts):
        outs = []
        for hh, cols in enumerate(head_cols):
            st = states[hh]
            o = _dot(atts[hh], d["v"][:, cols]) + _dot_nt(d["q_dec"][:, cols], st.astype(BF16))
            states[hh] = st * d["e_last"][:, cols] + _dot_tn(d["v"][:, cols], d["k_dec"][:, cols])
            outs.append(_rms_rows(o, gn))
        g = g_ref[d["rows"], :]
        y_ref[d["rows"], :] = (jnp.concatenate(outs, axis=1) * (g * _sigmoid(g))).astype(BF16)

    n_chunks = tt // c
    stage_a, stage_b = {}, {}
    for step in range(n_chunks + 2):
        if step < n_chunks:
            stage_a[step] = decays(step)
        if 0 <= step - 1 < n_chunks:
            stage_b[step - 1] = scores(stage_a[step - 1])
        if 0 <= step - 2 < n_chunks:
            outputs(stage_a.pop(step - 2), stage_b.pop(step - 2))

    for hh in range(heads_per_step):
        st_ref[hh] = states[hh]

    @pl.when(ti == pl.num_programs(2) - 1)
    def _():
        for hh in range(heads_per_step):
            s_out_ref[hh] = states[hh].T


def hgrn_seq(qzvg, lb_logits, g_norm, *, layer, batch, seq_len, tt, heads_per_step):
    m = qzvg.shape[0]
    nt = seq_len // tt
    hw = heads_per_step * HG_DK
    nh = HG_HEADS // heads_per_step
    sel, msk = _hgrn_consts()
    part = lambda p: pl.BlockSpec((tt, hw), lambda b, h, t: (b * nt + t, p * nh + h))
    y, s_out = pl.pallas_call(
        functools.partial(_hgrn_seq_kernel, layer=layer, heads_per_step=heads_per_step),
        grid=(batch, nh, nt),
        in_specs=[
            part(0), part(1), part(2), part(3),
            pl.BlockSpec((DEPTH, hw), lambda b, h, t: (0, h)),
            pl.BlockSpec((1, HG_DV), lambda b, h, t: (0, 0)),
            pl.BlockSpec(sel.shape, lambda b, h, t: (0, 0)),
            pl.BlockSpec(msk.shape, lambda b, h, t: (0, 0, 0)),
        ],
        out_specs=[
            pl.BlockSpec((tt, hw), lambda b, h, t: (b * nt + t, h)),
            pl.BlockSpec((None, heads_per_step, HG_DK, HG_DV), lambda b, h, t: (b, h, 0, 0)),
        ],
        out_shape=[jax.ShapeDtypeStruct((m, HG_HEADS * HG_DV), BF16),
                   jax.ShapeDtypeStruct((batch, HG_HEADS, HG_DK, HG_DV), F32)],
        scratch_shapes=[pltpu.VMEM((heads_per_step, HG_DV, HG_DK), F32)],
        compiler_params=_params(3),
        name="hgrn_seq",
    )(qzvg, qzvg, qzvg, qzvg, lb_logits, g_norm.reshape(1, HG_DV), sel, msk)
    return y, s_out


def _hgrn_step_kernel(x_ref, s_ref, lbl_ref, gn_ref, y_ref, s_out_ref, *, layer):
    nh = HG_HEADS
    lb = _hgrn_lower_bound(lbl_ref[...], layer)[0]
    pad = jnp.zeros((LANES - 3 * nh, HG_DK), F32)
    for bi in range(x_ref.shape[0]):
        q = x_ref[bi, 0:nh, :]
        z = x_ref[bi, nh:2 * nh, :]
        v = x_ref[bi, 2 * nh:3 * nh, :]
        g = x_ref[bi, 3 * nh:4 * nh, :]
        sq = q * _sigmoid(q)
        sz = _sigmoid(z)
        f = lb + (1.0 - lb) * sz
        k = (1.0 - lb) * (1.0 - sz)
        cols = jnp.concatenate([f, k, sq, pad], axis=0).T
        outs = []
        for h in range(nh):
            f_col = cols[:, h:h + 1]
            k_col = cols[:, nh + h:nh + h + 1]
            q_col = cols[:, 2 * nh + h:2 * nh + h + 1]
            s_new = s_ref[bi, h] * f_col + k_col * v[h:h + 1, :]
            s_out_ref[bi, h] = s_new
            outs.append(jnp.sum(s_new * q_col, axis=0, keepdims=True))
        o = jnp.concatenate(outs, axis=0)
        y_ref[bi] = (_rms_rows(o, gn_ref[...]) * (g * _sigmoid(g))).astype(BF16)


def hgrn_step(x4, state, lb_logits, g_norm, *, layer, seqs_per_step=4):
    b = x4.shape[0]
    nb = seqs_per_step
    return pl.pallas_call(
        functools.partial(_hgrn_step_kernel, layer=layer),
        grid=(b // nb,),
        in_specs=[
            pl.BlockSpec((nb, 4 * HG_HEADS, HG_DK), lambda i: (i, 0, 0)),
            pl.BlockSpec((nb, HG_HEADS, HG_DK, HG_DV), lambda i: (i, 0, 0, 0)),
            pl.BlockSpec((DEPTH, HG_HEADS, HG_DK), lambda i: (0, 0, 0)),
            pl.BlockSpec((1, HG_DV), lambda i: (0, 0)),
        ],
        out_specs=[
            pl.BlockSpec((nb, HG_HEADS, HG_DV), lambda i: (i, 0, 0)),
            pl.BlockSpec((nb, HG_HEADS, HG_DK, HG_DV), lambda i: (i, 0, 0, 0)),
        ],
        out_shape=[jax.ShapeDtypeStruct((b, HG_HEADS, HG_DV), BF16),
                   jax.ShapeDtypeStruct((b, HG_HEADS, HG_DK, HG_DV), F32)],
        compiler_params=_params(1),
        name="hgrn_step",
    )(x4, state, lb_logits.reshape(DEPTH, HG_HEADS, HG_DK), g_norm.reshape(1, HG_DV))


def _trunk(x3, xs3, states, w):
    batch, seq_len, _ = x3.shape
    m = batch * seq_len
    ns = xs3.shape[0]
    x = x3.reshape(m, D_MODEL)
    xs = xs3.reshape(1, ns, D_MODEL)
    tm = seq_len
    tn_in, tn_out, tf = 512, 256, 512
    zeros = lambda n: jnp.zeros((n,), F32)
    lru_wg = [_lru_gate_weights(w["lru_w_a"][j], w["lru_w_i"][j]) for j in range(w["lru_w_a"].shape[0])]
    rope_seq = _rope_tables(jnp.arange(seq_len, dtype=F32))
    rope_step = _rope_tables(jnp.full((ns,), PAST_LEN, F32))

    def out_proj(a, a_s, w_stack, li, b, res, res_s):
        k_blocks = 1
        while tm * (a.shape[1] // k_blocks) * a.dtype.itemsize >= SINGLE_BUFFER_BYTES:
            k_blocks *= 2
        k = a.shape[1] // k_blocks
        cap = max(LANES, min(tn_out, W_TILE_BYTES // (4 * k) // LANES * LANES))
        for ki in range(k_blocks):
            bias = b if ki == k_blocks - 1 else jnp.zeros_like(b)
            res, res_s = matmul_residual(a, a_s, w_stack, li, bias, res, res_s, tm=tm,
                                         tn=_largest_tile(D_MODEL, cap), k_blocks=k_blocks, k_index=ki)
        return res, res_s

    new_p = {"lru_h": [], "lru_conv": [], "swa_k": [], "swa_v": [], "hgrn": [], "ffn_conv": []}
    new_s = {"lru_h": [], "lru_conv": [], "swa_k": [], "swa_v": [], "hgrn": [], "ffn_conv": []}
    for layer in range(DEPTH):
        kind, j = LAYER_MIXER[layer], LAYER_SLOT[layer]
        g_mix = w["norm_mix"][layer]
        if kind == 0:
            gate, gate_s, xr, xr_s = norm_matmul_gated(x, xs, g_mix, w["lru_w_in"], j, tm=tm, tn=tn_in // 2)
            xr_s = xr_s[0]
            lru_w = (w["lru_conv_w"][j], w["lru_conv_b"][j], lru_wg[j], w["lru_b_a"][j], w["lru_b_i"][j],
                     w["lru_lambda"][j])
            y, h_new = lru_seq(gate, xr, *lru_w, batch=batch, seq_len=seq_len, tt=256)
            new_p["lru_h"].append(h_new)
            new_p["lru_conv"].append(xr.reshape(batch, seq_len, D_RNN)[:, seq_len - (CONV_A - 1):])
            cbuf = states["lru_conv"][j]
            ys, hs_new = lru_step(gate_s[0], xr_s, states["lru_h"][j], jnp.swapaxes(cbuf, 0, 1), *lru_w)
            new_s["lru_h"].append(hs_new)
            new_s["lru_conv"].append(jnp.concatenate([cbuf[:, 1:], xr_s[:, None]], axis=1))
            x, xs = out_proj(y, ys[None], w["lru_w_out"], j, zeros(D_MODEL), x, xs)
        elif kind == 1:
            n = QK_COLS + KV_COLS
            qkv, qkvs = norm_matmul(x, xs, g_mix, w["swa_w_qkv"], j, w["swa_b_qkv"][j], tm=tm,
                                    tn=_largest_tile(n, tn_in))
            qkvs = qkvs[0]
            o, k_rot = swa_seq(qkv, rope_seq, w["swa_sinks"][j], batch=batch, seq_len=seq_len)
            k_win = k_rot.reshape(batch, seq_len, KV_COLS)[:, seq_len - WINDOW:]
            v_win = qkv.reshape(batch, seq_len, n)[:, seq_len - WINDOW:, QK_COLS:]
            new_p["swa_k"].append(k_win.reshape(batch, WINDOW, N_KV, HEAD_DIM))
            new_p["swa_v"].append(v_win.reshape(batch, WINDOW, N_KV, HEAD_DIM))
            qs_rot, ks_rot = rope(qkvs, rope_step, tm=ns)
            ck = states["swa_k"][j]
            cv = states["swa_v"][j]
            k_new = ks_rot.reshape(ns, N_KV, HEAD_DIM)
            v_new = qkvs[:, QK_COLS:].reshape(ns, N_KV, HEAD_DIM)
            os_ = swa_step(qs_rot.reshape(ns, N_HEADS, HEAD_DIM), k_new, v_new,
                           ck.reshape(ns, WINDOW, KV_COLS), cv.reshape(ns, WINDOW, KV_COLS), w["swa_sinks"][j])
            new_s["swa_k"].append(jnp.concatenate([ck[:, 1:], k_new[:, None]], axis=1))
            new_s["swa_v"].append(jnp.concatenate([cv[:, 1:], v_new[:, None]], axis=1))
            x, xs = out_proj(o, os_.reshape(1, ns, N_HEADS * HEAD_DIM), w["swa_w_o"], j, w["swa_b_o"][j], x, xs)
        else:
            n = 2 * HG_HEADS * HG_DK + 2 * HG_HEADS * HG_DV
            qzvg, qzvgs = norm_matmul(x, xs, g_mix, w["hg_w_in"], j, zeros(n), tm=tm, tn=_largest_tile(n, tn_in))
            y, s_new = hgrn_seq(qzvg, w["hg_lb_logits"], w["hg_norm"][j], layer=layer, batch=batch,
                                seq_len=seq_len, tt=1024, heads_per_step=4)
            new_p["hgrn"].append(s_new)
            ys, ss_new = hgrn_step(qzvgs[0].reshape(ns, 4 * HG_HEADS, HG_DK), states["hgrn"][j], w["hg_lb_logits"],
                                   w["hg_norm"][j], layer=layer)
            new_s["hgrn"].append(ss_new)
            x, xs = out_proj(y, ys.reshape(1, ns, HG_HEADS * HG_DV), w["hg_w_o"], j, zeros(D_MODEL), x, xs)

        fbuf = states["ffn_conv"][layer]
        act, act_s, tail, gate_s = ffn_in(x, xs, w["norm_ffn"][layer], w["ffn_w_in"], w["ffn_conv_w"],
                                          w["ffn_conv_b"], layer, jnp.swapaxes(fbuf, 0, 1), tm=tm, tf=tf)
        new_p["ffn_conv"].append(tail[:, SUBLANES - (CONV_F - 1):])
        new_s["ffn_conv"].append(jnp.concatenate([fbuf[:, 1:], gate_s[0][:, None]], axis=1))
        x, xs = out_proj(act, act_s, w["ffn_w_out"], layer, zeros(D_MODEL), x, xs)

    y = final_norm(x, w["norm_final"], tm=1024).reshape(batch, seq_len, D_MODEL)
    ys = final_norm(xs[0], w["norm_final"], tm=ns).reshape(ns, 1, D_MODEL)
    order = ("lru_h", "lru_conv", "swa_k", "swa_v", "hgrn", "ffn_conv")
    return ((y, ys) + tuple(jnp.stack(new_p[k]) for k in order) + tuple(jnp.stack(new_s[k]) for k in order))


def kernel(x_prompt, x_sample, state_lru_h, state_lru_conv, cache_swa_k, cache_swa_v, state_hgrn, state_ffn_conv,
           norm_mix, norm_ffn, norm_final,
           lru_w_in, lru_conv_w, lru_conv_b, lru_w_a, lru_b_a, lru_w_i, lru_b_i, lru_lambda, lru_w_out,
           swa_w_qkv, swa_b_qkv, swa_sinks, swa_w_o, swa_b_o,
           hg_w_in, hg_lb_logits, hg_norm, hg_w_o,
           ffn_w_in, ffn_conv_w, ffn_conv_b, ffn_w_out):
    w = dict(norm_mix=norm_mix, norm_ffn=norm_ffn, norm_final=norm_final,
             lru_w_in=lru_w_in, lru_conv_w=lru_conv_w, lru_conv_b=lru_conv_b, lru_w_a=lru_w_a, lru_b_a=lru_b_a,
             lru_w_i=lru_w_i, lru_b_i=lru_b_i, lru_lambda=lru_lambda, lru_w_out=lru_w_out,
             swa_w_qkv=swa_w_qkv, swa_b_qkv=swa_b_qkv, swa_sinks=swa_sinks, swa_w_o=swa_w_o, swa_b_o=swa_b_o,
             hg_w_in=hg_w_in, hg_lb_logits=hg_lb_logits, hg_norm=hg_norm, hg_w_o=hg_w_o,
             ffn_w_in=ffn_w_in, ffn_conv_w=ffn_conv_w, ffn_conv_b=ffn_conv_b, ffn_w_out=ffn_w_out)
    states = dict(lru_h=state_lru_h, lru_conv=state_lru_conv, swa_k=cache_swa_k, swa_v=cache_swa_v,
                  hgrn=state_hgrn, ffn_conv=state_ffn_conv)
    return _trunk(x_prompt, x_sample, states, w)
```

```python
import functools

import numpy as np
import jax
import jax.numpy as jnp
from jax import lax
from jax.experimental import pallas as pl
from jax.experimental.pallas import tpu as pltpu

F32 = jnp.float32
BF16 = jnp.bfloat16

D_MODEL = 2048
DEPTH = 4
PAST_LEN = 16384
EPS = 1e-6
LAYER_MIXER = tuple(i % 3 for i in range(DEPTH))
LAYER_SLOT = tuple(LAYER_MIXER[:i].count(LAYER_MIXER[i]) for i in range(DEPTH))

D_RNN = 2560
LRU_BLOCKS = 16
LRU_BLOCK = D_RNN // LRU_BLOCKS
LRU_GROUP = 640
N_LRU_GROUPS = D_RNN // LRU_GROUP
CONV_A = 4
LRU_C = 8.0

N_HEADS = 32
N_KV = 4
HEAD_DIM = 64
GROUP = N_HEADS // N_KV
WINDOW = 128
ROT_DIM = HEAD_DIM // 4
ROPE_THETA = 500000.0
QK_COLS = (N_HEADS + N_KV) * HEAD_DIM
KV_COLS = N_KV * HEAD_DIM

HG_HEADS = 16
HG_DK = 128
HG_DV = 128
HG_CHUNK = 64
HG_LEVELS = 6

D_FF = 3 * D_MODEL
CONV_F = 3

LANES = 128
SUBLANES = 8
V7X_VMEM_LIMIT = 56 * 1024 * 1024
SINGLE_BUFFER_BYTES = 14 * 1024 * 1024
W_TILE_BYTES = 8 * 1024 * 1024
DOT_ROWS = 256


def _params(n_axes, vmem=V7X_VMEM_LIMIT):
    return pltpu.CompilerParams(dimension_semantics=("arbitrary",) * n_axes, vmem_limit_bytes=vmem)


def _sigmoid(x):
    return 1.0 / (1.0 + jnp.exp(-x))


def _rms_rows(x, g):
    var = jnp.mean(x * x, axis=-1, keepdims=True)
    return x * lax.rsqrt(var + EPS) * g


def _dot(a, b):
    return jnp.dot(a, b, preferred_element_type=F32)


def _dot_nt(a, b):
    return lax.dot_general(a, b, (((1,), (1,)), ((), ())), preferred_element_type=F32)


def _dot_tn(a, b):
    return lax.dot_general(a, b, (((0,), (0,)), ((), ())), preferred_element_type=F32)


def _largest_tile(n, cap):
    t = cap
    while n % t:
        t -= LANES
    return t


def _row_tile_spec(shape, index_map, dtype):
    if shape[0] * shape[1] * jnp.dtype(dtype).itemsize >= SINGLE_BUFFER_BYTES:
        return pl.BlockSpec(shape, index_map, pipeline_mode=pl.Buffered(1))
    return pl.BlockSpec(shape, index_map)


def _row_blocks(n_rows):
    step = min(n_rows, DOT_ROWS)
    return [slice(r, r + step) for r in range(0, n_rows, step)]


def _row_stream_scratch(tm, k):
    return [pltpu.VMEM((2, min(tm, DOT_ROWS), k), F32), pltpu.SemaphoreType.DMA((2,))]


def _for_normalized_blocks(x_hbm, xs_ref, g_ref, hn_ref, xbuf_ref, sem_ref, consume, on_first=None):
    tm = hn_ref.shape[0] - xs_ref.shape[0]
    blocks = _row_blocks(tm)

    def block_copy(b):
        rows = blocks[b]
        src = x_hbm.at[pl.ds(pl.program_id(0) * tm + rows.start, rows.stop - rows.start), :]
        return pltpu.make_async_copy(src, xbuf_ref.at[b % 2], sem_ref.at[b % 2])

    def run(normalize):
        if normalize:
            block_copy(0).start()
            if on_first is not None:
                on_first()
            hn_ref[tm:, :] = _rms_rows(xs_ref[...], g_ref[...]).astype(BF16)
        for b, rows in enumerate(blocks):
            if normalize:
                if b + 1 < len(blocks):
                    block_copy(b + 1).start()
                block_copy(b).wait()
                hn_ref[rows, :] = _rms_rows(xbuf_ref[b % 2], g_ref[...]).astype(BF16)
            lhs = hn_ref[rows.start:, :] if rows.stop == tm else hn_ref[rows, :]
            consume(rows, lhs, rows.stop - rows.start)

    first = pl.program_id(1) == 0
    pl.when(first)(functools.partial(run, True))
    pl.when(jnp.logical_not(first))(functools.partial(run, False))


def _norm_matmul_kernel(x_hbm, xs_ref, g_ref, w_ref, b_ref, o_ref, os_ref, hn_ref, xbuf_ref, sem_ref):
    w = w_ref[...].astype(BF16)

    def consume(rows, lhs, n):
        res = _dot(lhs, w) + b_ref[...]
        o_ref[rows, :] = res[:n]
        if res.shape[0] > n:
            os_ref[...] = res[n:]

    _for_normalized_blocks(x_hbm, xs_ref, g_ref, hn_ref, xbuf_ref, sem_ref, consume)


def norm_matmul(x, xs, g, w_stack, li, b, *, tm, tn):
    m, k = x.shape
    ns = xs.shape[1]
    n = w_stack.shape[2]
    return pl.pallas_call(
        _norm_matmul_kernel,
        grid=(m // tm, n // tn),
        in_specs=[
            pl.BlockSpec(memory_space=pl.ANY),
            pl.BlockSpec((None, ns, k), lambda i, j: (0, 0, 0)),
            pl.BlockSpec((1, k), lambda i, j: (0, 0)),
            pl.BlockSpec((None, k, tn), lambda i, j: (li, 0, j)),
            pl.BlockSpec((1, tn), lambda i, j: (0, j)),
        ],
        out_specs=[pl.BlockSpec((tm, tn), lambda i, j: (i, j)),
                   pl.BlockSpec((None, ns, tn), lambda i, j: (i, 0, j))],
        out_shape=[jax.ShapeDtypeStruct((m, n), F32), jax.ShapeDtypeStruct((m // tm, ns, n), F32)],
        scratch_shapes=[pltpu.VMEM((tm + ns, k), BF16), *_row_stream_scratch(tm, k)],
        compiler_params=_params(2),
        name="norm_matmul",
    )(x, xs, g.reshape(1, k), w_stack, b.reshape(1, n))


def _norm_matmul_gated_kernel(x_hbm, xs_ref, g_ref, wa_ref, wb_ref, act_ref, acts_ref, lin_ref, lins_ref,
                              hn_ref, xbuf_ref, sem_ref):
    wa = wa_ref[...].astype(BF16)
    wb = wb_ref[...].astype(BF16)

    def consume(rows, lhs, n):
        act = jax.nn.gelu(_dot(lhs, wa)).astype(BF16)
        lin = _dot(lhs, wb)
        act_ref[rows, :] = act[:n]
        lin_ref[rows, :] = lin[:n]
        if act.shape[0] > n:
            acts_ref[...] = act[n:]
            lins_ref[...] = lin[n:]

    _for_normalized_blocks(x_hbm, xs_ref, g_ref, hn_ref, xbuf_ref, sem_ref, consume)


def norm_matmul_gated(x, xs, g, w_stack, li, *, tm, tn):
    m, k = x.shape
    ns = xs.shape[1]
    n = w_stack.shape[2] // 2
    nj = n // tn
    n_i = m // tm
    rows_out = lambda: pl.BlockSpec((tm, tn), lambda i, j: (i, j))
    step_out = lambda: pl.BlockSpec((None, ns, tn), lambda i, j: (i, 0, j))
    return pl.pallas_call(
        _norm_matmul_gated_kernel,
        grid=(n_i, nj),
        in_specs=[
            pl.BlockSpec(memory_space=pl.ANY),
            pl.BlockSpec((None, ns, k), lambda i, j: (0, 0, 0)),
            pl.BlockSpec((1, k), lambda i, j: (0, 0)),
            pl.BlockSpec((None, k, tn), lambda i, j: (li, 0, j)),
            pl.BlockSpec((None, k, tn), lambda i, j: (li, 0, nj + j)),
        ],
        out_specs=[rows_out(), step_out(), rows_out(), step_out()],
        out_shape=[jax.ShapeDtypeStruct((m, n), BF16), jax.ShapeDtypeStruct((n_i, ns, n), BF16),
                   jax.ShapeDtypeStruct((m, n), F32), jax.ShapeDtypeStruct((n_i, ns, n), F32)],
        scratch_shapes=[pltpu.VMEM((tm + ns, k), BF16), *_row_stream_scratch(tm, k)],
        compiler_params=_params(2),
        name="norm_matmul_gated",
    )(x, xs, g.reshape(1, k), w_stack, w_stack)


def _matmul_residual_kernel(a_ref, as_ref, w_ref, b_ref, x_ref, xs_ref, o_ref, os_ref):
    tm = a_ref.shape[0]
    w = w_ref[...].astype(BF16)
    blocks = _row_blocks(tm)
    for rows in blocks[:-1]:
        o_ref[rows, :] = x_ref[rows, :] + (_dot(a_ref[rows, :], w) + b_ref[...])
    last = blocks[-1]
    lhs = jnp.concatenate([a_ref[last, :], as_ref[...]], axis=0)
    res = _dot(lhs, w) + b_ref[...]
    o_ref[last, :] = x_ref[last, :] + res[:tm - last.start]
    os_ref[...] = xs_ref[...] + res[tm - last.start:]


def matmul_residual(a, a_s, w_stack, li, b, x, xs, *, tm, tn, k_blocks=1, k_index=0):
    m = a.shape[0]
    k = a.shape[1] // k_blocks
    ns = xs.shape[1]
    n = w_stack.shape[2]
    return pl.pallas_call(
        _matmul_residual_kernel,
        grid=(m // tm, n // tn),
        in_specs=[
            _row_tile_spec((tm, k), lambda i, j: (i, k_index), a.dtype),
            pl.BlockSpec((None, ns, k), lambda i, j: (0, 0, k_index)),
            pl.BlockSpec((None, k, tn), lambda i, j: (li, k_index, j)),
            pl.BlockSpec((1, tn), lambda i, j: (0, j)),
            pl.BlockSpec((tm, tn), lambda i, j: (i, j)),
            pl.BlockSpec((None, ns, tn), lambda i, j: (0, 0, j)),
        ],
        out_specs=[pl.BlockSpec((tm, tn), lambda i, j: (i, j)),
                   pl.BlockSpec((None, ns, tn), lambda i, j: (i, 0, j))],
        out_shape=[jax.ShapeDtypeStruct((m, n), F32), jax.ShapeDtypeStruct((m // tm, ns, n), F32)],
        compiler_params=_params(2),
        name="matmul_residual",
    )(a, a_s, w_stack, b.reshape(1, n), x, xs)


def _norm_kernel(x_ref, g_ref, o_ref):
    o_ref[...] = _rms_rows(x_ref[...], g_ref[...])


def final_norm(x, g, *, tm):
    m, k = x.shape
    return pl.pallas_call(
        _norm_kernel,
        grid=(m // tm,),
        in_specs=[pl.BlockSpec((tm, k), lambda i: (i, 0)), pl.BlockSpec((1, k), lambda i: (0, 0))],
        out_specs=pl.BlockSpec((tm, k), lambda i: (i, 0)),
        out_shape=jax.ShapeDtypeStruct((m, k), F32),
        compiler_params=_params(1),
        name="final_norm",
    )(x, g.reshape(1, k))


def _ffn_in_kernel(x_hbm, xs_ref, g_ref, wg_ref, wu_ref, cw_ref, cb_ref, prev_ref,
                   act_ref, acts_ref, gt_ref, gts_ref, hn_ref, ext_ref, xbuf_ref, sem_ref):
    wg = wg_ref[...].astype(BF16)
    wu = wu_ref[...].astype(BF16)
    cw = cw_ref[...]

    def activation(gate, up, prev1, prev2):
        conv = cb_ref[...] + gate * cw[2:3] + prev2 * cw[0:1] + prev1 * cw[1:2]
        return (jax.nn.gelu(conv) * up).astype(BF16)

    def zero_history():
        ext_ref[0:SUBLANES, :] = jnp.zeros((SUBLANES, ext_ref.shape[1]), F32)

    def consume(rows, lhs, n):
        r0 = rows.start
        gate = _dot(lhs, wg)
        up = _dot(lhs, wu)
        ext_ref[SUBLANES + r0:SUBLANES + r0 + n, :] = gate[:n]
        prev1 = ext_ref[SUBLANES - 1 + r0:SUBLANES - 1 + r0 + n, :]
        prev2 = ext_ref[SUBLANES - 2 + r0:SUBLANES - 2 + r0 + n, :]
        act_ref[rows, :] = activation(gate[:n], up[:n], prev1, prev2)
        if gate.shape[0] > n:
            gt_ref[...] = gate[n - SUBLANES:n, :]
            gts_ref[...] = gate[n:, :]
            acts_ref[...] = activation(gate[n:], up[n:], prev_ref[1], prev_ref[0])

    _for_normalized_blocks(x_hbm, xs_ref, g_ref, hn_ref, xbuf_ref, sem_ref, consume, on_first=zero_history)


def ffn_in(x, xs, g, w_in, conv_w, conv_b, li, prev, *, tm, tf):
    m, k = x.shape
    ns = xs.shape[1]
    nf = D_FF // tf
    n_i = m // tm
    act, act_s, tail, gate_s = pl.pallas_call(
        _ffn_in_kernel,
        grid=(n_i, nf),
        in_specs=[
            pl.BlockSpec(memory_space=pl.ANY),
            pl.BlockSpec((None, ns, k), lambda i, f: (0, 0, 0)),
            pl.BlockSpec((1, k), lambda i, f: (0, 0)),
            pl.BlockSpec((None, k, tf), lambda i, f: (li, 0, f)),
            pl.BlockSpec((None, k, tf), lambda i, f: (li, 0, nf + f)),
            pl.BlockSpec((None, CONV_F, tf), lambda i, f: (li, 0, f)),
            pl.BlockSpec((None, 1, tf), lambda i, f: (li, 0, f)),
            pl.BlockSpec((2, ns, tf), lambda i, f: (0, 0, f)),
        ],
        out_specs=[
            pl.BlockSpec((tm, tf), lambda i, f: (i, f)),
            pl.BlockSpec((None, ns, tf), lambda i, f: (i, 0, f)),
            pl.BlockSpec((None, SUBLANES, tf), lambda i, f: (i, 0, f)),
            pl.BlockSpec((None, ns, tf), lambda i, f: (i, 0, f)),
        ],
        out_shape=[jax.ShapeDtypeStruct((m, D_FF), BF16), jax.ShapeDtypeStruct((n_i, ns, D_FF), BF16),
                   jax.ShapeDtypeStruct((n_i, SUBLANES, D_FF), F32), jax.ShapeDtypeStruct((n_i, ns, D_FF), F32)],
        scratch_shapes=[pltpu.VMEM((tm + ns, k), BF16), pltpu.VMEM((tm + SUBLANES, tf), F32),
                        *_row_stream_scratch(tm, k)],
        compiler_params=_params(2),
        name="ffn_in",
    )(x, xs, g.reshape(1, k), w_in, w_in, conv_w, conv_b.reshape(DEPTH, 1, D_FF), prev)
    return act, act_s, tail, gate_s


def _lru_gates(xc, wg_ref, ba, bi, lam):
    lam_abs = jnp.abs(lam)
    softplus_neg = jnp.maximum(-lam, 0.0) + jnp.log(1.0 + jnp.exp(-lam_abs))
    a_parts, u_parts = [], []
    for gi in range(N_LRU_GROUPS):
        cols = slice(gi * LRU_GROUP, (gi + 1) * LRU_GROUP)
        xg = xc[:, cols]
        proj = _dot(xg.astype(BF16), wg_ref[gi])
        r = _sigmoid(proj[:, :LRU_GROUP] + ba[:, cols])
        i = _sigmoid(proj[:, LRU_GROUP:] + bi[:, cols])
        log_a = -LRU_C * r * softplus_neg[:, cols]
        a = jnp.exp(log_a)
        a_parts.append(a)
        u_parts.append(jnp.sqrt(1.0 - a * a) * (i * xg))
    return a_parts, u_parts


def _lru_seq_kernel(gate_ref, xr_ref, cw_ref, cb_ref, wg_ref, ba_ref, bi_ref, lam_ref,
                    y_ref, hlast_ref, ext_ref, a_ref, u_ref, carry_ref):
    tt = xr_ref.shape[0]
    n_grp = tt // SUBLANES

    @pl.when(pl.program_id(1) == 0)
    def _():
        ext_ref[0:SUBLANES, :] = jnp.zeros((SUBLANES, D_RNN), F32)
        carry_ref[...] = jnp.zeros((1, D_RNN), F32)

    xr = xr_ref[...]
    ext_ref[SUBLANES:, :] = xr
    cw = cw_ref[...]
    xc = cb_ref[...] + xr * cw[CONV_A - 1:CONV_A]
    for j in range(CONV_A - 1):
        off = SUBLANES - (CONV_A - 1) + j
        xc = xc + ext_ref[off:off + tt, :] * cw[j:j + 1]
    ext_ref[0:SUBLANES, :] = xr[tt - SUBLANES:, :]

    a_parts, u_parts = _lru_gates(xc, wg_ref, ba_ref[...], bi_ref[...], lam_ref[...])
    for gi in range(N_LRU_GROUPS):
        cols = slice(gi * LRU_GROUP, (gi + 1) * LRU_GROUP)
        a_ref[:, :, cols] = a_parts[gi].reshape(n_grp, SUBLANES, LRU_GROUP)
        u_ref[:, :, cols] = u_parts[gi].reshape(n_grp, SUBLANES, LRU_GROUP)

    sub = lax.broadcasted_iota(jnp.int32, (n_grp, SUBLANES, LANES), 1)
    for ci in range(D_RNN // LANES):
        cols = slice(ci * LANES, (ci + 1) * LANES)
        a3 = a_ref[:, :, cols]
        u3 = u_ref[:, :, cols]
        d = 1
        while d < SUBLANES:
            keep = sub >= d
            a_sh = jnp.where(keep, pltpu.roll(a3, d, 1), 1.0)
            u_sh = jnp.where(keep, pltpu.roll(u3, d, 1), 0.0)
            u3 = a3 * u_sh + u3
            a3 = a3 * a_sh
            d *= 2
        h_prev = jnp.broadcast_to(carry_ref[:, cols], (SUBLANES, LANES))
        for g in range(0, n_grp, 2):
            h0 = a3[g] * h_prev + u3[g]
            h_prev = jnp.broadcast_to(h0[SUBLANES - 1:, :], (SUBLANES, LANES))
            h1 = a3[g + 1] * h_prev + u3[g + 1]
            h_prev = jnp.broadcast_to(h1[SUBLANES - 1:, :], (SUBLANES, LANES))
            rows = slice(g * SUBLANES, (g + 2) * SUBLANES)
            h = jnp.concatenate([h0, h1], axis=0)
            y_ref[rows, cols] = (gate_ref[rows, cols].astype(F32) * h).astype(BF16)
        carry_ref[:, cols] = h_prev[0:1, :]
        hlast_ref[:, cols] = h_prev[0:1, :]


def lru_seq(gate_act, xr, conv_w, conv_b, wg, b_a, b_i, lam, *, batch, seq_len, tt):
    m = xr.shape[0]
    nt = seq_len // tt
    vec = lambda: pl.BlockSpec((1, D_RNN), lambda b, t: (0, 0))
    y, h_last = pl.pallas_call(
        _lru_seq_kernel,
        grid=(batch, nt),
        in_specs=[
            pl.BlockSpec((tt, D_RNN), lambda b, t: (b * nt + t, 0)),
            pl.BlockSpec((tt, D_RNN), lambda b, t: (b * nt + t, 0)),
            pl.BlockSpec((CONV_A, D_RNN), lambda b, t: (0, 0)),
            vec(),
            pl.BlockSpec((N_LRU_GROUPS, LRU_GROUP, 2 * LRU_GROUP), lambda b, t: (0, 0, 0)),
            vec(), vec(), vec(),
        ],
        out_specs=[
            pl.BlockSpec((tt, D_RNN), lambda b, t: (b * nt + t, 0)),
            pl.BlockSpec((None, 1, D_RNN), lambda b, t: (b, 0, 0)),
        ],
        out_shape=[jax.ShapeDtypeStruct((m, D_RNN), BF16), jax.ShapeDtypeStruct((batch, 1, D_RNN), F32)],
        scratch_shapes=[pltpu.VMEM((tt + SUBLANES, D_RNN), F32), pltpu.VMEM((tt // SUBLANES, SUBLANES, D_RNN), F32),
                        pltpu.VMEM((tt // SUBLANES, SUBLANES, D_RNN), F32), pltpu.VMEM((1, D_RNN), F32)],
        compiler_params=_params(2),
        name="lru_seq",
    )(gate_act, xr, conv_w, conv_b.reshape(1, D_RNN), wg, b_a.reshape(1, D_RNN), b_i.reshape(1, D_RNN),
      lam.reshape(1, D_RNN))
    return y, h_last.reshape(batch, D_RNN)


def _lru_step_kernel(gate_ref, xr_ref, h0_ref, cbuf_ref, cw_ref, cb_ref, wg_ref, ba_ref, bi_ref, lam_ref,
                     y_ref, h_ref):
    xr = xr_ref[...]
    cw = cw_ref[...]
    xc = cb_ref[...] + xr * cw[CONV_A - 1:CONV_A]
    for j in range(CONV_A - 1):
        xc = xc + cbuf_ref[j] * cw[j:j + 1]
    a_parts, u_parts = _lru_gates(xc, wg_ref, ba_ref[...], bi_ref[...], lam_ref[...])
    a = jnp.concatenate(a_parts, axis=1)
    u = jnp.concatenate(u_parts, axis=1)
    h = u + a * h0_ref[...]
    h_ref[...] = h
    y_ref[...] = (gate_ref[...].astype(F32) * h).astype(BF16)


def lru_step(gate_act, xr, h0, cbuf, conv_w, conv_b, wg, b_a, b_i, lam):
    m = xr.shape[0]
    return pl.pallas_call(
        _lru_step_kernel,
        out_shape=[jax.ShapeDtypeStruct((m, D_RNN), BF16), jax.ShapeDtypeStruct((m, D_RNN), F32)],
        compiler_params=pltpu.CompilerParams(vmem_limit_bytes=V7X_VMEM_LIMIT),
        name="lru_step",
    )(gate_act, xr, h0, cbuf, conv_w, conv_b.reshape(1, D_RNN), wg, b_a.reshape(1, D_RNN), b_i.reshape(1, D_RNN),
      lam.reshape(1, D_RNN))


def _lru_gate_weights(w_a, w_i):
    per = LRU_GROUP // LRU_BLOCK
    row_blk = np.arange(LRU_GROUP)[:, None] // LRU_BLOCK
    col_blk = np.arange(2 * LRU_GROUP)[None, :] % LRU_GROUP // LRU_BLOCK
    on_diagonal = jnp.asarray(row_blk == col_blk)
    rows = lambda w: w.astype(BF16).reshape(N_LRU_GROUPS, LRU_GROUP, LRU_BLOCK)
    tiled = jnp.concatenate([jnp.tile(rows(w_a), (1, 1, per)), jnp.tile(rows(w_i), (1, 1, per))], axis=2)
    return jnp.where(on_diagonal, tiled, jnp.zeros((), BF16))


def _rope_tables(pos):
    half = ROT_DIM // 2
    inv = ROPE_THETA ** (-jnp.arange(half, dtype=F32) * (2.0 / ROT_DIM))
    ang = pos[:, None] * inv[None, :]
    cos, sin = jnp.cos(ang), jnp.sin(ang)
    rows = pos.shape[0]
    ones = jnp.ones((rows, HEAD_DIM - ROT_DIM), F32)
    zeros_h = jnp.zeros((rows, half), F32)
    zeros_t = jnp.zeros((rows, HEAD_DIM - ROT_DIM), F32)
    c = jnp.concatenate([cos, cos, ones], axis=1)
    s_lo = jnp.concatenate([-sin, zeros_h, zeros_t], axis=1)
    s_hi = jnp.concatenate([zeros_h, sin, zeros_t], axis=1)
    rep = LANES // HEAD_DIM
    return tuple(jnp.tile(t, (1, rep)) for t in (c, s_lo, s_hi))


def _rope_lanes(x, tables):
    c, s_lo, s_hi = tables
    half = ROT_DIM // 2
    return x * c + pltpu.roll(x, LANES - half, 1) * s_lo + pltpu.roll(x, half, 1) * s_hi


def _rope_kernel(qk_ref, c_ref, slo_ref, shi_ref, q_ref, k_ref):
    tables = (c_ref[...], slo_ref[...], shi_ref[...])
    n_q = N_HEADS * HEAD_DIM // LANES
    for ci in range(QK_COLS // LANES):
        rot = _rope_lanes(qk_ref[:, ci * LANES:(ci + 1) * LANES], tables)
        if ci < n_q:
            q_ref[:, ci * LANES:(ci + 1) * LANES] = rot.astype(BF16)
        else:
            k_ref[:, (ci - n_q) * LANES:(ci - n_q + 1) * LANES] = rot


def rope(qkv, tables, *, tm):
    m = qkv.shape[0]
    nt = tables[0].shape[0] // tm
    tab = lambda: pl.BlockSpec((tm, LANES), lambda i: (i % nt, 0))
    return pl.pallas_call(
        _rope_kernel,
        grid=(m // tm,),
        in_specs=[pl.BlockSpec((tm, QK_COLS), lambda i: (i, 0)), tab(), tab(), tab()],
        out_specs=[pl.BlockSpec((tm, N_HEADS * HEAD_DIM), lambda i: (i, 0)),
                   pl.BlockSpec((tm, KV_COLS), lambda i: (i, 0))],
        out_shape=[jax.ShapeDtypeStruct((m, N_HEADS * HEAD_DIM), BF16), jax.ShapeDtypeStruct((m, KV_COLS), F32)],
        compiler_params=_params(1),
        name="rope",
    )(qkv, *tables)


def _swa_seq_kernel(sink_ref, q_ref, kp_ref, kc_ref, vp_ref, vc_ref, cp_ref, slop_ref, ship_ref,
                    cc_ref, sloc_ref, shic_ref, o_ref, krot_ref):
    qi = pl.program_id(1)
    tq = q_ref.shape[0]
    nk = 2 * tq
    rope_prev = (cp_ref[...], slop_ref[...], ship_ref[...])
    rope_cur = (cc_ref[...], sloc_ref[...], shic_ref[...])
    lane_blocks = [slice(i * LANES, (i + 1) * LANES) for i in range(KV_COLS // LANES)]
    k_prev = jnp.concatenate([_rope_lanes(kp_ref[:, blk], rope_prev) for blk in lane_blocks], axis=1)
    k_cur = jnp.concatenate([_rope_lanes(kc_ref[:, blk], rope_cur) for blk in lane_blocks], axis=1)
    krot_ref[...] = k_cur
    kk = jnp.concatenate([k_prev, k_cur], axis=0) * (HEAD_DIM ** -0.5)
    vv = jnp.concatenate([vp_ref[...], vc_ref[...]], axis=0).astype(BF16)
    key = lax.broadcasted_iota(jnp.int32, (nk, tq), 0)
    qry = lax.broadcasted_iota(jnp.int32, (nk, tq), 1)
    rel = tq + qry - key
    valid = (rel >= 0) & (rel < WINDOW) & ((qi > 0) | (key >= tq))
    low_k = lax.broadcasted_iota(jnp.int32, (nk, LANES), 1) < HEAD_DIM
    low_o = lax.broadcasted_iota(jnp.int32, (tq, LANES), 1) < HEAD_DIM
    for kh in range(N_KV):
        blk = slice((kh // 2) * LANES, (kh // 2 + 1) * LANES)
        in_low = kh % 2 == 0
        k_blk = kk[:, blk]
        k_swap = pltpu.roll(k_blk, HEAD_DIM, 1)
        k_lo = jnp.where(low_k, k_blk if in_low else k_swap, 0.0).astype(BF16)
        k_hi = jnp.where(low_k, 0.0, k_swap if in_low else k_blk).astype(BF16)
        v_blk = vv[:, blk]
        for hp in range(GROUP // 2):
            h0 = kh * GROUP + 2 * hp
            hcols = slice(h0 * HEAD_DIM, (h0 + 2) * HEAD_DIM)
            q_pair = _rope_lanes(q_ref[:, hcols], rope_cur).astype(BF16)
            outs = []
            for which, k_pad in enumerate((k_lo, k_hi)):
                s = jnp.where(valid, _dot_nt(k_pad, q_pair), -jnp.inf)
                sink = sink_ref[h0 + which]
                mx = jnp.maximum(jnp.max(s, axis=0, keepdims=True), sink)
                e = jnp.exp(s - mx)
                denom = jnp.sum(e, axis=0, keepdims=True) + jnp.exp(sink - mx)
                p = (e * (1.0 / denom)).astype(BF16)
                outs.append(_dot_tn(p, v_blk))
            if in_low:
                o_pair = jnp.where(low_o, outs[0], pltpu.roll(outs[1], HEAD_DIM, 1))
            else:
                o_pair = jnp.where(low_o, pltpu.roll(outs[0], HEAD_DIM, 1), outs[1])
            o_ref[:, hcols] = o_pair.astype(BF16)


def swa_seq(qkv, rope_tables, sinks, *, batch, seq_len):
    m = qkv.shape[0]
    tq = WINDOW
    nq = seq_len // tq
    k_col = N_HEADS * HEAD_DIM // KV_COLS
    v_col = QK_COLS // KV_COLS
    prev = lambda col: pl.BlockSpec((tq, KV_COLS), lambda b, i: (b * nq + jnp.maximum(i - 1, 0), col))
    cur = lambda col: pl.BlockSpec((tq, KV_COLS), lambda b, i: (b * nq + i, col))
    tab_prev = lambda: pl.BlockSpec((tq, LANES), lambda b, i: (jnp.maximum(i - 1, 0), 0))
    tab_cur = lambda: pl.BlockSpec((tq, LANES), lambda b, i: (i, 0))
    return pl.pallas_call(
        _swa_seq_kernel,
        grid=(batch, nq),
        in_specs=[
            pl.BlockSpec(memory_space=pltpu.SMEM),
            pl.BlockSpec((tq, N_HEADS * HEAD_DIM), lambda b, i: (b * nq + i, 0)),
            prev(k_col), cur(k_col), prev(v_col), cur(v_col),
            tab_prev(), tab_prev(), tab_prev(), tab_cur(), tab_cur(), tab_cur(),
        ],
        out_specs=[pl.BlockSpec((tq, N_HEADS * HEAD_DIM), lambda b, i: (b * nq + i, 0)),
                   pl.BlockSpec((tq, KV_COLS), lambda b, i: (b * nq + i, 0))],
        out_shape=[jax.ShapeDtypeStruct((m, N_HEADS * HEAD_DIM), BF16), jax.ShapeDtypeStruct((m, KV_COLS), F32)],
        compiler_params=_params(2),
        name="swa_seq",
    )(sinks, qkv, qkv, qkv, qkv, qkv, *rope_tables, *rope_tables)


def _swa_step_kernel(q_ref, kn_ref, vn_ref, ck_ref, cv_ref, sink_ref, o_ref):
    col = lax.broadcasted_iota(jnp.int32, (GROUP, WINDOW), 1)
    scale = HEAD_DIM ** -0.5
    items = [(bi, kh) for bi in range(q_ref.shape[0]) for kh in range(N_KV)]
    heads = lambda kh: slice(kh * GROUP, (kh + 1) * GROUP)
    kcols = lambda kh: slice(kh * HEAD_DIM, (kh + 1) * HEAD_DIM)
    scores = []
    for bi, kh in items:
        q = q_ref[bi, heads(kh), :]
        k_new = kn_ref[bi, kh:kh + 1, :].astype(BF16).astype(F32)
        s_c = _dot_nt(q, ck_ref[bi, :, kcols(kh)].astype(BF16)) * scale
        s_c = jnp.where(col >= 1, s_c, -jnp.inf)
        s_n = jnp.sum(q.astype(F32) * k_new, axis=-1, keepdims=True) * scale
        scores.append((s_c, s_n))
    probs = []
    for (bi, kh), (s_c, s_n) in zip(items, scores):
        sink = sink_ref[heads(kh), :]
        mx = jnp.maximum(jnp.maximum(jnp.max(s_c, axis=-1, keepdims=True), s_n), sink)
        p_c = jnp.exp(s_c - mx)
        p_n = jnp.exp(s_n - mx)
        denom = jnp.sum(p_c, axis=-1, keepdims=True) + p_n + jnp.exp(sink - mx)
        probs.append((p_c.astype(BF16), p_n.astype(BF16).astype(F32), denom))
    for (bi, kh), (p_c, p_n, denom) in zip(items, probs):
        v_new = vn_ref[bi, kh:kh + 1, :].astype(BF16).astype(F32)
        o = _dot(p_c, cv_ref[bi, :, kcols(kh)].astype(BF16)) + p_n * v_new
        o_ref[bi, heads(kh), :] = (o / denom).astype(BF16)


def swa_step(q3, k_new, v_new, cache_k, cache_v, sinks, *, seqs_per_step=8):
    b = q3.shape[0]
    nb = seqs_per_step
    return pl.pallas_call(
        _swa_step_kernel,
        grid=(b // nb,),
        in_specs=[
            pl.BlockSpec((nb, N_HEADS, HEAD_DIM), lambda i: (i, 0, 0)),
            pl.BlockSpec((nb, N_KV, HEAD_DIM), lambda i: (i, 0, 0)),
            pl.BlockSpec((nb, N_KV, HEAD_DIM), lambda i: (i, 0, 0)),
            pl.BlockSpec((nb, WINDOW, KV_COLS), lambda i: (i, 0, 0)),
            pl.BlockSpec((nb, WINDOW, KV_COLS), lambda i: (i, 0, 0)),
            pl.BlockSpec((N_HEADS, 1), lambda i: (0, 0)),
        ],
        out_specs=pl.BlockSpec((nb, N_HEADS, HEAD_DIM), lambda i: (i, 0, 0)),
        out_shape=jax.ShapeDtypeStruct((b, N_HEADS, HEAD_DIM), BF16),
        compiler_params=_params(1),
        name="swa_step",
    )(q3, k_new, v_new, cache_k, cache_v, sinks.reshape(N_HEADS, 1))


def _hgrn_lower_bound(logits, layer):
    mx = jnp.max(logits, axis=0, keepdims=True)
    e = jnp.exp(logits - mx)
    sm = e / jnp.sum(e, axis=0, keepdims=True)
    lb = jnp.zeros_like(sm[0:1])
    for i in range(1, layer + 1):
        lb = lb + sm[i:i + 1]
    return lb


def _hgrn_consts():
    c = HG_CHUNK
    t = np.arange(c)[:, None]
    s = np.arange(c)[None, :]
    tri = (s <= t).astype(np.float32)
    sel = [tri]
    msk = [(s == t)]
    for lvl in range(1, HG_LEVELS + 1):
        w = 1 << (lvl - 1)
        ref_row = (t // (2 * w)) * (2 * w) + w - 1
        upper_t = (t % (2 * w)) >= w
        lower_s = (s % (2 * w)) < w
        sign = np.where(upper_t, 1.0, -1.0)
        sel.append(sign * (tri - (s <= ref_row)))
        msk.append(((t // (2 * w)) == (s // (2 * w))) & upper_t & lower_s)
    sel.append(1.0 - tri)
    sel = np.concatenate(sel, axis=0).astype(np.float32)
    sel2 = np.concatenate([sel, sel], axis=1)
    msk = np.stack(msk, axis=0).astype(np.float32)
    return jnp.asarray(sel2, BF16), jnp.asarray(msk, F32)


def _hgrn_seq_kernel(q_ref, z_ref, v_ref, g_ref, lbl_ref, gn_ref, sel_ref, msk_ref, y_ref, s_out_ref, st_ref,
                     *, layer, heads_per_step):
    ti = pl.program_id(2)
    tt = q_ref.shape[0]
    c = HG_CHUNK

    @pl.when(ti == 0)
    def _():
        st_ref[...] = jnp.zeros(st_ref.shape, F32)

    lb_all = _hgrn_lower_bound(lbl_ref[...], layer)
    gn = gn_ref[...]
    row = lax.broadcasted_iota(jnp.int32, (c, heads_per_step * HG_DK), 0)
    upper = [None] + [((row >> (lvl - 1)) & 1) == 1 for lvl in range(1, HG_LEVELS + 1)]
    states = [st_ref[hh] for hh in range(heads_per_step)]

    head_cols = [slice(hh * HG_DK, (hh + 1) * HG_DK) for hh in range(heads_per_step)]

    def decays(ci):
        rows = slice(ci * c, (ci + 1) * c)
        q = q_ref[rows, :]
        sq = q * _sigmoid(q)
        sz = _sigmoid(z_ref[rows, :])
        log_f = jnp.log(lb_all + (1.0 - lb_all) * sz)
        k = (1.0 - lb_all) * (1.0 - sz)
        hi = log_f.astype(BF16)
        lo = (log_f - hi.astype(F32)).astype(BF16)
        ex = _dot(sel_ref[...], jnp.concatenate([hi, lo], axis=0))
        e_b = jnp.exp(ex[0:c])
        xs = [(jnp.where(upper[lvl], sq, k) * jnp.exp(ex[lvl * c:(lvl + 1) * c])).astype(BF16)
              for lvl in range(1, HG_LEVELS + 1)]
        return dict(rows=rows, v=v_ref[rows, :].astype(BF16), sq=sq.astype(BF16), k=k.astype(BF16), xs=xs,
                    q_dec=(sq * e_b).astype(BF16), k_dec=(k * jnp.exp(ex[(HG_LEVELS + 1) * c:])).astype(BF16),
                    e_last=e_b[c - 1:c, :])

    def scores(d):
        atts = []
        for cols in head_cols:
            att = _dot_nt(d["sq"][:, cols], d["k"][:, cols]) * msk_ref[0]
            for lvl in range(1, HG_LEVELS + 1):
                x = d["xs"][lvl - 1][:, cols]
                att = att + _dot_nt(x, x) * msk_ref[lvl]
            atts.append(att.astype(BF16))
        return atts

    def outputs(d, atts):
        outs = []
        for hh, cols in enumerate(head_cols):
            st = states[hh]
            o = _dot(atts[hh], d["v"][:, cols]) + _dot_nt(d["q_dec"][:, cols], st.astype(BF16))
            states[hh] = st * d["e_last"][:, cols] + _dot_tn(d["v"][:, cols], d["k_dec"][:, cols])
            outs.append(_rms_rows(o, gn))
        g = g_ref[d["rows"], :]
        y_ref[d["rows"], :] = (jnp.concatenate(outs, axis=1) * (g * _sigmoid(g))).astype(BF16)

    n_chunks = tt // c
    stage_a, stage_b = {}, {}
    for step in range(n_chunks + 2):
        if step < n_chunks:
            stage_a[step] = decays(step)
        if 0 <= step - 1 < n_chunks:
            stage_b[step - 1] = scores(stage_a[step - 1])
        if 0 <= step - 2 < n_chunks:
            outputs(stage_a.pop(step - 2), stage_b.pop(step - 2))

    for hh in range(heads_per_step):
        st_ref[hh] = states[hh]

    @pl.when(ti == pl.num_programs(2) - 1)
    def _():
        for hh in range(heads_per_step):
            s_out_ref[hh] = states[hh].T


def hgrn_seq(qzvg, lb_logits, g_norm, *, layer, batch, seq_len, tt, heads_per_step):
    m = qzvg.shape[0]
    nt = seq_len // tt
    hw = heads_per_step * HG_DK
    nh = HG_HEADS // heads_per_step
    sel, msk = _hgrn_consts()
    part = lambda p: pl.BlockSpec((tt, hw), lambda b, h, t: (b * nt + t, p * nh + h))
    y, s_out = pl.pallas_call(
        functools.partial(_hgrn_seq_kernel, layer=layer, heads_per_step=heads_per_step),
        grid=(batch, nh, nt),
        in_specs=[
            part(0), part(1), part(2), part(3),
            pl.BlockSpec((DEPTH, hw), lambda b, h, t: (0, h)),
            pl.BlockSpec((1, HG_DV), lambda b, h, t: (0, 0)),
            pl.BlockSpec(sel.shape, lambda b, h, t: (0, 0)),
            pl.BlockSpec(msk.shape, lambda b, h, t: (0, 0, 0)),
        ],
        out_specs=[
            pl.BlockSpec((tt, hw), lambda b, h, t: (b * nt + t, h)),
            pl.BlockSpec((None, heads_per_step, HG_DK, HG_DV), lambda b, h, t: (b, h, 0, 0)),
        ],
        out_shape=[jax.ShapeDtypeStruct((m, HG_HEADS * HG_DV), BF16),
                   jax.ShapeDtypeStruct((batch, HG_HEADS, HG_DK, HG_DV), F32)],
        scratch_shapes=[pltpu.VMEM((heads_per_step, HG_DV, HG_DK), F32)],
        compiler_params=_params(3),
        name="hgrn_seq",
    )(qzvg, qzvg, qzvg, qzvg, lb_logits, g_norm.reshape(1, HG_DV), sel, msk)
    return y, s_out


def _hgrn_step_kernel(x_ref, s_ref, lbl_ref, gn_ref, y_ref, s_out_ref, *, layer):
    nh = HG_HEADS
    lb = _hgrn_lower_bound(lbl_ref[...], layer)[0]
    pad = jnp.zeros((LANES - 3 * nh, HG_DK), F32)
    for bi in range(x_ref.shape[0]):
        q = x_ref[bi, 0:nh, :]
        z = x_ref[bi, nh:2 * nh, :]
        v = x_ref[bi, 2 * nh:3 * nh, :]
        g = x_ref[bi, 3 * nh:4 * nh, :]
        sq = q * _sigmoid(q)
        sz = _sigmoid(z)
        f = lb + (1.0 - lb) * sz
        k = (1.0 - lb) * (1.0 - sz)
        cols = jnp.concatenate([f, k, sq, pad], axis=0).T
        outs = []
        for h in range(nh):
            f_col = cols[:, h:h + 1]
            k_col = cols[:, nh + h:nh + h + 1]
            q_col = cols[:, 2 * nh + h:2 * nh + h + 1]
            s_new = s_ref[bi, h] * f_col + k_col * v[h:h + 1, :]
            s_out_ref[bi, h] = s_new
            outs.append(jnp.sum(s_new * q_col, axis=0, keepdims=True))
        o = jnp.concatenate(outs, axis=0)
        y_ref[bi] = (_rms_rows(o, gn_ref[...]) * (g * _sigmoid(g))).astype(BF16)


def hgrn_step(x4, state, lb_logits, g_norm, *, layer, seqs_per_step=4):
    b = x4.shape[0]
    nb = seqs_per_step
    return pl.pallas_call(
        functools.partial(_hgrn_step_kernel, layer=layer),
        grid=(b // nb,),
        in_specs=[
            pl.BlockSpec((nb, 4 * HG_HEADS, HG_DK), lambda i: (i, 0, 0)),
            pl.BlockSpec((nb, HG_HEADS, HG_DK, HG_DV), lambda i: (i, 0, 0, 0)),
            pl.BlockSpec((DEPTH, HG_HEADS, HG_DK), lambda i: (0, 0, 0)),
            pl.BlockSpec((1, HG_DV), lambda i: (0, 0)),
        ],
        out_specs=[
            pl.BlockSpec((nb, HG_HEADS, HG_DV), lambda i: (i, 0, 0)),
            pl.BlockSpec((nb, HG_HEADS, HG_DK, HG_DV), lambda i: (i, 0, 0, 0)),
        ],
        out_shape=[jax.ShapeDtypeStruct((b, HG_HEADS, HG_DV), BF16),
                   jax.ShapeDtypeStruct((b, HG_HEADS, HG_DK, HG_DV), F32)],
        compiler_params=_params(1),
        name="hgrn_step",
    )(x4, state, lb_logits.reshape(DEPTH, HG_HEADS, HG_DK), g_norm.reshape(1, HG_DV))


def _trunk(x3, xs3, states, w):
    batch, seq_len, _ = x3.shape
    m = batch * seq_len
    ns = xs3.shape[0]
    x = x3.reshape(m, D_MODEL)
    xs = xs3.reshape(1, ns, D_MODEL)
    tm = seq_len
    tn_in, tn_out, tf = 512, 256, 512
    zeros = lambda n: jnp.zeros((n,), F32)
    lru_wg = [_lru_gate_weights(w["lru_w_a"][j], w["lru_w_i"][j]) for j in range(w["lru_w_a"].shape[0])]
    rope_seq = _rope_tables(jnp.arange(seq_len, dtype=F32))
    rope_step = _rope_tables(jnp.full((ns,), PAST_LEN, F32))

    def out_proj(a, a_s, w_stack, li, b, res, res_s):
        k_blocks = 1
        while tm * (a.shape[1] // k_blocks) * a.dtype.itemsize >= SINGLE_BUFFER_BYTES:
            k_blocks *= 2
        k = a.shape[1] // k_blocks
        cap = max(LANES, min(tn_out, W_TILE_BYTES // (4 * k) // LANES * LANES))
        for ki in range(k_blocks):
            bias = b if ki == k_blocks - 1 else jnp.zeros_like(b)
            res, res_s = matmul_residual(a, a_s, w_stack, li, bias, res, res_s, tm=tm,
                                         tn=_largest_tile(D_MODEL, cap), k_blocks=k_blocks, k_index=ki)
        return res, res_s

    new_p = {"lru_h": [], "lru_conv": [], "swa_k": [], "swa_v": [], "hgrn": [], "ffn_conv": []}
    new_s = {"lru_h": [], "lru_conv": [], "swa_k": [], "swa_v": [], "hgrn": [], "ffn_conv": []}
    for layer in range(DEPTH):
        kind, j = LAYER_MIXER[layer], LAYER_SLOT[layer]
        g_mix = w["norm_mix"][layer]
        if kind == 0:
            gate, gate_s, xr, xr_s = norm_matmul_gated(x, xs, g_mix, w["lru_w_in"], j, tm=tm, tn=tn_in // 2)
            xr_s = xr_s[0]
            lru_w = (w["lru_conv_w"][j], w["lru_conv_b"][j], lru_wg[j], w["lru_b_a"][j], w["lru_b_i"][j],
                     w["lru_lambda"][j])
            y, h_new = lru_seq(gate, xr, *lru_w, batch=batch, seq_len=seq_len, tt=256)
            new_p["lru_h"].append(h_new)
            new_p["lru_conv"].append(xr.reshape(batch, seq_len, D_RNN)[:, seq_len - (CONV_A - 1):])
            cbuf = states["lru_conv"][j]
            ys, hs_new = lru_step(gate_s[0], xr_s, states["lru_h"][j], jnp.swapaxes(cbuf, 0, 1), *lru_w)
            new_s["lru_h"].append(hs_new)
            new_s["lru_conv"].append(jnp.concatenate([cbuf[:, 1:], xr_s[:, None]], axis=1))
            x, xs = out_proj(y, ys[None], w["lru_w_out"], j, zeros(D_MODEL), x, xs)
        elif kind == 1:
            n = QK_COLS + KV_COLS
            qkv, qkvs = norm_matmul(x, xs, g_mix, w["swa_w_qkv"], j, w["swa_b_qkv"][j], tm=tm,
                                    tn=_largest_tile(n, tn_in))
            qkvs = qkvs[0]
            o, k_rot = swa_seq(qkv, rope_seq, w["swa_sinks"][j], batch=batch, seq_len=seq_len)
            k_win = k_rot.reshape(batch, seq_len, KV_COLS)[:, seq_len - WINDOW:]
            v_win = qkv.reshape(batch, seq_len, n)[:, seq_len - WINDOW:, QK_COLS:]
            new_p["swa_k"].append(k_win.reshape(batch, WINDOW, N_KV, HEAD_DIM))
            new_p["swa_v"].append(v_win.reshape(batch, WINDOW, N_KV, HEAD_DIM))
            qs_rot, ks_rot = rope(qkvs, rope_step, tm=ns)
            ck = states["swa_k"][j]
            cv = states["swa_v"][j]
            k_new = ks_rot.reshape(ns, N_KV, HEAD_DIM)
            v_new = qkvs[:, QK_COLS:].reshape(ns, N_KV, HEAD_DIM)
            os_ = swa_step(qs_rot.reshape(ns, N_HEADS, HEAD_DIM), k_new, v_new,
                           ck.reshape(ns, WINDOW, KV_COLS), cv.reshape(ns, WINDOW, KV_COLS), w["swa_sinks"][j])
            new_s["swa_k"].append(jnp.concatenate([ck[:, 1:], k_new[:, None]], axis=1))
            new_s["swa_v"].append(jnp.concatenate([cv[:, 1:], v_new[:, None]], axis=1))
            x, xs = out_proj(o, os_.reshape(1, ns, N_HEADS * HEAD_DIM), w["swa_w_o"], j, w["swa_b_o"][j], x, xs)
        else:
            n = 2 * HG_HEADS * HG_DK + 2 * HG_HEADS * HG_DV
            qzvg, qzvgs = norm_matmul(x, xs, g_mix, w["hg_w_in"], j, zeros(n), tm=tm, tn=_largest_tile(n, tn_in))
            y, s_new = hgrn_seq(qzvg, w["hg_lb_logits"], w["hg_norm"][j], layer=layer, batch=batch,
                                seq_len=seq_len, tt=1024, heads_per_step=4)
            new_p["hgrn"].append(s_new)
            ys, ss_new = hgrn_step(qzvgs[0].reshape(ns, 4 * HG_HEADS, HG_DK), states["hgrn"][j], w["hg_lb_logits"],
                                   w["hg_norm"][j], layer=layer)
            new_s["hgrn"].append(ss_new)
            x, xs = out_proj(y, ys.reshape(1, ns, HG_HEADS * HG_DV), w["hg_w_o"], j, zeros(D_MODEL), x, xs)

        fbuf = states["ffn_conv"][layer]
        act, act_s, tail, gate_s = ffn_in(x, xs, w["norm_ffn"][layer], w["ffn_w_in"], w["ffn_conv_w"],
                                          w["ffn_conv_b"], layer, jnp.swapaxes(fbuf, 0, 1), tm=tm, tf=tf)
        new_p["ffn_conv"].append(tail[:, SUBLANES - (CONV_F - 1):])
        new_s["ffn_conv"].append(jnp.concatenate([fbuf[:, 1:], gate_s[0][:, None]], axis=1))
        x, xs = out_proj(act, act_s, w["ffn_w_out"], layer, zeros(D_MODEL), x, xs)

    y = final_norm(x, w["norm_final"], tm=1024).reshape(batch, seq_len, D_MODEL)
    ys = final_norm(xs[0], w["norm_final"], tm=ns).reshape(ns, 1, D_MODEL)
    order = ("lru_h", "lru_conv", "swa_k", "swa_v", "hgrn", "ffn_conv")
    return ((y, ys) + tuple(jnp.stack(new_p[k]) for k in order) + tuple(jnp.stack(new_s[k]) for k in order))


def kernel(x_prompt, x_sample, state_lru_h, state_lru_conv, cache_swa_k, cache_swa_v, state_hgrn, state_ffn_conv,
           norm_mix, norm_ffn, norm_final,
           lru_w_in, lru_conv_w, lru_conv_b, lru_w_a, lru_b_a, lru_w_i, lru_b_i, lru_lambda, lru_w_out,
           swa_w_qkv, swa_b_qkv, swa_sinks, swa_w_o, swa_b_o,
           hg_w_in, hg_lb_logits, hg_norm, hg_w_o,
           ffn_w_in, ffn_conv_w, ffn_conv_b, ffn_w_out):
    w = dict(norm_mix=norm_mix, norm_ffn=norm_ffn, norm_final=norm_final,
             lru_w_in=lru_w_in, lru_conv_w=lru_conv_w, lru_conv_b=lru_conv_b, lru_w_a=lru_w_a, lru_b_a=lru_b_a,
             lru_w_i=lru_w_i, lru_b_i=lru_b_i, lru_lambda=lru_lambda, lru_w_out=lru_w_out,
             swa_w_qkv=swa_w_qkv, swa_b_qkv=swa_b_qkv, swa_sinks=swa_sinks, swa_w_o=swa_w_o, swa_b_o=swa_b_o,
             hg_w_in=hg_w_in, hg_lb_logits=hg_lb_logits, hg_norm=hg_norm, hg_w_o=hg_w_o,
             ffn_w_in=ffn_w_in, ffn_conv_w=ffn_conv_w, ffn_conv_b=ffn_conv_b, ffn_w_out=ffn_w_out)
    states = dict(lru_h=state_lru_h, lru_conv=state_lru_conv, swa_k=cache_swa_k, swa_v=cache_swa_v,
                  hgrn=state_hgrn, ffn_conv=state_ffn_conv)
    return _trunk(x_prompt, x_sample, states, w)
```

```python
import functools

import numpy as np
import jax
import jax.numpy as jnp
from jax import lax
from jax.experimental import pallas as pl
from jax.experimental.pallas import tpu as pltpu

F32 = jnp.float32
BF16 = jnp.bfloat16

D_MODEL = 2048
DEPTH = 4
PAST_LEN = 16384
EPS = 1e-6
LAYER_MIXER = tuple(i % 3 for i in range(DEPTH))
LAYER_SLOT = tuple(LAYER_MIXER[:i].count(LAYER_MIXER[i]) for i in range(DEPTH))

D_RNN = 2560
LRU_BLOCKS = 16
LRU_BLOCK = D_RNN // LRU_BLOCKS
LRU_GROUP = 640
N_LRU_GROUPS = D_RNN // LRU_GROUP
CONV_A = 4
LRU_C = 8.0

N_HEADS = 32
N_KV = 4
HEAD_DIM = 64
GROUP = N_HEADS // N_KV
WINDOW = 128
ROT_DIM = HEAD_DIM // 4
ROPE_THETA = 500000.0
QK_COLS = (N_HEADS + N_KV) * HEAD_DIM
KV_COLS = N_KV * HEAD_DIM

HG_HEADS = 16
HG_DK = 128
HG_DV = 128
HG_CHUNK = 64
HG_LEVELS = 6

D_FF = 3 * D_MODEL
CONV_F = 3

LANES = 128
SUBLANES = 8
MXU_COLS = 256
V7X_VMEM_LIMIT = 56 * 1024 * 1024
SINGLE_BUFFER_BYTES = 14 * 1024 * 1024
W_TILE_BYTES = 8 * 1024 * 1024
DOT_ROWS = 256


def _params(n_axes, vmem=V7X_VMEM_LIMIT):
    return pltpu.CompilerParams(dimension_semantics=("arbitrary",) * n_axes, vmem_limit_bytes=vmem)


def _sigmoid(x):
    return 1.0 / (1.0 + jnp.exp(-x))


def _rms_rows(x, g):
    var = jnp.mean(x * x, axis=-1, keepdims=True)
    return x * lax.rsqrt(var + EPS) * g


def _dot(a, b):
    return jnp.dot(a, b, preferred_element_type=F32)


def _dot_nt(a, b):
    return lax.dot_general(a, b, (((1,), (1,)), ((), ())), preferred_element_type=F32)


def _dot_tn(a, b):
    return lax.dot_general(a, b, (((0,), (0,)), ((), ())), preferred_element_type=F32)


def _largest_tile(n, cap):
    for unit in (MXU_COLS, LANES):
        t = cap // unit * unit
        while t > 0 and n % t:
            t -= unit
        if t > 0:
            return t
    raise ValueError(f"no lane-aligned tile of {n} columns under {cap}")


def _row_tile_spec(shape, index_map, dtype):
    if shape[0] * shape[1] * jnp.dtype(dtype).itemsize >= SINGLE_BUFFER_BYTES:
        return pl.BlockSpec(shape, index_map, pipeline_mode=pl.Buffered(1))
    return pl.BlockSpec(shape, index_map)


def _row_blocks(n_rows):
    step = min(n_rows, DOT_ROWS)
    return [slice(r, r + step) for r in range(0, n_rows, step)]


def _row_stream_scratch(tm, k):
    return [pltpu.VMEM((2, min(tm, DOT_ROWS), k), F32), pltpu.SemaphoreType.DMA((2,))]


def _for_normalized_blocks(x_hbm, xs_ref, g_ref, hn_ref, xbuf_ref, sem_ref, consume, on_first=None):
    tm = hn_ref.shape[0] - xs_ref.shape[0]
    blocks = _row_blocks(tm)

    def block_copy(b):
        rows = blocks[b]
        src = x_hbm.at[pl.ds(pl.program_id(0) * tm + rows.start, rows.stop - rows.start), :]
        return pltpu.make_async_copy(src, xbuf_ref.at[b % 2], sem_ref.at[b % 2])

    def run(normalize):
        if normalize:
            block_copy(0).start()
            if on_first is not None:
                on_first()
            hn_ref[tm:, :] = _rms_rows(xs_ref[...], g_ref[...]).astype(BF16)
        for b, rows in enumerate(blocks):
            if normalize:
                if b + 1 < len(blocks):
                    block_copy(b + 1).start()
                block_copy(b).wait()
                hn_ref[rows, :] = _rms_rows(xbuf_ref[b % 2], g_ref[...]).astype(BF16)
            lhs = hn_ref[rows.start:, :] if rows.stop == tm else hn_ref[rows, :]
            consume(rows, lhs, rows.stop - rows.start)

    first = pl.program_id(1) == 0
    pl.when(first)(functools.partial(run, True))
    pl.when(jnp.logical_not(first))(functools.partial(run, False))


def _norm_matmul_kernel(x_hbm, xs_ref, g_ref, w_ref, b_ref, o_ref, os_ref, hn_ref, xbuf_ref, sem_ref):
    w = w_ref[...].astype(BF16)

    def consume(rows, lhs, n):
        res = _dot(lhs, w) + b_ref[...]
        o_ref[rows, :] = res[:n]
        if res.shape[0] > n:
            os_ref[...] = res[n:]

    _for_normalized_blocks(x_hbm, xs_ref, g_ref, hn_ref, xbuf_ref, sem_ref, consume)


def norm_matmul(x, xs, g, w_stack, li, b, *, tm, tn):
    m, k = x.shape
    ns = xs.shape[1]
    n = w_stack.shape[2]
    return pl.pallas_call(
        _norm_matmul_kernel,
        grid=(m // tm, n // tn),
        in_specs=[
            pl.BlockSpec(memory_space=pl.ANY),
            pl.BlockSpec((None, ns, k), lambda i, j: (0, 0, 0)),
            pl.BlockSpec((1, k), lambda i, j: (0, 0)),
            pl.BlockSpec((None, k, tn), lambda i, j: (li, 0, j)),
            pl.BlockSpec((1, tn), lambda i, j: (0, j)),
        ],
        out_specs=[pl.BlockSpec((tm, tn), lambda i, j: (i, j)),
                   pl.BlockSpec((None, ns, tn), lambda i, j: (i, 0, j))],
        out_shape=[jax.ShapeDtypeStruct((m, n), F32), jax.ShapeDtypeStruct((m // tm, ns, n), F32)],
        scratch_shapes=[pltpu.VMEM((tm + ns, k), BF16), *_row_stream_scratch(tm, k)],
        compiler_params=_params(2),
        name="norm_matmul",
    )(x, xs, g.reshape(1, k), w_stack, b.reshape(1, n))


def _norm_matmul_gated_kernel(x_hbm, xs_ref, g_ref, wa_ref, wb_ref, act_ref, acts_ref, lin_ref, lins_ref,
                              hn_ref, xbuf_ref, sem_ref):
    wa = wa_ref[...].astype(BF16)
    wb = wb_ref[...].astype(BF16)

    def consume(rows, lhs, n):
        act = jax.nn.gelu(_dot(lhs, wa)).astype(BF16)
        lin = _dot(lhs, wb)
        act_ref[rows, :] = act[:n]
        lin_ref[rows, :] = lin[:n]
        if act.shape[0] > n:
            acts_ref[...] = act[n:]
            lins_ref[...] = lin[n:]

    _for_normalized_blocks(x_hbm, xs_ref, g_ref, hn_ref, xbuf_ref, sem_ref, consume)


def norm_matmul_gated(x, xs, g, w_stack, li, *, tm, tn):
    m, k = x.shape
    ns = xs.shape[1]
    n = w_stack.shape[2] // 2
    nj = n // tn
    n_i = m // tm
    rows_out = lambda: pl.BlockSpec((tm, tn), lambda i, j: (i, j))
    step_out = lambda: pl.BlockSpec((None, ns, tn), lambda i, j: (i, 0, j))
    return pl.pallas_call(
        _norm_matmul_gated_kernel,
        grid=(n_i, nj),
        in_specs=[
            pl.BlockSpec(memory_space=pl.ANY),
            pl.BlockSpec((None, ns, k), lambda i, j: (0, 0, 0)),
            pl.BlockSpec((1, k), lambda i, j: (0, 0)),
            pl.BlockSpec((None, k, tn), lambda i, j: (li, 0, j)),
            pl.BlockSpec((None, k, tn), lambda i, j: (li, 0, nj + j)),
        ],
        out_specs=[rows_out(), step_out(), rows_out(), step_out()],
        out_shape=[jax.ShapeDtypeStruct((m, n), BF16), jax.ShapeDtypeStruct((n_i, ns, n), BF16),
                   jax.ShapeDtypeStruct((m, n), F32), jax.ShapeDtypeStruct((n_i, ns, n), F32)],
        scratch_shapes=[pltpu.VMEM((tm + ns, k), BF16), *_row_stream_scratch(tm, k)],
        compiler_params=_params(2),
        name="norm_matmul_gated",
    )(x, xs, g.reshape(1, k), w_stack, w_stack)


def _matmul_residual_kernel(a_ref, as_ref, w_ref, b_ref, x_ref, xs_ref, o_ref, os_ref):
    tm = a_ref.shape[0]
    w = w_ref[...].astype(BF16)
    blocks = _row_blocks(tm)
    for rows in blocks[:-1]:
        o_ref[rows, :] = x_ref[rows, :] + (_dot(a_ref[rows, :], w) + b_ref[...])
    last = blocks[-1]
    lhs = jnp.concatenate([a_ref[last, :], as_ref[...]], axis=0)
    res = _dot(lhs, w) + b_ref[...]
    o_ref[last, :] = x_ref[last, :] + res[:tm - last.start]
    os_ref[...] = xs_ref[...] + res[tm - last.start:]


def matmul_residual(a, a_s, w_stack, li, b, x, xs, *, tm, tn, k_blocks=1, k_index=0):
    m = a.shape[0]
    k = a.shape[1] // k_blocks
    ns = xs.shape[1]
    n = w_stack.shape[2]
    return pl.pallas_call(
        _matmul_residual_kernel,
        grid=(m // tm, n // tn),
        in_specs=[
            _row_tile_spec((tm, k), lambda i, j: (i, k_index), a.dtype),
            pl.BlockSpec((None, ns, k), lambda i, j: (0, 0, k_index)),
            pl.BlockSpec((None, k, tn), lambda i, j: (li, k_index, j)),
            pl.BlockSpec((1, tn), lambda i, j: (0, j)),
            pl.BlockSpec((tm, tn), lambda i, j: (i, j)),
            pl.BlockSpec((None, ns, tn), lambda i, j: (0, 0, j)),
        ],
        out_specs=[pl.BlockSpec((tm, tn), lambda i, j: (i, j)),
                   pl.BlockSpec((None, ns, tn), lambda i, j: (i, 0, j))],
        out_shape=[jax.ShapeDtypeStruct((m, n), F32), jax.ShapeDtypeStruct((m // tm, ns, n), F32)],
        compiler_params=_params(2),
        name="matmul_residual",
    )(a, a_s, w_stack, b.reshape(1, n), x, xs)


def _norm_kernel(x_ref, g_ref, o_ref):
    o_ref[...] = _rms_rows(x_ref[...], g_ref[...])


def final_norm(x, g, *, tm):
    m, k = x.shape
    return pl.pallas_call(
        _norm_kernel,
        grid=(m // tm,),
        in_specs=[pl.BlockSpec((tm, k), lambda i: (i, 0)), pl.BlockSpec((1, k), lambda i: (0, 0))],
        out_specs=pl.BlockSpec((tm, k), lambda i: (i, 0)),
        out_shape=jax.ShapeDtypeStruct((m, k), F32),
        compiler_params=_params(1),
        name="final_norm",
    )(x, g.reshape(1, k))


def _ffn_in_kernel(x_hbm, xs_ref, g_ref, wg_ref, wu_ref, cw_ref, cb_ref, prev_ref,
                   act_ref, acts_ref, gt_ref, gts_ref, hn_ref, ext_ref, xbuf_ref, sem_ref):
    wg = wg_ref[...].astype(BF16)
    wu = wu_ref[...].astype(BF16)
    cw = cw_ref[...]

    def activation(gate, up, prev1, prev2):
        conv = cb_ref[...] + gate * cw[2:3] + prev2 * cw[0:1] + prev1 * cw[1:2]
        return (jax.nn.gelu(conv) * up).astype(BF16)

    def zero_history():
        ext_ref[0:SUBLANES, :] = jnp.zeros((SUBLANES, ext_ref.shape[1]), F32)

    def consume(rows, lhs, n):
        r0 = rows.start
        gate = _dot(lhs, wg)
        up = _dot(lhs, wu)
        ext_ref[SUBLANES + r0:SUBLANES + r0 + n, :] = gate[:n]
        prev1 = ext_ref[SUBLANES - 1 + r0:SUBLANES - 1 + r0 + n, :]
        prev2 = ext_ref[SUBLANES - 2 + r0:SUBLANES - 2 + r0 + n, :]
        act_ref[rows, :] = activation(gate[:n], up[:n], prev1, prev2)
        if gate.shape[0] > n:
            gt_ref[...] = gate[n - SUBLANES:n, :]
            gts_ref[...] = gate[n:, :]
            acts_ref[...] = activation(gate[n:], up[n:], prev_ref[1], prev_ref[0])

    _for_normalized_blocks(x_hbm, xs_ref, g_ref, hn_ref, xbuf_ref, sem_ref, consume, on_first=zero_history)


def ffn_in(x, xs, g, w_in, conv_w, conv_b, li, prev, *, tm, tf):
    m, k = x.shape
    ns = xs.shape[1]
    nf = D_FF // tf
    n_i = m // tm
    act, act_s, tail, gate_s = pl.pallas_call(
        _ffn_in_kernel,
        grid=(n_i, nf),
        in_specs=[
            pl.BlockSpec(memory_space=pl.ANY),
            pl.BlockSpec((None, ns, k), lambda i, f: (0, 0, 0)),
            pl.BlockSpec((1, k), lambda i, f: (0, 0)),
            pl.BlockSpec((None, k, tf), lambda i, f: (li, 0, f)),
            pl.BlockSpec((None, k, tf), lambda i, f: (li, 0, nf + f)),
            pl.BlockSpec((None, CONV_F, tf), lambda i, f: (li, 0, f)),
            pl.BlockSpec((None, 1, tf), lambda i, f: (li, 0, f)),
            pl.BlockSpec((2, ns, tf), lambda i, f: (0, 0, f)),
        ],
        out_specs=[
            pl.BlockSpec((tm, tf), lambda i, f: (i, f)),
            pl.BlockSpec((None, ns, tf), lambda i, f: (i, 0, f)),
            pl.BlockSpec((None, SUBLANES, tf), lambda i, f: (i, 0, f)),
            pl.BlockSpec((None, ns, tf), lambda i, f: (i, 0, f)),
        ],
        out_shape=[jax.ShapeDtypeStruct((m, D_FF), BF16), jax.ShapeDtypeStruct((n_i, ns, D_FF), BF16),
                   jax.ShapeDtypeStruct((n_i, SUBLANES, D_FF), F32), jax.ShapeDtypeStruct((n_i, ns, D_FF), F32)],
        scratch_shapes=[pltpu.VMEM((tm + ns, k), BF16), pltpu.VMEM((tm + SUBLANES, tf), F32),
                        *_row_stream_scratch(tm, k)],
        compiler_params=_params(2),
        name="ffn_in",
    )(x, xs, g.reshape(1, k), w_in, w_in, conv_w, conv_b.reshape(DEPTH, 1, D_FF), prev)
    return act, act_s, tail, gate_s


def _lru_gates(xc, wg_ref, ba, bi, lam):
    lam_abs = jnp.abs(lam)
    softplus_neg = jnp.maximum(-lam, 0.0) + jnp.log(1.0 + jnp.exp(-lam_abs))
    a_parts, u_parts = [], []
    for gi in range(N_LRU_GROUPS):
        cols = slice(gi * LRU_GROUP, (gi + 1) * LRU_GROUP)
        xg = xc[:, cols]
        proj = _dot(xg.astype(BF16), wg_ref[gi])
        r = _sigmoid(proj[:, :LRU_GROUP] + ba[:, cols])
        i = _sigmoid(proj[:, LRU_GROUP:] + bi[:, cols])
        log_a = -LRU_C * r * softplus_neg[:, cols]
        a = jnp.exp(log_a)
        a_parts.append(a)
        u_parts.append(jnp.sqrt(1.0 - a * a) * (i * xg))
    return a_parts, u_parts


def _lru_seq_kernel(gate_ref, xr_ref, cw_ref, cb_ref, wg_ref, ba_ref, bi_ref, lam_ref,
                    y_ref, hlast_ref, ext_ref, a_ref, u_ref, carry_ref):
    tt = xr_ref.shape[0]
    n_grp = tt // SUBLANES

    @pl.when(pl.program_id(1) == 0)
    def _():
        ext_ref[0:SUBLANES, :] = jnp.zeros((SUBLANES, D_RNN), F32)
        carry_ref[...] = jnp.zeros((1, D_RNN), F32)

    xr = xr_ref[...]
    ext_ref[SUBLANES:, :] = xr
    cw = cw_ref[...]
    xc = cb_ref[...] + xr * cw[CONV_A - 1:CONV_A]
    for j in range(CONV_A - 1):
        off = SUBLANES - (CONV_A - 1) + j
        xc = xc + ext_ref[off:off + tt, :] * cw[j:j + 1]
    ext_ref[0:SUBLANES, :] = xr[tt - SUBLANES:, :]

    a_parts, u_parts = _lru_gates(xc, wg_ref, ba_ref[...], bi_ref[...], lam_ref[...])
    for gi in range(N_LRU_GROUPS):
        cols = slice(gi * LRU_GROUP, (gi + 1) * LRU_GROUP)
        a_ref[:, :, cols] = a_parts[gi].reshape(n_grp, SUBLANES, LRU_GROUP)
        u_ref[:, :, cols] = u_parts[gi].reshape(n_grp, SUBLANES, LRU_GROUP)

    sub = lax.broadcasted_iota(jnp.int32, (n_grp, SUBLANES, LANES), 1)
    for ci in range(D_RNN // LANES):
        cols = slice(ci * LANES, (ci + 1) * LANES)
        a3 = a_ref[:, :, cols]
        u3 = u_ref[:, :, cols]
        d = 1
        while d < SUBLANES:
            keep = sub >= d
            a_sh = jnp.where(keep, pltpu.roll(a3, d, 1), 1.0)
            u_sh = jnp.where(keep, pltpu.roll(u3, d, 1), 0.0)
            u3 = a3 * u_sh + u3
            a3 = a3 * a_sh
            d *= 2
        h_prev = jnp.broadcast_to(carry_ref[:, cols], (SUBLANES, LANES))
        for g in range(0, n_grp, 2):
            h0 = a3[g] * h_prev + u3[g]
            h_prev = jnp.broadcast_to(h0[SUBLANES - 1:, :], (SUBLANES, LANES))
            h1 = a3[g + 1] * h_prev + u3[g + 1]
            h_prev = jnp.broadcast_to(h1[SUBLANES - 1:, :], (SUBLANES, LANES))
            rows = slice(g * SUBLANES, (g + 2) * SUBLANES)
            h = jnp.concatenate([h0, h1], axis=0)
            y_ref[rows, cols] = (gate_ref[rows, cols].astype(F32) * h).astype(BF16)
        carry_ref[:, cols] = h_prev[0:1, :]
        hlast_ref[:, cols] = h_prev[0:1, :]


def lru_seq(gate_act, xr, conv_w, conv_b, wg, b_a, b_i, lam, *, batch, seq_len, tt):
    m = xr.shape[0]
    nt = seq_len // tt
    vec = lambda: pl.BlockSpec((1, D_RNN), lambda b, t: (0, 0))
    y, h_last = pl.pallas_call(
        _lru_seq_kernel,
        grid=(batch, nt),
        in_specs=[
            pl.BlockSpec((tt, D_RNN), lambda b, t: (b * nt + t, 0)),
            pl.BlockSpec((tt, D_RNN), lambda b, t: (b * nt + t, 0)),
            pl.BlockSpec((CONV_A, D_RNN), lambda b, t: (0, 0)),
            vec(),
            pl.BlockSpec((N_LRU_GROUPS, LRU_GROUP, 2 * LRU_GROUP), lambda b, t: (0, 0, 0)),
            vec(), vec(), vec(),
        ],
        out_specs=[
            pl.BlockSpec((tt, D_RNN), lambda b, t: (b * nt + t, 0)),
            pl.BlockSpec((None, 1, D_RNN), lambda b, t: (b, 0, 0)),
        ],
        out_shape=[jax.ShapeDtypeStruct((m, D_RNN), BF16), jax.ShapeDtypeStruct((batch, 1, D_RNN), F32)],
        scratch_shapes=[pltpu.VMEM((tt + SUBLANES, D_RNN), F32), pltpu.VMEM((tt // SUBLANES, SUBLANES, D_RNN), F32),
                        pltpu.VMEM((tt // SUBLANES, SUBLANES, D_RNN), F32), pltpu.VMEM((1, D_RNN), F32)],
        compiler_params=_params(2),
        name="lru_seq",
    )(gate_act, xr, conv_w, conv_b.reshape(1, D_RNN), wg, b_a.reshape(1, D_RNN), b_i.reshape(1, D_RNN),
      lam.reshape(1, D_RNN))
    return y, h_last.reshape(batch, D_RNN)


def _lru_step_kernel(gate_ref, xr_ref, h0_ref, cbuf_ref, cw_ref, cb_ref, wg_ref, ba_ref, bi_ref, lam_ref,
                     y_ref, h_ref):
    xr = xr_ref[...]
    cw = cw_ref[...]
    xc = cb_ref[...] + xr * cw[CONV_A - 1:CONV_A]
    for j in range(CONV_A - 1):
        xc = xc + cbuf_ref[j] * cw[j:j + 1]
    a_parts, u_parts = _lru_gates(xc, wg_ref, ba_ref[...], bi_ref[...], lam_ref[...])
    a = jnp.concatenate(a_parts, axis=1)
    u = jnp.concatenate(u_parts, axis=1)
    h = u + a * h0_ref[...]
    h_ref[...] = h
    y_ref[...] = (gate_ref[...].astype(F32) * h).astype(BF16)


def lru_step(gate_act, xr, h0, cbuf, conv_w, conv_b, wg, b_a, b_i, lam):
    m = xr.shape[0]
    return pl.pallas_call(
        _lru_step_kernel,
        out_shape=[jax.ShapeDtypeStruct((m, D_RNN), BF16), jax.ShapeDtypeStruct((m, D_RNN), F32)],
        compiler_params=pltpu.CompilerParams(vmem_limit_bytes=V7X_VMEM_LIMIT),
        name="lru_step",
    )(gate_act, xr, h0, cbuf, conv_w, conv_b.reshape(1, D_RNN), wg, b_a.reshape(1, D_RNN), b_i.reshape(1, D_RNN),
      lam.reshape(1, D_RNN))


def _lru_gate_weights(w_a, w_i):
    per = LRU_GROUP // LRU_BLOCK
    row_blk = np.arange(LRU_GROUP)[:, None] // LRU_BLOCK
    col_blk = np.arange(2 * LRU_GROUP)[None, :] % LRU_GROUP // LRU_BLOCK
    on_diagonal = jnp.asarray(row_blk == col_blk)
    rows = lambda w: w.astype(BF16).reshape(N_LRU_GROUPS, LRU_GROUP, LRU_BLOCK)
    tiled = jnp.concatenate([jnp.tile(rows(w_a), (1, 1, per)), jnp.tile(rows(w_i), (1, 1, per))], axis=2)
    return jnp.where(on_diagonal, tiled, jnp.zeros((), BF16))


def _rope_tables(pos):
    half = ROT_DIM // 2
    inv = ROPE_THETA ** (-jnp.arange(half, dtype=F32) * (2.0 / ROT_DIM))
    ang = pos[:, None] * inv[None, :]
    cos, sin = jnp.cos(ang), jnp.sin(ang)
    rows = pos.shape[0]
    ones = jnp.ones((rows, HEAD_DIM - ROT_DIM), F32)
    zeros_h = jnp.zeros((rows, half), F32)
    zeros_t = jnp.zeros((rows, HEAD_DIM - ROT_DIM), F32)
    c = jnp.concatenate([cos, cos, ones], axis=1)
    s_lo = jnp.concatenate([-sin, zeros_h, zeros_t], axis=1)
    s_hi = jnp.concatenate([zeros_h, sin, zeros_t], axis=1)
    rep = LANES // HEAD_DIM
    return tuple(jnp.tile(t, (1, rep)) for t in (c, s_lo, s_hi))


def _rope_lanes(x, tables):
    c, s_lo, s_hi = tables
    half = ROT_DIM // 2
    return x * c + pltpu.roll(x, LANES - half, 1) * s_lo + pltpu.roll(x, half, 1) * s_hi


def _rope_kernel(qk_ref, c_ref, slo_ref, shi_ref, q_ref, k_ref):
    tables = (c_ref[...], slo_ref[...], shi_ref[...])
    n_q = N_HEADS * HEAD_DIM // LANES
    for ci in range(QK_COLS // LANES):
        rot = _rope_lanes(qk_ref[:, ci * LANES:(ci + 1) * LANES], tables)
        if ci < n_q:
            q_ref[:, ci * LANES:(ci + 1) * LANES] = rot.astype(BF16)
        else:
            k_ref[:, (ci - n_q) * LANES:(ci - n_q + 1) * LANES] = rot


def rope(qkv, tables, *, tm):
    m = qkv.shape[0]
    nt = tables[0].shape[0] // tm
    tab = lambda: pl.BlockSpec((tm, LANES), lambda i: (i % nt, 0))
    return pl.pallas_call(
        _rope_kernel,
        grid=(m // tm,),
        in_specs=[pl.BlockSpec((tm, QK_COLS), lambda i: (i, 0)), tab(), tab(), tab()],
        out_specs=[pl.BlockSpec((tm, N_HEADS * HEAD_DIM), lambda i: (i, 0)),
                   pl.BlockSpec((tm, KV_COLS), lambda i: (i, 0))],
        out_shape=[jax.ShapeDtypeStruct((m, N_HEADS * HEAD_DIM), BF16), jax.ShapeDtypeStruct((m, KV_COLS), F32)],
        compiler_params=_params(1),
        name="rope",
    )(qkv, *tables)


def _swa_seq_kernel(sink_ref, q_ref, kp_ref, kc_ref, vp_ref, vc_ref, cp_ref, slop_ref, ship_ref,
                    cc_ref, sloc_ref, shic_ref, o_ref, krot_ref):
    qi = pl.program_id(1)
    tq = q_ref.shape[0]
    nk = 2 * tq
    rope_prev = (cp_ref[...], slop_ref[...], ship_ref[...])
    rope_cur = (cc_ref[...], sloc_ref[...], shic_ref[...])
    lane_blocks = [slice(i * LANES, (i + 1) * LANES) for i in range(KV_COLS // LANES)]
    k_prev = jnp.concatenate([_rope_lanes(kp_ref[:, blk], rope_prev) for blk in lane_blocks], axis=1)
    k_cur = jnp.concatenate([_rope_lanes(kc_ref[:, blk], rope_cur) for blk in lane_blocks], axis=1)
    krot_ref[...] = k_cur
    kk = jnp.concatenate([k_prev, k_cur], axis=0) * (HEAD_DIM ** -0.5)
    vv = jnp.concatenate([vp_ref[...], vc_ref[...]], axis=0).astype(BF16)
    key = lax.broadcasted_iota(jnp.int32, (nk, tq), 0)
    qry = lax.broadcasted_iota(jnp.int32, (nk, tq), 1)
    rel = tq + qry - key
    valid = (rel >= 0) & (rel < WINDOW) & ((qi > 0) | (key >= tq))
    low_k = lax.broadcasted_iota(jnp.int32, (nk, LANES), 1) < HEAD_DIM
    low_o = lax.broadcasted_iota(jnp.int32, (tq, LANES), 1) < HEAD_DIM
    for kh in range(N_KV):
        blk = slice((kh // 2) * LANES, (kh // 2 + 1) * LANES)
        in_low = kh % 2 == 0
        k_blk = kk[:, blk]
        k_swap = pltpu.roll(k_blk, HEAD_DIM, 1)
        k_lo = jnp.where(low_k, k_blk if in_low else k_swap, 0.0).astype(BF16)
        k_hi = jnp.where(low_k, 0.0, k_swap if in_low else k_blk).astype(BF16)
        v_blk = vv[:, blk]
        for hp in range(GROUP // 2):
            h0 = kh * GROUP + 2 * hp
            hcols = slice(h0 * HEAD_DIM, (h0 + 2) * HEAD_DIM)
            q_pair = _rope_lanes(q_ref[:, hcols], rope_cur).astype(BF16)
            outs = []
            for which, k_pad in enumerate((k_lo, k_hi)):
                s = jnp.where(valid, _dot_nt(k_pad, q_pair), -jnp.inf)
                sink = sink_ref[h0 + which]
                mx = jnp.maximum(jnp.max(s, axis=0, keepdims=True), sink)
                e = jnp.exp(s - mx)
                denom = jnp.sum(e, axis=0, keepdims=True) + jnp.exp(sink - mx)
                p = (e * (1.0 / denom)).astype(BF16)
                outs.append(_dot_tn(p, v_blk))
            if in_low:
                o_pair = jnp.where(low_o, outs[0], pltpu.roll(outs[1], HEAD_DIM, 1))
            else:
                o_pair = jnp.where(low_o, pltpu.roll(outs[0], HEAD_DIM, 1), outs[1])
            o_ref[:, hcols] = o_pair.astype(BF16)


def swa_seq(qkv, rope_tables, sinks, *, batch, seq_len):
    m = qkv.shape[0]
    tq = WINDOW
    nq = seq_len // tq
    k_col = N_HEADS * HEAD_DIM // KV_COLS
    v_col = QK_COLS // KV_COLS
    prev = lambda col: pl.BlockSpec((tq, KV_COLS), lambda b, i: (b * nq + jnp.maximum(i - 1, 0), col))
    cur = lambda col: pl.BlockSpec((tq, KV_COLS), lambda b, i: (b * nq + i, col))
    tab_prev = lambda: pl.BlockSpec((tq, LANES), lambda b, i: (jnp.maximum(i - 1, 0), 0))
    tab_cur = lambda: pl.BlockSpec((tq, LANES), lambda b, i: (i, 0))
    return pl.pallas_call(
        _swa_seq_kernel,
        grid=(batch, nq),
        in_specs=[
            pl.BlockSpec(memory_space=pltpu.SMEM),
            pl.BlockSpec((tq, N_HEADS * HEAD_DIM), lambda b, i: (b * nq + i, 0)),
            prev(k_col), cur(k_col), prev(v_col), cur(v_col),
            tab_prev(), tab_prev(), tab_prev(), tab_cur(), tab_cur(), tab_cur(),
        ],
        out_specs=[pl.BlockSpec((tq, N_HEADS * HEAD_DIM), lambda b, i: (b * nq + i, 0)),
                   pl.BlockSpec((tq, KV_COLS), lambda b, i: (b * nq + i, 0))],
        out_shape=[jax.ShapeDtypeStruct((m, N_HEADS * HEAD_DIM), BF16), jax.ShapeDtypeStruct((m, KV_COLS), F32)],
        compiler_params=_params(2),
        name="swa_seq",
    )(sinks, qkv, qkv, qkv, qkv, qkv, *rope_tables, *rope_tables)


def _swa_step_kernel(q_ref, kn_ref, vn_ref, ck_ref, cv_ref, sink_ref, o_ref):
    col = lax.broadcasted_iota(jnp.int32, (GROUP, WINDOW), 1)
    scale = HEAD_DIM ** -0.5
    items = [(bi, kh) for bi in range(q_ref.shape[0]) for kh in range(N_KV)]
    heads = lambda kh: slice(kh * GROUP, (kh + 1) * GROUP)
    kcols = lambda kh: slice(kh * HEAD_DIM, (kh + 1) * HEAD_DIM)
    scores = []
    for bi, kh in items:
        q = q_ref[bi, heads(kh), :]
        k_new = kn_ref[bi, kh:kh + 1, :].astype(BF16).astype(F32)
        s_c = _dot_nt(q, ck_ref[bi, :, kcols(kh)].astype(BF16)) * scale
        s_c = jnp.where(col >= 1, s_c, -jnp.inf)
        s_n = jnp.sum(q.astype(F32) * k_new, axis=-1, keepdims=True) * scale
        scores.append((s_c, s_n))
    probs = []
    for (bi, kh), (s_c, s_n) in zip(items, scores):
        sink = sink_ref[heads(kh), :]
        mx = jnp.maximum(jnp.maximum(jnp.max(s_c, axis=-1, keepdims=True), s_n), sink)
        p_c = jnp.exp(s_c - mx)
        p_n = jnp.exp(s_n - mx)
        denom = jnp.sum(p_c, axis=-1, keepdims=True) + p_n + jnp.exp(sink - mx)
        probs.append((p_c.astype(BF16), p_n.astype(BF16).astype(F32), denom))
    for (bi, kh), (p_c, p_n, denom) in zip(items, probs):
        v_new = vn_ref[bi, kh:kh + 1, :].astype(BF16).astype(F32)
        o = _dot(p_c, cv_ref[bi, :, kcols(kh)].astype(BF16)) + p_n * v_new
        o_ref[bi, heads(kh), :] = (o / denom).astype(BF16)


def swa_step(q3, k_new, v_new, cache_k, cache_v, sinks, *, seqs_per_step=8):
    b = q3.shape[0]
    nb = seqs_per_step
    return pl.pallas_call(
        _swa_step_kernel,
        grid=(b // nb,),
        in_specs=[
            pl.BlockSpec((nb, N_HEADS, HEAD_DIM), lambda i: (i, 0, 0)),
            pl.BlockSpec((nb, N_KV, HEAD_DIM), lambda i: (i, 0, 0)),
            pl.BlockSpec((nb, N_KV, HEAD_DIM), lambda i: (i, 0, 0)),
            pl.BlockSpec((nb, WINDOW, KV_COLS), lambda i: (i, 0, 0)),
            pl.BlockSpec((nb, WINDOW, KV_COLS), lambda i: (i, 0, 0)),
            pl.BlockSpec((N_HEADS, 1), lambda i: (0, 0)),
        ],
        out_specs=pl.BlockSpec((nb, N_HEADS, HEAD_DIM), lambda i: (i, 0, 0)),
        out_shape=jax.ShapeDtypeStruct((b, N_HEADS, HEAD_DIM), BF16),
        compiler_params=_params(1),
        name="swa_step",
    )(q3, k_new, v_new, cache_k, cache_v, sinks.reshape(N_HEADS, 1))


def _hgrn_lower_bound(logits, layer):
    mx = jnp.max(logits, axis=0, keepdims=True)
    e = jnp.exp(logits - mx)
    sm = e / jnp.sum(e, axis=0, keepdims=True)
    lb = jnp.zeros_like(sm[0:1])
    for i in range(1, layer + 1):
        lb = lb + sm[i:i + 1]
    return lb


def _hgrn_consts():
    c = HG_CHUNK
    t = np.arange(c)[:, None]
    s = np.arange(c)[None, :]
    tri = (s <= t).astype(np.float32)
    sel = [tri]
    msk = [(s == t)]
    for lvl in range(1, HG_LEVELS + 1):
        w = 1 << (lvl - 1)
        ref_row = (t // (2 * w)) * (2 * w) + w - 1
        upper_t = (t % (2 * w)) >= w
        lower_s = (s % (2 * w)) < w
        sign = np.where(upper_t, 1.0, -1.0)
        sel.append(sign * (tri - (s <= ref_row)))
        msk.append(((t // (2 * w)) == (s // (2 * w))) & upper_t & lower_s)
    sel.append(1.0 - tri)
    sel = np.concatenate(sel, axis=0).astype(np.float32)
    sel2 = np.concatenate([sel, sel], axis=1)
    msk = np.stack(msk, axis=0).astype(np.float32)
    return jnp.asarray(sel2, BF16), jnp.asarray(msk, F32)


def _hgrn_seq_kernel(q_ref, z_ref, v_ref, g_ref, lbl_ref, gn_ref, sel_ref, msk_ref, y_ref, s_out_ref, st_ref,
                     *, layer, heads_per_step):
    ti = pl.program_id(2)
    tt = q_ref.shape[0]
    c = HG_CHUNK

    @pl.when(ti == 0)
    def _():
        st_ref[...] = jnp.zeros(st_ref.shape, F32)

    lb_all = _hgrn_lower_bound(lbl_ref[...], layer)
    gn = gn_ref[...]
    row = lax.broadcasted_iota(jnp.int32, (c, heads_per_step * HG_DK), 0)
    upper = [None] + [((row >> (lvl - 1)) & 1) == 1 for lvl in range(1, HG_LEVELS + 1)]
    states = [st_ref[hh] for hh in range(heads_per_step)]

    head_cols = [slice(hh * HG_DK, (hh + 1) * HG_DK) for hh in range(heads_per_step)]

    def decays(ci):
        rows = slice(ci * c, (ci + 1) * c)
        q = q_ref[rows, :]
        sq = q * _sigmoid(q)
        sz = _sigmoid(z_ref[rows, :])
        log_f = jnp.log(lb_all + (1.0 - lb_all) * sz)
        k = (1.0 - lb_all) * (1.0 - sz)
        hi = log_f.astype(BF16)
        lo = (log_f - hi.astype(F32)).astype(BF16)
        ex = _dot(sel_ref[...], jnp.concatenate([hi, lo], axis=0))
        e_b = jnp.exp(ex[0:c])
        xs = [(jnp.where(upper[lvl], sq, k) * jnp.exp(ex[lvl * c:(lvl + 1) * c])).astype(BF16)
              for lvl in range(1, HG_LEVELS + 1)]
        return dict(rows=rows, v=v_ref[rows, :].astype(BF16), sq=sq.astype(BF16), k=k.astype(BF16), xs=xs,
                    q_dec=(sq * e_b).astype(BF16), k_dec=(k * jnp.exp(ex[(HG_LEVELS + 1) * c:])).astype(BF16),
                    e_last=e_b[c - 1:c, :])

    def scores(d):
        atts = []
        for cols in head_cols:
            att = _dot_nt(d["sq"][:, cols], d["k"][:, cols]) * msk_ref[0]
            for lvl in range(1, HG_LEVELS + 1):
                x = d["xs"][lvl - 1][:, cols]
                att = att + _dot_nt(x, x) * msk_ref[lvl]
            atts.append(att.astype(BF16))
        return atts

    def outputs(d, atts):
        outs = []
        for hh, cols in enumerate(head_cols):
            st = states[hh]
            o = _dot(atts[hh], d["v"][:, cols]) + _dot_nt(d["q_dec"][:, cols], st.astype(BF16))
            states[hh] = st * d["e_last"][:, cols] + _dot_tn(d["v"][:, cols], d["k_dec"][:, cols])
            outs.append(_rms_rows(o, gn))
        g = g_ref[d["rows"], :]
        y_ref[d["rows"], :] = (jnp.concatenate(outs, axis=1) * (g * _sigmoid(g))).astype(BF16)

    n_chunks = tt // c
    stage_a, stage_b = {}, {}
    for step in range(n_chunks + 2):
        if step < n_chunks:
            stage_a[step] = decays(step)
        if 0 <= step - 1 < n_chunks:
            stage_b[step - 1] = scores(stage_a[step - 1])
        if 0 <= step - 2 < n_chunks:
            outputs(stage_a.pop(step - 2), stage_b.pop(step - 2))

    for hh in range(heads_per_step):
        st_ref[hh] = states[hh]

    @pl.when(ti == pl.num_programs(2) - 1)
    def _():
        for hh in range(heads_per_step):
            s_out_ref[hh] = states[hh].T


def hgrn_seq(qzvg, lb_logits, g_norm, *, layer, batch, seq_len, tt, heads_per_step):
    m = qzvg.shape[0]
    nt = seq_len // tt
    hw = heads_per_step * HG_DK
    nh = HG_HEADS // heads_per_step
    sel, msk = _hgrn_consts()
    part = lambda p: pl.BlockSpec((tt, hw), lambda b, h, t: (b * nt + t, p * nh + h))
    y, s_out = pl.pallas_call(
        functools.partial(_hgrn_seq_kernel, layer=layer, heads_per_step=heads_per_step),
        grid=(batch, nh, nt),
        in_specs=[
            part(0), part(1), part(2), part(3),
            pl.BlockSpec((DEPTH, hw), lambda b, h, t: (0, h)),
            pl.BlockSpec((1, HG_DV), lambda b, h, t: (0, 0)),
            pl.BlockSpec(sel.shape, lambda b, h, t: (0, 0)),
            pl.BlockSpec(msk.shape, lambda b, h, t: (0, 0, 0)),
        ],
        out_specs=[
            pl.BlockSpec((tt, hw), lambda b, h, t: (b * nt + t, h)),
            pl.BlockSpec((None, heads_per_step, HG_DK, HG_DV), lambda b, h, t: (b, h, 0, 0)),
        ],
        out_shape=[jax.ShapeDtypeStruct((m, HG_HEADS * HG_DV), BF16),
                   jax.ShapeDtypeStruct((batch, HG_HEADS, HG_DK, HG_DV), F32)],
        scratch_shapes=[pltpu.VMEM((heads_per_step, HG_DV, HG_DK), F32)],
        compiler_params=_params(3),
        name="hgrn_seq",
    )(qzvg, qzvg, qzvg, qzvg, lb_logits, g_norm.reshape(1, HG_DV), sel, msk)
    return y, s_out


def _hgrn_step_kernel(x_ref, s_ref, lbl_ref, gn_ref, y_ref, s_out_ref, *, layer):
    nh = HG_HEADS
    lb = _hgrn_lower_bound(lbl_ref[...], layer)[0]
    pad = jnp.zeros((LANES - 3 * nh, HG_DK), F32)
    for bi in range(x_ref.shape[0]):
        q = x_ref[bi, 0:nh, :]
        z = x_ref[bi, nh:2 * nh, :]
        v = x_ref[bi, 2 * nh:3 * nh, :]
        g = x_ref[bi, 3 * nh:4 * nh, :]
        sq = q * _sigmoid(q)
        sz = _sigmoid(z)
        f = lb + (1.0 - lb) * sz
        k = (1.0 - lb) * (1.0 - sz)
        cols = jnp.concatenate([f, k, sq, pad], axis=0).T
        outs = []
        for h in range(nh):
            f_col = cols[:, h:h + 1]
            k_col = cols[:, nh + h:nh + h + 1]
            q_col = cols[:, 2 * nh + h:2 * nh + h + 1]
            s_new = s_ref[bi, h] * f_col + k_col * v[h:h + 1, :]
            s_out_ref[bi, h] = s_new
            outs.append(jnp.sum(s_new * q_col, axis=0, keepdims=True))
        o = jnp.concatenate(outs, axis=0)
        y_ref[bi] = (_rms_rows(o, gn_ref[...]) * (g * _sigmoid(g))).astype(BF16)


def hgrn_step(x4, state, lb_logits, g_norm, *, layer, seqs_per_step=4):
    b = x4.shape[0]
    nb = seqs_per_step
    return pl.pallas_call(
        functools.partial(_hgrn_step_kernel, layer=layer),
        grid=(b // nb,),
        in_specs=[
            pl.BlockSpec((nb, 4 * HG_HEADS, HG_DK), lambda i: (i, 0, 0)),
            pl.BlockSpec((nb, HG_HEADS, HG_DK, HG_DV), lambda i: (i, 0, 0, 0)),
            pl.BlockSpec((DEPTH, HG_HEADS, HG_DK), lambda i: (0, 0, 0)),
            pl.BlockSpec((1, HG_DV), lambda i: (0, 0)),
        ],
        out_specs=[
            pl.BlockSpec((nb, HG_HEADS, HG_DV), lambda i: (i, 0, 0)),
            pl.BlockSpec((nb, HG_HEADS, HG_DK, HG_DV), lambda i: (i, 0, 0, 0)),
        ],
        out_shape=[jax.ShapeDtypeStruct((b, HG_HEADS, HG_DV), BF16),
                   jax.ShapeDtypeStruct((b, HG_HEADS, HG_DK, HG_DV), F32)],
        compiler_params=_params(1),
        name="hgrn_step",
    )(x4, state, lb_logits.reshape(DEPTH, HG_HEADS, HG_DK), g_norm.reshape(1, HG_DV))


def _trunk(x3, xs3, states, w):
    batch, seq_len, _ = x3.shape
    m = batch * seq_len
    ns = xs3.shape[0]
    x = x3.reshape(m, D_MODEL)
    xs = xs3.reshape(1, ns, D_MODEL)
    tm = seq_len
    tn_in, tn_out, tf = 1024, 256, 512
    zeros = lambda n: jnp.zeros((n,), F32)
    lru_wg = [_lru_gate_weights(w["lru_w_a"][j], w["lru_w_i"][j]) for j in range(w["lru_w_a"].shape[0])]
    rope_seq = _rope_tables(jnp.arange(seq_len, dtype=F32))
    rope_step = _rope_tables(jnp.full((ns,), PAST_LEN, F32))

    def out_proj(a, a_s, w_stack, li, b, res, res_s):
        k_blocks = 1
        while tm * (a.shape[1] // k_blocks) * a.dtype.itemsize >= SINGLE_BUFFER_BYTES:
            k_blocks *= 2
        k = a.shape[1] // k_blocks
        cap = max(LANES, min(tn_out, W_TILE_BYTES // (4 * k) // LANES * LANES))
        for ki in range(k_blocks):
            bias = b if ki == k_blocks - 1 else jnp.zeros_like(b)
            res, res_s = matmul_residual(a, a_s, w_stack, li, bias, res, res_s, tm=tm,
                                         tn=_largest_tile(D_MODEL, cap), k_blocks=k_blocks, k_index=ki)
        return res, res_s

    new_p = {"lru_h": [], "lru_conv": [], "swa_k": [], "swa_v": [], "hgrn": [], "ffn_conv": []}
    new_s = {"lru_h": [], "lru_conv": [], "swa_k": [], "swa_v": [], "hgrn": [], "ffn_conv": []}
    for layer in range(DEPTH):
        kind, j = LAYER_MIXER[layer], LAYER_SLOT[layer]
        g_mix = w["norm_mix"][layer]
        if kind == 0:
            gate, gate_s, xr, xr_s = norm_matmul_gated(x, xs, g_mix, w["lru_w_in"], j, tm=tm, tn=tn_in // 2)
            xr_s = xr_s[0]
            lru_w = (w["lru_conv_w"][j], w["lru_conv_b"][j], lru_wg[j], w["lru_b_a"][j], w["lru_b_i"][j],
                     w["lru_lambda"][j])
            y, h_new = lru_seq(gate, xr, *lru_w, batch=batch, seq_len=seq_len, tt=256)
            new_p["lru_h"].append(h_new)
            new_p["lru_conv"].append(xr.reshape(batch, seq_len, D_RNN)[:, seq_len - (CONV_A - 1):])
            cbuf = states["lru_conv"][j]
            ys, hs_new = lru_step(gate_s[0], xr_s, states["lru_h"][j], jnp.swapaxes(cbuf, 0, 1), *lru_w)
            new_s["lru_h"].append(hs_new)
            new_s["lru_conv"].append(jnp.concatenate([cbuf[:, 1:], xr_s[:, None]], axis=1))
            x, xs = out_proj(y, ys[None], w["lru_w_out"], j, zeros(D_MODEL), x, xs)
        elif kind == 1:
            n = QK_COLS + KV_COLS
            qkv, qkvs = norm_matmul(x, xs, g_mix, w["swa_w_qkv"], j, w["swa_b_qkv"][j], tm=tm,
                                    tn=_largest_tile(n, tn_in))
            qkvs = qkvs[0]
            o, k_rot = swa_seq(qkv, rope_seq, w["swa_sinks"][j], batch=batch, seq_len=seq_len)
            k_win = k_rot.reshape(batch, seq_len, KV_COLS)[:, seq_len - WINDOW:]
            v_win = qkv.reshape(batch, seq_len, n)[:, seq_len - WINDOW:, QK_COLS:]
            new_p["swa_k"].append(k_win.reshape(batch, WINDOW, N_KV, HEAD_DIM))
            new_p["swa_v"].append(v_win.reshape(batch, WINDOW, N_KV, HEAD_DIM))
            qs_rot, ks_rot = rope(qkvs, rope_step, tm=ns)
            ck = states["swa_k"][j]
            cv = states["swa_v"][j]
            k_new = ks_rot.reshape(ns, N_KV, HEAD_DIM)
            v_new = qkvs[:, QK_COLS:].reshape(ns, N_KV, HEAD_DIM)
            os_ = swa_step(qs_rot.reshape(ns, N_HEADS, HEAD_DIM), k_new, v_new,
                           ck.reshape(ns, WINDOW, KV_COLS), cv.reshape(ns, WINDOW, KV_COLS), w["swa_sinks"][j])
            new_s["swa_k"].append(jnp.concatenate([ck[:, 1:], k_new[:, None]], axis=1))
            new_s["swa_v"].append(jnp.concatenate([cv[:, 1:], v_new[:, None]], axis=1))
            x, xs = out_proj(o, os_.reshape(1, ns, N_HEADS * HEAD_DIM), w["swa_w_o"], j, w["swa_b_o"][j], x, xs)
        else:
            n = 2 * HG_HEADS * HG_DK + 2 * HG_HEADS * HG_DV
            qzvg, qzvgs = norm_matmul(x, xs, g_mix, w["hg_w_in"], j, zeros(n), tm=tm, tn=_largest_tile(n, tn_in))
            y, s_new = hgrn_seq(qzvg, w["hg_lb_logits"], w["hg_norm"][j], layer=layer, batch=batch,
                                seq_len=seq_len, tt=1024, heads_per_step=4)
            new_p["hgrn"].append(s_new)
            ys, ss_new = hgrn_step(qzvgs[0].reshape(ns, 4 * HG_HEADS, HG_DK), states["hgrn"][j], w["hg_lb_logits"],
                                   w["hg_norm"][j], layer=layer)
            new_s["hgrn"].append(ss_new)
            x, xs = out_proj(y, ys.reshape(1, ns, HG_HEADS * HG_DV), w["hg_w_o"], j, zeros(D_MODEL), x, xs)

        fbuf = states["ffn_conv"][layer]
        act, act_s, tail, gate_s = ffn_in(x, xs, w["norm_ffn"][layer], w["ffn_w_in"], w["ffn_conv_w"],
                                          w["ffn_conv_b"], layer, jnp.swapaxes(fbuf, 0, 1), tm=tm, tf=tf)
        new_p["ffn_conv"].append(tail[:, SUBLANES - (CONV_F - 1):])
        new_s["ffn_conv"].append(jnp.concatenate([fbuf[:, 1:], gate_s[0][:, None]], axis=1))
        x, xs = out_proj(act, act_s, w["ffn_w_out"], layer, zeros(D_MODEL), x, xs)

    y = final_norm(x, w["norm_final"], tm=1024).reshape(batch, seq_len, D_MODEL)
    ys = final_norm(xs[0], w["norm_final"], tm=ns).reshape(ns, 1, D_MODEL)
    order = ("lru_h", "lru_conv", "swa_k", "swa_v", "hgrn", "ffn_conv")
    return ((y, ys) + tuple(jnp.stack(new_p[k]) for k in order) + tuple(jnp.stack(new_s[k]) for k in order))


def kernel(x_prompt, x_sample, state_lru_h, state_lru_conv, cache_swa_k, cache_swa_v, state_hgrn, state_ffn_conv,
           norm_mix, norm_ffn, norm_final,
           lru_w_in, lru_conv_w, lru_conv_b, lru_w_a, lru_b_a, lru_w_i, lru_b_i, lru_lambda, lru_w_out,
           swa_w_qkv, swa_b_qkv, swa_sinks, swa_w_o, swa_b_o,
           hg_w_in, hg_lb_logits, hg_norm, hg_w_o,
           ffn_w_in, ffn_conv_w, ffn_conv_b, ffn_w_out):
    w = dict(norm_mix=norm_mix, norm_ffn=norm_ffn, norm_final=norm_final,
             lru_w_in=lru_w_in, lru_conv_w=lru_conv_w, lru_conv_b=lru_conv_b, lru_w_a=lru_w_a, lru_b_a=lru_b_a,
             lru_w_i=lru_w_i, lru_b_i=lru_b_i, lru_lambda=lru_lambda, lru_w_out=lru_w_out,
             swa_w_qkv=swa_w_qkv, swa_b_qkv=swa_b_qkv, swa_sinks=swa_sinks, swa_w_o=swa_w_o, swa_b_o=swa_b_o,
             hg_w_in=hg_w_in, hg_lb_logits=hg_lb_logits, hg_norm=hg_norm, hg_w_o=hg_w_o,
             ffn_w_in=ffn_w_in, ffn_conv_w=ffn_conv_w, ffn_conv_b=ffn_conv_b, ffn_w_out=ffn_w_out)
    states = dict(lru_h=state_lru_h, lru_conv=state_lru_conv, swa_k=cache_swa_k, swa_v=cache_swa_v,
                  hgrn=state_hgrn, ffn_conv=state_ffn_conv)
    return _trunk(x_prompt, x_sample, states, w)
```

```python
import functools

import numpy as np
import jax
import jax.numpy as jnp
from jax import lax
from jax.experimental import pallas as pl
from jax.experimental.pallas import tpu as pltpu

F32 = jnp.float32
BF16 = jnp.bfloat16

D_MODEL = 2048
DEPTH = 4
PAST_LEN = 16384
EPS = 1e-6
LAYER_MIXER = tuple(i % 3 for i in range(DEPTH))
LAYER_SLOT = tuple(LAYER_MIXER[:i].count(LAYER_MIXER[i]) for i in range(DEPTH))

D_RNN = 2560
LRU_BLOCKS = 16
LRU_BLOCK = D_RNN // LRU_BLOCKS
LRU_GROUP = 640
N_LRU_GROUPS = D_RNN // LRU_GROUP
CONV_A = 4
LRU_C = 8.0

N_HEADS = 32
N_KV = 4
HEAD_DIM = 64
GROUP = N_HEADS // N_KV
WINDOW = 128
ROT_DIM = HEAD_DIM // 4
ROPE_THETA = 500000.0
QK_COLS = (N_HEADS + N_KV) * HEAD_DIM
KV_COLS = N_KV * HEAD_DIM

HG_HEADS = 16
HG_DK = 128
HG_DV = 128
HG_CHUNK = 64
HG_LEVELS = 6

D_FF = 3 * D_MODEL
CONV_F = 3

LANES = 128
SUBLANES = 8
MXU_COLS = 256
V7X_VMEM_LIMIT = 56 * 1024 * 1024
SINGLE_BUFFER_BYTES = 14 * 1024 * 1024
W_TILE_BYTES = 8 * 1024 * 1024
DOT_ROWS = 256


def _params(n_axes, vmem=V7X_VMEM_LIMIT):
    return pltpu.CompilerParams(dimension_semantics=("arbitrary",) * n_axes, vmem_limit_bytes=vmem)


def _sigmoid(x):
    return 1.0 / (1.0 + jnp.exp(-x))


def _rms_rows(x, g):
    var = jnp.mean(x * x, axis=-1, keepdims=True)
    return x * lax.rsqrt(var + EPS) * g


def _dot(a, b):
    return jnp.dot(a, b, preferred_element_type=F32)


def _dot_nt(a, b):
    return lax.dot_general(a, b, (((1,), (1,)), ((), ())), preferred_element_type=F32)


def _dot_tn(a, b):
    return lax.dot_general(a, b, (((0,), (0,)), ((), ())), preferred_element_type=F32)


def _largest_tile(n, cap):
    for unit in (MXU_COLS, LANES):
        t = cap // unit * unit
        while t > 0 and n % t:
            t -= unit
        if t > 0:
            return t
    raise ValueError(f"no lane-aligned tile of {n} columns under {cap}")


def _row_tile_spec(shape, index_map, dtype):
    if shape[0] * shape[1] * jnp.dtype(dtype).itemsize >= SINGLE_BUFFER_BYTES:
        return pl.BlockSpec(shape, index_map, pipeline_mode=pl.Buffered(1))
    return pl.BlockSpec(shape, index_map)


def _row_blocks(n_rows):
    step = min(n_rows, DOT_ROWS)
    return [slice(r, r + step) for r in range(0, n_rows, step)]


def _row_stream_scratch(tm, k):
    return [pltpu.VMEM((2, min(tm, DOT_ROWS), k), F32), pltpu.SemaphoreType.DMA((2,))]


def _for_normalized_blocks(x_hbm, xs_ref, g_ref, hn_ref, xbuf_ref, sem_ref, consume, on_first=None):
    tm = hn_ref.shape[0] - xs_ref.shape[0]
    blocks = _row_blocks(tm)

    def block_copy(b):
        rows = blocks[b]
        src = x_hbm.at[pl.ds(pl.program_id(0) * tm + rows.start, rows.stop - rows.start), :]
        return pltpu.make_async_copy(src, xbuf_ref.at[b % 2], sem_ref.at[b % 2])

    def run(normalize):
        if normalize:
            block_copy(0).start()
            if on_first is not None:
                on_first()
            hn_ref[tm:, :] = _rms_rows(xs_ref[...], g_ref[...]).astype(BF16)
        for b, rows in enumerate(blocks):
            if normalize:
                if b + 1 < len(blocks):
                    block_copy(b + 1).start()
                block_copy(b).wait()
                hn_ref[rows, :] = _rms_rows(xbuf_ref[b % 2], g_ref[...]).astype(BF16)
            lhs = hn_ref[rows.start:, :] if rows.stop == tm else hn_ref[rows, :]
            consume(rows, lhs, rows.stop - rows.start)

    first = pl.program_id(1) == 0
    pl.when(first)(functools.partial(run, True))
    pl.when(jnp.logical_not(first))(functools.partial(run, False))


def _norm_matmul_kernel(x_hbm, xs_ref, g_ref, w_ref, b_ref, o_ref, os_ref, hn_ref, xbuf_ref, sem_ref):
    w = w_ref[...].astype(BF16)

    def consume(rows, lhs, n):
        res = _dot(lhs, w) + b_ref[...]
        o_ref[rows, :] = res[:n]
        if res.shape[0] > n:
            os_ref[...] = res[n:]

    _for_normalized_blocks(x_hbm, xs_ref, g_ref, hn_ref, xbuf_ref, sem_ref, consume)


def norm_matmul(x, xs, g, w_stack, li, b, *, tm, tn):
    m, k = x.shape
    ns = xs.shape[1]
    n = w_stack.shape[2]
    return pl.pallas_call(
        _norm_matmul_kernel,
        grid=(m // tm, n // tn),
        in_specs=[
            pl.BlockSpec(memory_space=pl.ANY),
            pl.BlockSpec((None, ns, k), lambda i, j: (0, 0, 0)),
            pl.BlockSpec((1, k), lambda i, j: (0, 0)),
            pl.BlockSpec((None, k, tn), lambda i, j: (li, 0, j)),
            pl.BlockSpec((1, tn), lambda i, j: (0, j)),
        ],
        out_specs=[pl.BlockSpec((tm, tn), lambda i, j: (i, j)),
                   pl.BlockSpec((None, ns, tn), lambda i, j: (i, 0, j))],
        out_shape=[jax.ShapeDtypeStruct((m, n), F32), jax.ShapeDtypeStruct((m // tm, ns, n), F32)],
        scratch_shapes=[pltpu.VMEM((tm + ns, k), BF16), *_row_stream_scratch(tm, k)],
        compiler_params=_params(2),
        name="norm_matmul",
    )(x, xs, g.reshape(1, k), w_stack, b.reshape(1, n))


def _norm_matmul_gated_kernel(x_hbm, xs_ref, g_ref, wa_ref, wb_ref, act_ref, acts_ref, lin_ref, lins_ref,
                              hn_ref, xbuf_ref, sem_ref):
    wa = wa_ref[...].astype(BF16)
    wb = wb_ref[...].astype(BF16)

    def consume(rows, lhs, n):
        act = jax.nn.gelu(_dot(lhs, wa)).astype(BF16)
        lin = _dot(lhs, wb)
        act_ref[rows, :] = act[:n]
        lin_ref[rows, :] = lin[:n]
        if act.shape[0] > n:
            acts_ref[...] = act[n:]
            lins_ref[...] = lin[n:]

    _for_normalized_blocks(x_hbm, xs_ref, g_ref, hn_ref, xbuf_ref, sem_ref, consume)


def norm_matmul_gated(x, xs, g, w_stack, li, *, tm, tn):
    m, k = x.shape
    ns = xs.shape[1]
    n = w_stack.shape[2] // 2
    nj = n // tn
    n_i = m // tm
    rows_out = lambda: pl.BlockSpec((tm, tn), lambda i, j: (i, j))
    step_out = lambda: pl.BlockSpec((None, ns, tn), lambda i, j: (i, 0, j))
    return pl.pallas_call(
        _norm_matmul_gated_kernel,
        grid=(n_i, nj),
        in_specs=[
            pl.BlockSpec(memory_space=pl.ANY),
            pl.BlockSpec((None, ns, k), lambda i, j: (0, 0, 0)),
            pl.BlockSpec((1, k), lambda i, j: (0, 0)),
            pl.BlockSpec((None, k, tn), lambda i, j: (li, 0, j)),
            pl.BlockSpec((None, k, tn), lambda i, j: (li, 0, nj + j)),
        ],
        out_specs=[rows_out(), step_out(), rows_out(), step_out()],
        out_shape=[jax.ShapeDtypeStruct((m, n), BF16), jax.ShapeDtypeStruct((n_i, ns, n), BF16),
                   jax.ShapeDtypeStruct((m, n), F32), jax.ShapeDtypeStruct((n_i, ns, n), F32)],
        scratch_shapes=[pltpu.VMEM((tm + ns, k), BF16), *_row_stream_scratch(tm, k)],
        compiler_params=_params(2),
        name="norm_matmul_gated",
    )(x, xs, g.reshape(1, k), w_stack, w_stack)


def _matmul_residual_kernel(a_ref, as_ref, w_ref, b_ref, x_ref, xs_ref, o_ref, os_ref):
    tm = a_ref.shape[0]
    w = w_ref[...].astype(BF16)
    blocks = _row_blocks(tm)
    for rows in blocks[:-1]:
        o_ref[rows, :] = x_ref[rows, :] + (_dot(a_ref[rows, :], w) + b_ref[...])
    last = blocks[-1]
    lhs = jnp.concatenate([a_ref[last, :], as_ref[...]], axis=0)
    res = _dot(lhs, w) + b_ref[...]
    o_ref[last, :] = x_ref[last, :] + res[:tm - last.start]
    os_ref[...] = xs_ref[...] + res[tm - last.start:]


def matmul_residual(a, a_s, w_stack, li, b, x, xs, *, tm, tn, k_blocks=1, k_index=0):
    m = a.shape[0]
    k = a.shape[1] // k_blocks
    ns = xs.shape[1]
    n = w_stack.shape[2]
    return pl.pallas_call(
        _matmul_residual_kernel,
        grid=(m // tm, n // tn),
        in_specs=[
            _row_tile_spec((tm, k), lambda i, j: (i, k_index), a.dtype),
            pl.BlockSpec((None, ns, k), lambda i, j: (0, 0, k_index)),
            pl.BlockSpec((None, k, tn), lambda i, j: (li, k_index, j)),
            pl.BlockSpec((1, tn), lambda i, j: (0, j)),
            pl.BlockSpec((tm, tn), lambda i, j: (i, j)),
            pl.BlockSpec((None, ns, tn), lambda i, j: (0, 0, j)),
        ],
        out_specs=[pl.BlockSpec((tm, tn), lambda i, j: (i, j)),
                   pl.BlockSpec((None, ns, tn), lambda i, j: (i, 0, j))],
        out_shape=[jax.ShapeDtypeStruct((m, n), F32), jax.ShapeDtypeStruct((m // tm, ns, n), F32)],
        compiler_params=_params(2),
        name="matmul_residual",
    )(a, a_s, w_stack, b.reshape(1, n), x, xs)


def _norm_kernel(x_ref, g_ref, o_ref):
    o_ref[...] = _rms_rows(x_ref[...], g_ref[...])


def final_norm(x, g, *, tm):
    m, k = x.shape
    return pl.pallas_call(
        _norm_kernel,
        grid=(m // tm,),
        in_specs=[pl.BlockSpec((tm, k), lambda i: (i, 0)), pl.BlockSpec((1, k), lambda i: (0, 0))],
        out_specs=pl.BlockSpec((tm, k), lambda i: (i, 0)),
        out_shape=jax.ShapeDtypeStruct((m, k), F32),
        compiler_params=_params(1),
        name="final_norm",
    )(x, g.reshape(1, k))


def _ffn_in_kernel(x_hbm, xs_ref, g_ref, wg_ref, wu_ref, cw_ref, cb_ref, prev_ref,
                   act_ref, acts_ref, gt_ref, gts_ref, hn_ref, ext_ref, xbuf_ref, sem_ref):
    wg = wg_ref[...].astype(BF16)
    wu = wu_ref[...].astype(BF16)
    cw = cw_ref[...]

    def activation(gate, up, prev1, prev2):
        conv = cb_ref[...] + gate * cw[2:3] + prev2 * cw[0:1] + prev1 * cw[1:2]
        return (jax.nn.gelu(conv) * up).astype(BF16)

    def zero_history():
        ext_ref[0:SUBLANES, :] = jnp.zeros((SUBLANES, ext_ref.shape[1]), F32)

    def consume(rows, lhs, n):
        r0 = rows.start
        gate = _dot(lhs, wg)
        up = _dot(lhs, wu)
        ext_ref[SUBLANES + r0:SUBLANES + r0 + n, :] = gate[:n]
        prev1 = ext_ref[SUBLANES - 1 + r0:SUBLANES - 1 + r0 + n, :]
        prev2 = ext_ref[SUBLANES - 2 + r0:SUBLANES - 2 + r0 + n, :]
        act_ref[rows, :] = activation(gate[:n], up[:n], prev1, prev2)
        if gate.shape[0] > n:
            gt_ref[...] = gate[n - SUBLANES:n, :]
            gts_ref[...] = gate[n:, :]
            acts_ref[...] = activation(gate[n:], up[n:], prev_ref[1], prev_ref[0])

    _for_normalized_blocks(x_hbm, xs_ref, g_ref, hn_ref, xbuf_ref, sem_ref, consume, on_first=zero_history)


def ffn_in(x, xs, g, w_in, conv_w, conv_b, li, prev, *, tm, tf):
    m, k = x.shape
    ns = xs.shape[1]
    nf = D_FF // tf
    n_i = m // tm
    act, act_s, tail, gate_s = pl.pallas_call(
        _ffn_in_kernel,
        grid=(n_i, nf),
        in_specs=[
            pl.BlockSpec(memory_space=pl.ANY),
            pl.BlockSpec((None, ns, k), lambda i, f: (0, 0, 0)),
            pl.BlockSpec((1, k), lambda i, f: (0, 0)),
            pl.BlockSpec((None, k, tf), lambda i, f: (li, 0, f)),
            pl.BlockSpec((None, k, tf), lambda i, f: (li, 0, nf + f)),
            pl.BlockSpec((None, CONV_F, tf), lambda i, f: (li, 0, f)),
            pl.BlockSpec((None, 1, tf), lambda i, f: (li, 0, f)),
            pl.BlockSpec((2, ns, tf), lambda i, f: (0, 0, f)),
        ],
        out_specs=[
            pl.BlockSpec((tm, tf), lambda i, f: (i, f)),
            pl.BlockSpec((None, ns, tf), lambda i, f: (i, 0, f)),
            pl.BlockSpec((None, SUBLANES, tf), lambda i, f: (i, 0, f)),
            pl.BlockSpec((None, ns, tf), lambda i, f: (i, 0, f)),
        ],
        out_shape=[jax.ShapeDtypeStruct((m, D_FF), BF16), jax.ShapeDtypeStruct((n_i, ns, D_FF), BF16),
                   jax.ShapeDtypeStruct((n_i, SUBLANES, D_FF), F32), jax.ShapeDtypeStruct((n_i, ns, D_FF), F32)],
        scratch_shapes=[pltpu.VMEM((tm + ns, k), BF16), pltpu.VMEM((tm + SUBLANES, tf), F32),
                        *_row_stream_scratch(tm, k)],
        compiler_params=_params(2),
        name="ffn_in",
    )(x, xs, g.reshape(1, k), w_in, w_in, conv_w, conv_b.reshape(DEPTH, 1, D_FF), prev)
    return act, act_s, tail, gate_s


def _lru_gates(xc, wg_ref, ba, bi, lam):
    lam_abs = jnp.abs(lam)
    softplus_neg = jnp.maximum(-lam, 0.0) + jnp.log(1.0 + jnp.exp(-lam_abs))
    a_parts, u_parts = [], []
    for gi in range(N_LRU_GROUPS):
        cols = slice(gi * LRU_GROUP, (gi + 1) * LRU_GROUP)
        xg = xc[:, cols]
        proj = _dot(xg.astype(BF16), wg_ref[gi])
        r = _sigmoid(proj[:, :LRU_GROUP] + ba[:, cols])
        i = _sigmoid(proj[:, LRU_GROUP:] + bi[:, cols])
        log_a = -LRU_C * r * softplus_neg[:, cols]
        a = jnp.exp(log_a)
        a_parts.append(a)
        u_parts.append(jnp.sqrt(1.0 - a * a) * (i * xg))
    return a_parts, u_parts


def _lru_seq_kernel(gate_ref, xr_ref, cw_ref, cb_ref, wg_ref, ba_ref, bi_ref, lam_ref,
                    y_ref, hlast_ref, ext_ref, a_ref, u_ref, carry_ref):
    tt = xr_ref.shape[0]
    n_grp = tt // SUBLANES

    @pl.when(pl.program_id(1) == 0)
    def _():
        ext_ref[0:SUBLANES, :] = jnp.zeros((SUBLANES, D_RNN), F32)
        carry_ref[...] = jnp.zeros((1, D_RNN), F32)

    xr = xr_ref[...]
    ext_ref[SUBLANES:, :] = xr
    cw = cw_ref[...]
    xc = cb_ref[...] + xr * cw[CONV_A - 1:CONV_A]
    for j in range(CONV_A - 1):
        off = SUBLANES - (CONV_A - 1) + j
        xc = xc + ext_ref[off:off + tt, :] * cw[j:j + 1]
    ext_ref[0:SUBLANES, :] = xr[tt - SUBLANES:, :]

    a_parts, u_parts = _lru_gates(xc, wg_ref, ba_ref[...], bi_ref[...], lam_ref[...])
    for gi in range(N_LRU_GROUPS):
        cols = slice(gi * LRU_GROUP, (gi + 1) * LRU_GROUP)
        a_ref[:, :, cols] = a_parts[gi].reshape(n_grp, SUBLANES, LRU_GROUP)
        u_ref[:, :, cols] = u_parts[gi].reshape(n_grp, SUBLANES, LRU_GROUP)

    sub = lax.broadcasted_iota(jnp.int32, (n_grp, SUBLANES, LANES), 1)
    for ci in range(D_RNN // LANES):
        cols = slice(ci * LANES, (ci + 1) * LANES)
        a3 = a_ref[:, :, cols]
        u3 = u_ref[:, :, cols]
        d = 1
        while d < SUBLANES:
            keep = sub >= d
            a_sh = jnp.where(keep, pltpu.roll(a3, d, 1), 1.0)
            u_sh = jnp.where(keep, pltpu.roll(u3, d, 1), 0.0)
            u3 = a3 * u_sh + u3
            a3 = a3 * a_sh
            d *= 2
        h_prev = jnp.broadcast_to(carry_ref[:, cols], (SUBLANES, LANES))
        for g in range(0, n_grp, 2):
            h0 = a3[g] * h_prev + u3[g]
            h_prev = jnp.broadcast_to(h0[SUBLANES - 1:, :], (SUBLANES, LANES))
            h1 = a3[g + 1] * h_prev + u3[g + 1]
            h_prev = jnp.broadcast_to(h1[SUBLANES - 1:, :], (SUBLANES, LANES))
            rows = slice(g * SUBLANES, (g + 2) * SUBLANES)
            h = jnp.concatenate([h0, h1], axis=0)
            y_ref[rows, cols] = (gate_ref[rows, cols].astype(F32) * h).astype(BF16)
        carry_ref[:, cols] = h_prev[0:1, :]
        hlast_ref[:, cols] = h_prev[0:1, :]


def lru_seq(gate_act, xr, conv_w, conv_b, wg, b_a, b_i, lam, *, batch, seq_len, tt):
    m = xr.shape[0]
    nt = seq_len // tt
    vec = lambda: pl.BlockSpec((1, D_RNN), lambda b, t: (0, 0))
    y, h_last = pl.pallas_call(
        _lru_seq_kernel,
        grid=(batch, nt),
        in_specs=[
            pl.BlockSpec((tt, D_RNN), lambda b, t: (b * nt + t, 0)),
            pl.BlockSpec((tt, D_RNN), lambda b, t: (b * nt + t, 0)),
            pl.BlockSpec((CONV_A, D_RNN), lambda b, t: (0, 0)),
            vec(),
            pl.BlockSpec((N_LRU_GROUPS, LRU_GROUP, 2 * LRU_GROUP), lambda b, t: (0, 0, 0)),
            vec(), vec(), vec(),
        ],
        out_specs=[
            pl.BlockSpec((tt, D_RNN), lambda b, t: (b * nt + t, 0)),
            pl.BlockSpec((None, 1, D_RNN), lambda b, t: (b, 0, 0)),
        ],
        out_shape=[jax.ShapeDtypeStruct((m, D_RNN), BF16), jax.ShapeDtypeStruct((batch, 1, D_RNN), F32)],
        scratch_shapes=[pltpu.VMEM((tt + SUBLANES, D_RNN), F32), pltpu.VMEM((tt // SUBLANES, SUBLANES, D_RNN), F32),
                        pltpu.VMEM((tt // SUBLANES, SUBLANES, D_RNN), F32), pltpu.VMEM((1, D_RNN), F32)],
        compiler_params=_params(2),
        name="lru_seq",
    )(gate_act, xr, conv_w, conv_b.reshape(1, D_RNN), wg, b_a.reshape(1, D_RNN), b_i.reshape(1, D_RNN),
      lam.reshape(1, D_RNN))
    return y, h_last.reshape(batch, D_RNN)


def _lru_step_kernel(gate_ref, xr_ref, h0_ref, cbuf_ref, cw_ref, cb_ref, wg_ref, ba_ref, bi_ref, lam_ref,
                     y_ref, h_ref):
    xr = xr_ref[...]
    cw = cw_ref[...]
    xc = cb_ref[...] + xr * cw[CONV_A - 1:CONV_A]
    for j in range(CONV_A - 1):
        xc = xc + cbuf_ref[j] * cw[j:j + 1]
    a_parts, u_parts = _lru_gates(xc, wg_ref, ba_ref[...], bi_ref[...], lam_ref[...])
    a = jnp.concatenate(a_parts, axis=1)
    u = jnp.concatenate(u_parts, axis=1)
    h = u + a * h0_ref[...]
    h_ref[...] = h
    y_ref[...] = (gate_ref[...].astype(F32) * h).astype(BF16)


def lru_step(gate_act, xr, h0, cbuf, conv_w, conv_b, wg, b_a, b_i, lam):
    m = xr.shape[0]
    return pl.pallas_call(
        _lru_step_kernel,
        out_shape=[jax.ShapeDtypeStruct((m, D_RNN), BF16), jax.ShapeDtypeStruct((m, D_RNN), F32)],
        compiler_params=pltpu.CompilerParams(vmem_limit_bytes=V7X_VMEM_LIMIT),
        name="lru_step",
    )(gate_act, xr, h0, cbuf, conv_w, conv_b.reshape(1, D_RNN), wg, b_a.reshape(1, D_RNN), b_i.reshape(1, D_RNN),
      lam.reshape(1, D_RNN))


def _lru_gate_weights(w_a, w_i):
    per = LRU_GROUP // LRU_BLOCK
    row_blk = np.arange(LRU_GROUP)[:, None] // LRU_BLOCK
    col_blk = np.arange(2 * LRU_GROUP)[None, :] % LRU_GROUP // LRU_BLOCK
    on_diagonal = jnp.asarray(row_blk == col_blk)
    rows = lambda w: w.astype(BF16).reshape(N_LRU_GROUPS, LRU_GROUP, LRU_BLOCK)
    tiled = jnp.concatenate([jnp.tile(rows(w_a), (1, 1, per)), jnp.tile(rows(w_i), (1, 1, per))], axis=2)
    return jnp.where(on_diagonal, tiled, jnp.zeros((), BF16))


def _rope_tables(pos):
    half = ROT_DIM // 2
    inv = ROPE_THETA ** (-jnp.arange(half, dtype=F32) * (2.0 / ROT_DIM))
    ang = pos[:, None] * inv[None, :]
    cos, sin = jnp.cos(ang), jnp.sin(ang)
    rows = pos.shape[0]
    ones = jnp.ones((rows, HEAD_DIM - ROT_DIM), F32)
    zeros_h = jnp.zeros((rows, half), F32)
    zeros_t = jnp.zeros((rows, HEAD_DIM - ROT_DIM), F32)
    c = jnp.concatenate([cos, cos, ones], axis=1)
    s_lo = jnp.concatenate([-sin, zeros_h, zeros_t], axis=1)
    s_hi = jnp.concatenate([zeros_h, sin, zeros_t], axis=1)
    rep = LANES // HEAD_DIM
    return tuple(jnp.tile(t, (1, rep)) for t in (c, s_lo, s_hi))


def _rope_lanes(x, tables):
    c, s_lo, s_hi = tables
    half = ROT_DIM // 2
    return x * c + pltpu.roll(x, LANES - half, 1) * s_lo + pltpu.roll(x, half, 1) * s_hi


def _rope_kernel(qk_ref, c_ref, slo_ref, shi_ref, q_ref, k_ref):
    tables = (c_ref[...], slo_ref[...], shi_ref[...])
    n_q = N_HEADS * HEAD_DIM // LANES
    for ci in range(QK_COLS // LANES):
        rot = _rope_lanes(qk_ref[:, ci * LANES:(ci + 1) * LANES], tables)
        if ci < n_q:
            q_ref[:, ci * LANES:(ci + 1) * LANES] = rot.astype(BF16)
        else:
            k_ref[:, (ci - n_q) * LANES:(ci - n_q + 1) * LANES] = rot


def rope(qkv, tables, *, tm):
    m = qkv.shape[0]
    nt = tables[0].shape[0] // tm
    tab = lambda: pl.BlockSpec((tm, LANES), lambda i: (i % nt, 0))
    return pl.pallas_call(
        _rope_kernel,
        grid=(m // tm,),
        in_specs=[pl.BlockSpec((tm, QK_COLS), lambda i: (i, 0)), tab(), tab(), tab()],
        out_specs=[pl.BlockSpec((tm, N_HEADS * HEAD_DIM), lambda i: (i, 0)),
                   pl.BlockSpec((tm, KV_COLS), lambda i: (i, 0))],
        out_shape=[jax.ShapeDtypeStruct((m, N_HEADS * HEAD_DIM), BF16), jax.ShapeDtypeStruct((m, KV_COLS), F32)],
        compiler_params=_params(1),
        name="rope",
    )(qkv, *tables)


def _swa_seq_kernel(sink_ref, q_ref, kp_ref, kc_ref, vp_ref, vc_ref, cp_ref, slop_ref, ship_ref,
                    cc_ref, sloc_ref, shic_ref, o_ref, krot_ref):
    qi = pl.program_id(1)
    tq = q_ref.shape[0]
    nk = 2 * tq
    rope_prev = (cp_ref[...], slop_ref[...], ship_ref[...])
    rope_cur = (cc_ref[...], sloc_ref[...], shic_ref[...])
    lane_blocks = [slice(i * LANES, (i + 1) * LANES) for i in range(KV_COLS // LANES)]
    k_prev = jnp.concatenate([_rope_lanes(kp_ref[:, blk], rope_prev) for blk in lane_blocks], axis=1)
    k_cur = jnp.concatenate([_rope_lanes(kc_ref[:, blk], rope_cur) for blk in lane_blocks], axis=1)
    krot_ref[...] = k_cur
    kk = jnp.concatenate([k_prev, k_cur], axis=0) * (HEAD_DIM ** -0.5)
    vv = jnp.concatenate([vp_ref[...], vc_ref[...]], axis=0).astype(BF16)
    key = lax.broadcasted_iota(jnp.int32, (nk, tq), 0)
    qry = lax.broadcasted_iota(jnp.int32, (nk, tq), 1)
    rel = tq + qry - key
    valid = (rel >= 0) & (rel < WINDOW) & ((qi > 0) | (key >= tq))
    low_k = lax.broadcasted_iota(jnp.int32, (nk, LANES), 1) < HEAD_DIM
    low_o = lax.broadcasted_iota(jnp.int32, (tq, LANES), 1) < HEAD_DIM
    for kh in range(N_KV):
        blk = slice((kh // 2) * LANES, (kh // 2 + 1) * LANES)
        in_low = kh % 2 == 0
        k_blk = kk[:, blk]
        k_swap = pltpu.roll(k_blk, HEAD_DIM, 1)
        k_lo = jnp.where(low_k, k_blk if in_low else k_swap, 0.0).astype(BF16)
        k_hi = jnp.where(low_k, 0.0, k_swap if in_low else k_blk).astype(BF16)
        v_blk = vv[:, blk]
        for hp in range(GROUP // 2):
            h0 = kh * GROUP + 2 * hp
            hcols = slice(h0 * HEAD_DIM, (h0 + 2) * HEAD_DIM)
            q_pair = _rope_lanes(q_ref[:, hcols], rope_cur).astype(BF16)
            outs = []
            for which, k_pad in enumerate((k_lo, k_hi)):
                s = jnp.where(valid, _dot_nt(k_pad, q_pair), -jnp.inf)
                sink = sink_ref[h0 + which]
                mx = jnp.maximum(jnp.max(s, axis=0, keepdims=True), sink)
                e = jnp.exp(s - mx)
                denom = jnp.sum(e, axis=0, keepdims=True) + jnp.exp(sink - mx)
                p = (e * (1.0 / denom)).astype(BF16)
                outs.append(_dot_tn(p, v_blk))
            if in_low:
                o_pair = jnp.where(low_o, outs[0], pltpu.roll(outs[1], HEAD_DIM, 1))
            else:
                o_pair = jnp.where(low_o, pltpu.roll(outs[0], HEAD_DIM, 1), outs[1])
            o_ref[:, hcols] = o_pair.astype(BF16)


def swa_seq(qkv, rope_tables, sinks, *, batch, seq_len):
    m = qkv.shape[0]
    tq = WINDOW
    nq = seq_len // tq
    k_col = N_HEADS * HEAD_DIM // KV_COLS
    v_col = QK_COLS // KV_COLS
    prev = lambda col: pl.BlockSpec((tq, KV_COLS), lambda b, i: (b * nq + jnp.maximum(i - 1, 0), col))
    cur = lambda col: pl.BlockSpec((tq, KV_COLS), lambda b, i: (b * nq + i, col))
    tab_prev = lambda: pl.BlockSpec((tq, LANES), lambda b, i: (jnp.maximum(i - 1, 0), 0))
    tab_cur = lambda: pl.BlockSpec((tq, LANES), lambda b, i: (i, 0))
    return pl.pallas_call(
        _swa_seq_kernel,
        grid=(batch, nq),
        in_specs=[
            pl.BlockSpec(memory_space=pltpu.SMEM),
            pl.BlockSpec((tq, N_HEADS * HEAD_DIM), lambda b, i: (b * nq + i, 0)),
            prev(k_col), cur(k_col), prev(v_col), cur(v_col),
            tab_prev(), tab_prev(), tab_prev(), tab_cur(), tab_cur(), tab_cur(),
        ],
        out_specs=[pl.BlockSpec((tq, N_HEADS * HEAD_DIM), lambda b, i: (b * nq + i, 0)),
                   pl.BlockSpec((tq, KV_COLS), lambda b, i: (b * nq + i, 0))],
        out_shape=[jax.ShapeDtypeStruct((m, N_HEADS * HEAD_DIM), BF16), jax.ShapeDtypeStruct((m, KV_COLS), F32)],
        compiler_params=_params(2),
        name="swa_seq",
    )(sinks, qkv, qkv, qkv, qkv, qkv, *rope_tables, *rope_tables)


def _swa_step_kernel(q_ref, kn_ref, vn_ref, ck_ref, cv_ref, sink_ref, o_ref):
    col = lax.broadcasted_iota(jnp.int32, (GROUP, WINDOW), 1)
    scale = HEAD_DIM ** -0.5
    items = [(bi, kh) for bi in range(q_ref.shape[0]) for kh in range(N_KV)]
    heads = lambda kh: slice(kh * GROUP, (kh + 1) * GROUP)
    kcols = lambda kh: slice(kh * HEAD_DIM, (kh + 1) * HEAD_DIM)
    scores = []
    for bi, kh in items:
        q = q_ref[bi, heads(kh), :]
        k_new = kn_ref[bi, kh:kh + 1, :].astype(BF16).astype(F32)
        s_c = _dot_nt(q, ck_ref[bi, :, kcols(kh)].astype(BF16)) * scale
        s_c = jnp.where(col >= 1, s_c, -jnp.inf)
        s_n = jnp.sum(q.astype(F32) * k_new, axis=-1, keepdims=True) * scale
        scores.append((s_c, s_n))
    probs = []
    for (bi, kh), (s_c, s_n) in zip(items, scores):
        sink = sink_ref[heads(kh), :]
        mx = jnp.maximum(jnp.maximum(jnp.max(s_c, axis=-1, keepdims=True), s_n), sink)
        p_c = jnp.exp(s_c - mx)
        p_n = jnp.exp(s_n - mx)
        denom = jnp.sum(p_c, axis=-1, keepdims=True) + p_n + jnp.exp(sink - mx)
        probs.append((p_c.astype(BF16), p_n.astype(BF16).astype(F32), denom))
    for (bi, kh), (p_c, p_n, denom) in zip(items, probs):
        v_new = vn_ref[bi, kh:kh + 1, :].astype(BF16).astype(F32)
        o = _dot(p_c, cv_ref[bi, :, kcols(kh)].astype(BF16)) + p_n * v_new
        o_ref[bi, heads(kh), :] = (o / denom).astype(BF16)


def swa_step(q3, k_new, v_new, cache_k, cache_v, sinks, *, seqs_per_step=8):
    b = q3.shape[0]
    nb = seqs_per_step
    return pl.pallas_call(
        _swa_step_kernel,
        grid=(b // nb,),
        in_specs=[
            pl.BlockSpec((nb, N_HEADS, HEAD_DIM), lambda i: (i, 0, 0)),
            pl.BlockSpec((nb, N_KV, HEAD_DIM), lambda i: (i, 0, 0)),
            pl.BlockSpec((nb, N_KV, HEAD_DIM), lambda i: (i, 0, 0)),
            pl.BlockSpec((nb, WINDOW, KV_COLS), lambda i: (i, 0, 0)),
            pl.BlockSpec((nb, WINDOW, KV_COLS), lambda i: (i, 0, 0)),
            pl.BlockSpec((N_HEADS, 1), lambda i: (0, 0)),
        ],
        out_specs=pl.BlockSpec((nb, N_HEADS, HEAD_DIM), lambda i: (i, 0, 0)),
        out_shape=jax.ShapeDtypeStruct((b, N_HEADS, HEAD_DIM), BF16),
        compiler_params=_params(1),
        name="swa_step",
    )(q3, k_new, v_new, cache_k, cache_v, sinks.reshape(N_HEADS, 1))


def _hgrn_lower_bound(logits, layer):
    mx = jnp.max(logits, axis=0, keepdims=True)
    e = jnp.exp(logits - mx)
    sm = e / jnp.sum(e, axis=0, keepdims=True)
    lb = jnp.zeros_like(sm[0:1])
    for i in range(1, layer + 1):
        lb = lb + sm[i:i + 1]
    return lb


def _hgrn_consts():
    c = HG_CHUNK
    t = np.arange(c)[:, None]
    s = np.arange(c)[None, :]
    tri = (s <= t).astype(np.float32)
    sel = [tri]
    msk = [(s == t)]
    for lvl in range(1, HG_LEVELS + 1):
        w = 1 << (lvl - 1)
        ref_row = (t // (2 * w)) * (2 * w) + w - 1
        upper_t = (t % (2 * w)) >= w
        lower_s = (s % (2 * w)) < w
        sign = np.where(upper_t, 1.0, -1.0)
        sel.append(sign * (tri - (s <= ref_row)))
        msk.append(((t // (2 * w)) == (s // (2 * w))) & upper_t & lower_s)
    sel.append(1.0 - tri)
    sel = np.concatenate(sel, axis=0).astype(np.float32)
    sel2 = np.concatenate([sel, sel], axis=1)
    msk = np.stack(msk, axis=0).astype(np.float32)
    return jnp.asarray(sel2, BF16), jnp.asarray(msk, F32)


def _hgrn_seq_kernel(q_ref, z_ref, v_ref, g_ref, lbl_ref, gn_ref, sel_ref, msk_ref, y_ref, s_out_ref, st_ref,
                     *, layer, heads_per_step):
    ti = pl.program_id(2)
    tt = q_ref.shape[0]
    c = HG_CHUNK

    @pl.when(ti == 0)
    def _():
        st_ref[...] = jnp.zeros(st_ref.shape, F32)

    lb_all = _hgrn_lower_bound(lbl_ref[...], layer)
    gn = gn_ref[...]
    row = lax.broadcasted_iota(jnp.int32, (c, heads_per_step * HG_DK), 0)
    upper = [None] + [((row >> (lvl - 1)) & 1) == 1 for lvl in range(1, HG_LEVELS + 1)]
    states = [st_ref[hh] for hh in range(heads_per_step)]

    head_cols = [slice(hh * HG_DK, (hh + 1) * HG_DK) for hh in range(heads_per_step)]

    def decays(ci):
        rows = slice(ci * c, (ci + 1) * c)
        q = q_ref[rows, :]
        sq = q * _sigmoid(q)
        sz = _sigmoid(z_ref[rows, :])
        log_f = jnp.log(lb_all + (1.0 - lb_all) * sz)
        k = (1.0 - lb_all) * (1.0 - sz)
        hi = log_f.astype(BF16)
        lo = (log_f - hi.astype(F32)).astype(BF16)
        ex = _dot(sel_ref[...], jnp.concatenate([hi, lo], axis=0))
        e_b = jnp.exp(ex[0:c])
        xs = [(jnp.where(upper[lvl], sq, k) * jnp.exp(ex[lvl * c:(lvl + 1) * c])).astype(BF16)
              for lvl in range(1, HG_LEVELS + 1)]
        return dict(rows=rows, v=v_ref[rows, :].astype(BF16), sq=sq.astype(BF16), k=k.astype(BF16), xs=xs,
                    q_dec=(sq * e_b).astype(BF16), k_dec=(k * jnp.exp(ex[(HG_LEVELS + 1) * c:])).astype(BF16),
                    e_last=e_b[c - 1:c, :])

    def scores(d):
        atts = []
        for cols in head_cols:
            att = _dot_nt(d["sq"][:, cols], d["k"][:, cols]) * msk_ref[0]
            for lvl in range(1, HG_LEVELS + 1):
                x = d["xs"][lvl - 1][:, cols]
                att = att + _dot_nt(x, x) * msk_ref[lvl]
            atts.append(att.astype(BF16))
        return atts

    def outputs(d, atts):
        outs = []
        for hh, cols in enumerate(head_cols):
            st = states[hh]
            o = _dot(atts[hh], d["v"][:, cols]) + _dot_nt(d["q_dec"][:, cols], st.astype(BF16))
            states[hh] = st * d["e_last"][:, cols] + _dot_tn(d["v"][:, cols], d["k_dec"][:, cols])
            outs.append(_rms_rows(o, gn))
        g = g_ref[d["rows"], :]
        y_ref[d["rows"], :] = (jnp.concatenate(outs, axis=1) * (g * _sigmoid(g))).astype(BF16)

    n_chunks = tt // c
    stage_a, stage_b = {}, {}
    for step in range(n_chunks + 2):
        if step < n_chunks:
            stage_a[step] = decays(step)
        if 0 <= step - 1 < n_chunks:
            stage_b[step - 1] = scores(stage_a[step - 1])
        if 0 <= step - 2 < n_chunks:
            outputs(stage_a.pop(step - 2), stage_b.pop(step - 2))

    for hh in range(heads_per_step):
        st_ref[hh] = states[hh]

    @pl.when(ti == pl.num_programs(2) - 1)
    def _():
        for hh in range(heads_per_step):
            s_out_ref[hh] = states[hh].T


def hgrn_seq(qzvg, lb_logits, g_norm, *, layer, batch, seq_len, tt, heads_per_step):
    m = qzvg.shape[0]
    nt = seq_len // tt
    hw = heads_per_step * HG_DK
    nh = HG_HEADS // heads_per_step
    sel, msk = _hgrn_consts()
    part = lambda p: pl.BlockSpec((tt, hw), lambda b, h, t: (b * nt + t, p * nh + h))
    y, s_out = pl.pallas_call(
        functools.partial(_hgrn_seq_kernel, layer=layer, heads_per_step=heads_per_step),
        grid=(batch, nh, nt),
        in_specs=[
            part(0), part(1), part(2), part(3),
            pl.BlockSpec((DEPTH, hw), lambda b, h, t: (0, h)),
            pl.BlockSpec((1, HG_DV), lambda b, h, t: (0, 0)),
            pl.BlockSpec(sel.shape, lambda b, h, t: (0, 0)),
            pl.BlockSpec(msk.shape, lambda b, h, t: (0, 0, 0)),
        ],
        out_specs=[
            pl.BlockSpec((tt, hw), lambda b, h, t: (b * nt + t, h)),
            pl.BlockSpec((None, heads_per_step, HG_DK, HG_DV), lambda b, h, t: (b, h, 0, 0)),
        ],
        out_shape=[jax.ShapeDtypeStruct((m, HG_HEADS * HG_DV), BF16),
                   jax.ShapeDtypeStruct((batch, HG_HEADS, HG_DK, HG_DV), F32)],
        scratch_shapes=[pltpu.VMEM((heads_per_step, HG_DV, HG_DK), F32)],
        compiler_params=_params(3),
        name="hgrn_seq",
    )(qzvg, qzvg, qzvg, qzvg, lb_logits, g_norm.reshape(1, HG_DV), sel, msk)
    return y, s_out


def _hgrn_step_kernel(x_ref, s_ref, lbl_ref, gn_ref, y_ref, s_out_ref, *, layer):
    nh = HG_HEADS
    lb = _hgrn_lower_bound(lbl_ref[...], layer)[0]
    pad = jnp.zeros((LANES - 3 * nh, HG_DK), F32)
    for bi in range(x_ref.shape[0]):
        q = x_ref[bi, 0:nh, :]
        z = x_ref[bi, nh:2 * nh, :]
        v = x_ref[bi, 2 * nh:3 * nh, :]
        g = x_ref[bi, 3 * nh:4 * nh, :]
        sq = q * _sigmoid(q)
        sz = _sigmoid(z)
        f = lb + (1.0 - lb) * sz
        k = (1.0 - lb) * (1.0 - sz)
        cols = jnp.concatenate([f, k, sq, pad], axis=0).T
        outs = []
        for h in range(nh):
            f_col = cols[:, h:h + 1]
            k_col = cols[:, nh + h:nh + h + 1]
            q_col = cols[:, 2 * nh + h:2 * nh + h + 1]
            s_new = s_ref[bi, h] * f_col + k_col * v[h:h + 1, :]
            s_out_ref[bi, h] = s_new
            outs.append(jnp.sum(s_new * q_col, axis=0, keepdims=True))
        o = jnp.concatenate(outs, axis=0)
        y_ref[bi] = (_rms_rows(o, gn_ref[...]) * (g * _sigmoid(g))).astype(BF16)


def hgrn_step(x4, state, lb_logits, g_norm, *, layer, seqs_per_step=4):
    b = x4.shape[0]
    nb = seqs_per_step
    return pl.pallas_call(
        functools.partial(_hgrn_step_kernel, layer=layer),
        grid=(b // nb,),
        in_specs=[
            pl.BlockSpec((nb, 4 * HG_HEADS, HG_DK), lambda i: (i, 0, 0)),
            pl.BlockSpec((nb, HG_HEADS, HG_DK, HG_DV), lambda i: (i, 0, 0, 0)),
            pl.BlockSpec((DEPTH, HG_HEADS, HG_DK), lambda i: (0, 0, 0)),
            pl.BlockSpec((1, HG_DV), lambda i: (0, 0)),
        ],
        out_specs=[
            pl.BlockSpec((nb, HG_HEADS, HG_DV), lambda i: (i, 0, 0)),
            pl.BlockSpec((nb, HG_HEADS, HG_DK, HG_DV), lambda i: (i, 0, 0, 0)),
        ],
        out_shape=[jax.ShapeDtypeStruct((b, HG_HEADS, HG_DV), BF16),
                   jax.ShapeDtypeStruct((b, HG_HEADS, HG_DK, HG_DV), F32)],
        compiler_params=_params(1),
        name="hgrn_step",
    )(x4, state, lb_logits.reshape(DEPTH, HG_HEADS, HG_DK), g_norm.reshape(1, HG_DV))


def _trunk(x3, xs3, states, w):
    batch, seq_len, _ = x3.shape
    m = batch * seq_len
    ns = xs3.shape[0]
    x = x3.reshape(m, D_MODEL)
    xs = xs3.reshape(1, ns, D_MODEL)
    tm = seq_len
    tn_in, tn_out, tf = 1024, 512, 512
    zeros = lambda n: jnp.zeros((n,), F32)
    lru_wg = [_lru_gate_weights(w["lru_w_a"][j], w["lru_w_i"][j]) for j in range(w["lru_w_a"].shape[0])]
    rope_seq = _rope_tables(jnp.arange(seq_len, dtype=F32))
    rope_step = _rope_tables(jnp.full((ns,), PAST_LEN, F32))

    def out_proj(a, a_s, w_stack, li, b, res, res_s):
        k_blocks = 1
        while tm * (a.shape[1] // k_blocks) * a.dtype.itemsize >= SINGLE_BUFFER_BYTES:
            k_blocks *= 2
        k = a.shape[1] // k_blocks
        cap = max(LANES, min(tn_out, W_TILE_BYTES // (4 * k) // LANES * LANES))
        for ki in range(k_blocks):
            bias = b if ki == k_blocks - 1 else jnp.zeros_like(b)
            res, res_s = matmul_residual(a, a_s, w_stack, li, bias, res, res_s, tm=tm,
                                         tn=_largest_tile(D_MODEL, cap), k_blocks=k_blocks, k_index=ki)
        return res, res_s

    new_p = {"lru_h": [], "lru_conv": [], "swa_k": [], "swa_v": [], "hgrn": [], "ffn_conv": []}
    new_s = {"lru_h": [], "lru_conv": [], "swa_k": [], "swa_v": [], "hgrn": [], "ffn_conv": []}
    for layer in range(DEPTH):
        kind, j = LAYER_MIXER[layer], LAYER_SLOT[layer]
        g_mix = w["norm_mix"][layer]
        if kind == 0:
            gate, gate_s, xr, xr_s = norm_matmul_gated(x, xs, g_mix, w["lru_w_in"], j, tm=tm, tn=tn_in // 2)
            xr_s = xr_s[0]
            lru_w = (w["lru_conv_w"][j], w["lru_conv_b"][j], lru_wg[j], w["lru_b_a"][j], w["lru_b_i"][j],
                     w["lru_lambda"][j])
            y, h_new = lru_seq(gate, xr, *lru_w, batch=batch, seq_len=seq_len, tt=256)
            new_p["lru_h"].append(h_new)
            new_p["lru_conv"].append(xr.reshape(batch, seq_len, D_RNN)[:, seq_len - (CONV_A - 1):])
            cbuf = states["lru_conv"][j]
            ys, hs_new = lru_step(gate_s[0], xr_s, states["lru_h"][j], jnp.swapaxes(cbuf, 0, 1), *lru_w)
            new_s["lru_h"].append(hs_new)
            new_s["lru_conv"].append(jnp.concatenate([cbuf[:, 1:], xr_s[:, None]], axis=1))
            x, xs = out_proj(y, ys[None], w["lru_w_out"], j, zeros(D_MODEL), x, xs)
        elif kind == 1:
            n = QK_COLS + KV_COLS
            qkv, qkvs = norm_matmul(x, xs, g_mix, w["swa_w_qkv"], j, w["swa_b_qkv"][j], tm=tm,
                                    tn=_largest_tile(n, tn_in))
            qkvs = qkvs[0]
            o, k_rot = swa_seq(qkv, rope_seq, w["swa_sinks"][j], batch=batch, seq_len=seq_len)
            k_win = k_rot.reshape(batch, seq_len, KV_COLS)[:, seq_len - WINDOW:]
            v_win = qkv.reshape(batch, seq_len, n)[:, seq_len - WINDOW:, QK_COLS:]
            new_p["swa_k"].append(k_win.reshape(batch, WINDOW, N_KV, HEAD_DIM))
            new_p["swa_v"].append(v_win.reshape(batch, WINDOW, N_KV, HEAD_DIM))
            qs_rot, ks_rot = rope(qkvs, rope_step, tm=ns)
            ck = states["swa_k"][j]
            cv = states["swa_v"][j]
            k_new = ks_rot.reshape(ns, N_KV, HEAD_DIM)
            v_new = qkvs[:, QK_COLS:].reshape(ns, N_KV, HEAD_DIM)
            os_ = swa_step(qs_rot.reshape(ns, N_HEADS, HEAD_DIM), k_new, v_new,
                           ck.reshape(ns, WINDOW, KV_COLS), cv.reshape(ns, WINDOW, KV_COLS), w["swa_sinks"][j])
            new_s["swa_k"].append(jnp.concatenate([ck[:, 1:], k_new[:, None]], axis=1))
            new_s["swa_v"].append(jnp.concatenate([cv[:, 1:], v_new[:, None]], axis=1))
            x, xs = out_proj(o, os_.reshape(1, ns, N_HEADS * HEAD_DIM), w["swa_w_o"], j, w["swa_b_o"][j], x, xs)
        else:
            n = 2 * HG_HEADS * HG_DK + 2 * HG_HEADS * HG_DV
            qzvg, qzvgs = norm_matmul(x, xs, g_mix, w["hg_w_in"], j, zeros(n), tm=tm, tn=_largest_tile(n, tn_in))
            y, s_new = hgrn_seq(qzvg, w["hg_lb_logits"], w["hg_norm"][j], layer=layer, batch=batch,
                                seq_len=seq_len, tt=1024, heads_per_step=4)
            new_p["hgrn"].append(s_new)
            ys, ss_new = hgrn_step(qzvgs[0].reshape(ns, 4 * HG_HEADS, HG_DK), states["hgrn"][j], w["hg_lb_logits"],
                                   w["hg_norm"][j], layer=layer)
            new_s["hgrn"].append(ss_new)
            x, xs = out_proj(y, ys.reshape(1, ns, HG_HEADS * HG_DV), w["hg_w_o"], j, zeros(D_MODEL), x, xs)

        fbuf = states["ffn_conv"][layer]
        act, act_s, tail, gate_s = ffn_in(x, xs, w["norm_ffn"][layer], w["ffn_w_in"], w["ffn_conv_w"],
                                          w["ffn_conv_b"], layer, jnp.swapaxes(fbuf, 0, 1), tm=tm, tf=tf)
        new_p["ffn_conv"].append(tail[:, SUBLANES - (CONV_F - 1):])
        new_s["ffn_conv"].append(jnp.concatenate([fbuf[:, 1:], gate_s[0][:, None]], axis=1))
        x, xs = out_proj(act, act_s, w["ffn_w_out"], layer, zeros(D_MODEL), x, xs)

    y = final_norm(x, w["norm_final"], tm=1024).reshape(batch, seq_len, D_MODEL)
    ys = final_norm(xs[0], w["norm_final"], tm=ns).reshape(ns, 1, D_MODEL)
    order = ("lru_h", "lru_conv", "swa_k", "swa_v", "hgrn", "ffn_conv")
    return ((y, ys) + tuple(jnp.stack(new_p[k]) for k in order) + tuple(jnp.stack(new_s[k]) for k in order))


def kernel(x_prompt, x_sample, state_lru_h, state_lru_conv, cache_swa_k, cache_swa_v, state_hgrn, state_ffn_conv,
           norm_mix, norm_ffn, norm_final,
           lru_w_in, lru_conv_w, lru_conv_b, lru_w_a, lru_b_a, lru_w_i, lru_b_i, lru_lambda, lru_w_out,
           swa_w_qkv, swa_b_qkv, swa_sinks, swa_w_o, swa_b_o,
           hg_w_in, hg_lb_logits, hg_norm, hg_w_o,
           ffn_w_in, ffn_conv_w, ffn_conv_b, ffn_w_out):
    w = dict(norm_mix=norm_mix, norm_ffn=norm_ffn, norm_final=norm_final,
             lru_w_in=lru_w_in, lru_conv_w=lru_conv_w, lru_conv_b=lru_conv_b, lru_w_a=lru_w_a, lru_b_a=lru_b_a,
             lru_w_i=lru_w_i, lru_b_i=lru_b_i, lru_lambda=lru_lambda, lru_w_out=lru_w_out,
             swa_w_qkv=swa_w_qkv, swa_b_qkv=swa_b_qkv, swa_sinks=swa_sinks, swa_w_o=swa_w_o, swa_b_o=swa_b_o,
             hg_w_in=hg_w_in, hg_lb_logits=hg_lb_logits, hg_norm=hg_norm, hg_w_o=hg_w_o,
             ffn_w_in=ffn_w_in, ffn_conv_w=ffn_conv_w, ffn_conv_b=ffn_conv_b, ffn_w_out=ffn_w_out)
    states = dict(lru_h=state_lru_h, lru_conv=state_lru_conv, swa_k=cache_swa_k, swa_v=cache_swa_v,
                  hgrn=state_hgrn, ffn_conv=state_ffn_conv)
    return _trunk(x_prompt, x_sample, states, w)
```
